```python
import jax, jax.numpy as jnp
from jax import lax
import numpy as np

D_MODEL = 1024
BATCH = 16
SEQ = 2048
DEPTH = 2

CHUNK = 64
N_MIXERS = 2
N_GLA_LAYERS = (DEPTH + 1) // 2
N_SGU_LAYERS = DEPTH // 2
EPS = 1e-6

GLA_HEADS = 4
GLA_DK = D_MODEL // 2 // GLA_HEADS
GLA_DV = D_MODEL // GLA_HEADS
GLA_HK = GLA_HEADS * GLA_DK
GLA_HV = GLA_HEADS * GLA_DV
GLA_GATE_RANK = 16
GLA_TAU = 16.0
GLA_IN = 2 * GLA_HK + 2 * GLA_HV + GLA_GATE_RANK

SGU_BLOCK = 128
SGU_GROUPS = 4
SGU_FFN = 4 * D_MODEL
SGU_HALF = SGU_FFN // 2
SGU_GC = SGU_HALF // SGU_GROUPS

N_GROUPS = 4
EXPERTS_PER_GROUP = 8
N_EXPERTS = N_GROUPS * EXPERTS_PER_GROUP
TOP_K = 2
EXPERT_FF = D_MODEL // 2
ROUTE_BLOCK = 128

kernel_name = "hybrid_gla_sgu_hier_moe"


def rmsnorm(x, g):
    xf = x.astype(jnp.float32)
    y = xf * lax.rsqrt(jnp.mean(xf * xf, axis=-1, keepdims=True) + EPS)
    return (y * g.astype(jnp.float32)).astype(x.dtype)


def layernorm(x, g, b):
    xf = x.astype(jnp.float32)
    mu = jnp.mean(xf, axis=-1, keepdims=True)
    xc = xf - mu
    y = xc * lax.rsqrt(jnp.mean(xc * xc, axis=-1, keepdims=True) + EPS)
    return (y * g.astype(jnp.float32) + b.astype(jnp.float32)).astype(x.dtype)


def gla_mixer(h, w_in, w_gate_up, b_gate, head_g, w_out):
    B, S, _ = h.shape
    N = S // CHUNK
    proj = h @ w_in
    q, k, v, r, glr = jnp.split(
        proj, [GLA_HK, 2 * GLA_HK, 2 * GLA_HK + GLA_HV, 2 * GLA_HK + 2 * GLA_HV], axis=-1)
    log_a = jax.nn.log_sigmoid((glr @ w_gate_up + b_gate).astype(jnp.float32)) / GLA_TAU
    scale = GLA_DK ** -0.5

    def to_chunks(t, d):
        return t.astype(jnp.float32).reshape(B, N, CHUNK, GLA_HEADS, d).transpose(1, 0, 3, 2, 4)

    qc = to_chunks(q, GLA_DK) * scale
    kc = to_chunks(k, GLA_DK)
    vc = to_chunks(v, GLA_DV)
    gc = to_chunks(log_a, GLA_DK)

    def step(state, inp):
        q_, k_, v_, g_ = inp
        b = jnp.cumsum(g_, axis=2)
        b_end = b[:, :, -1:, :]
        k_dec = k_ * jnp.exp(b_end - b)
        att = jnp.einsum('bhtk,bhsk->bhts', q_, k_dec)
        o = (jnp.einsum('bhtk,bhkv->bhtv', q_ * jnp.exp(b_end), state)
             + jnp.einsum('bhts,bhsv->bhtv', att, v_))
        new_state = (jnp.exp(b_end)[:, :, 0, :, None] * state
                     + jnp.einsum('bhsk,bhsv->bhkv', k_dec, v_))
        return new_state, o

    s0 = jnp.zeros((B, GLA_HEADS, GLA_DK, GLA_DV), jnp.float32)
    _, o = lax.scan(step, s0, (qc, kc, vc, gc))
    o = o.transpose(1, 0, 3, 2, 4).reshape(B, S, GLA_HEADS, GLA_DV)
    o = rmsnorm(o, head_g).reshape(B, S, GLA_HV)
    o = o * jax.nn.silu(r.astype(jnp.float32))
    return o.astype(h.dtype) @ w_out


def sgu_mixer(h, w_in, ln_g, ln_b, w_s, b_s, w_out):
    B, S, _ = h.shape
    z = jax.nn.gelu(h @ w_in, approximate=False)
    u, v = jnp.split(z, 2, axis=-1)
    v = layernorm(v, ln_g, ln_b)
    pos = jnp.arange(SGU_BLOCK)
    mask = (pos[:, None] // CHUNK) >= (pos[None, :] // CHUNK)
    ws = jnp.where(mask[None], w_s, jnp.zeros_like(w_s))
    vb = v.reshape(B, S // SGU_BLOCK, SGU_BLOCK, SGU_GROUPS, SGU_GC)
    mixed = jnp.einsum('gts,bnsgc->bntgc', ws, vb) + b_s.T[None, None, :, :, None]
    out = u * mixed.reshape(B, S, SGU_HALF)
    return out @ w_out


def routed_experts(xt, expert, gate, w1, w3, w2):
    T, D = xt.shape
    A = T * TOP_K
    NB = -(-A // ROUTE_BLOCK) + N_EXPERTS
    flat_e = expert.reshape(A).astype(jnp.int32)
    flat_tok = jnp.arange(A, dtype=jnp.int32) // TOP_K
    flat_gate = gate.reshape(A)
    order = jnp.argsort(flat_e)
    e_sorted = flat_e[order]
    counts = jnp.bincount(flat_e, length=N_EXPERTS).astype(jnp.int32)
    padded = ((counts + ROUTE_BLOCK - 1) // ROUTE_BLOCK) * ROUTE_BLOCK
    pad_end = jnp.cumsum(padded)
    pad_start = pad_end - padded
    start = jnp.cumsum(counts) - counts
    dest = pad_start[e_sorted] + jnp.arange(A, dtype=jnp.int32) - start[e_sorted]
    buf_tok = jnp.full((NB * ROUTE_BLOCK,), T, jnp.int32).at[dest].set(flat_tok[order])
    buf_gate = jnp.zeros((NB * ROUTE_BLOCK,), flat_gate.dtype).at[dest].set(flat_gate[order])
    blk_e = jnp.minimum(
        jnp.searchsorted(pad_end, jnp.arange(NB, dtype=jnp.int32) * ROUTE_BLOCK, side='right'),
        N_EXPERTS - 1)
    x_pad = jnp.concatenate([xt, jnp.zeros((1, D), xt.dtype)], axis=0)
    xb = x_pad[buf_tok].reshape(NB, ROUTE_BLOCK, D)

    def run(args):
        xblk, e = args
        hid = jax.nn.silu(xblk @ w1[e]) * (xblk @ w3[e])
        return hid @ w2[e]

    yb = lax.map(run, (xb, blk_e)).reshape(NB * ROUTE_BLOCK, D)
    y = jax.ops.segment_sum(yb * buf_gate[:, None].astype(yb.dtype), buf_tok, num_segments=T + 1)
    return y[:T]


def hier_moe(h, w_group, b_group, w_sub, b_sub, w1, w3, w2):
    B, S, D = h.shape
    T = B * S
    xt = h.reshape(T, D)
    xf = xt.astype(jnp.float32)
    p_group = jax.nn.softmax(xf @ w_group.astype(jnp.float32) + b_group.astype(jnp.float32), axis=-1)
    g_w, g_idx = lax.top_k(p_group, 1)
    sub_logits = (jnp.einsum('td,gde->tge', xf, w_sub.astype(jnp.float32))
                  + b_sub.astype(jnp.float32))
    chosen = jnp.take_along_axis(sub_logits, g_idx[:, :, None], axis=1)[:, 0]
    p_sub = jax.nn.softmax(chosen, axis=-1)
    s_w, s_idx = lax.top_k(p_sub, TOP_K)
    s_w = s_w / jnp.sum(s_w, axis=-1, keepdims=True)
    gate = (g_w * s_w).astype(h.dtype)
    expert = g_idx * EXPERTS_PER_GROUP + s_idx
    return routed_experts(xt, expert, gate, w1, w3, w2).reshape(B, S, D)


def setup_inputs(seed: int = 0) -> dict:
    key = jax.random.key(seed)
    ks = jax.random.split(key, 24)
    f32 = jnp.float32
    nrm = lambda k, shape, s: jax.random.normal(k, shape, f32) * s
    NA, NS = N_GLA_LAYERS, N_SGU_LAYERS
    return {
        "x": nrm(ks[0], (BATCH, SEQ, D_MODEL), 1.0),
        "gla_norm": 1.0 + nrm(ks[1], (NA, D_MODEL), 0.02),
        "gla_w_in": nrm(ks[2], (NA, D_MODEL, GLA_IN), D_MODEL ** -0.5),
        "gla_w_gate_up": nrm(ks[3], (NA, GLA_GATE_RANK, GLA_HK), GLA_GATE_RANK ** -0.5),
        "gla_b_gate": nrm(ks[4], (NA, GLA_HK), 0.1),
        "gla_head_g": 1.0 + nrm(ks[5], (NA, GLA_HEADS, GLA_DV), 0.02),
        "gla_w_out": nrm(ks[6], (NA, GLA_HV, D_MODEL), GLA_HV ** -0.5),
        "sgu_norm": 1.0 + nrm(ks[7], (NS, D_MODEL), 0.02),
        "sgu_w_in": nrm(ks[8], (NS, D_MODEL, SGU_FFN), D_MODEL ** -0.5),
        "sgu_ln_g": 1.0 + nrm(ks[9], (NS, SGU_HALF), 0.02),
        "sgu_ln_b": nrm(ks[10], (NS, SGU_HALF), 0.02),
        "sgu_w_s": nrm(ks[11], (NS, SGU_GROUPS, SGU_BLOCK, SGU_BLOCK), SGU_BLOCK ** -0.5),
        "sgu_b_s": 1.0 + nrm(ks[12], (NS, SGU_GROUPS, SGU_BLOCK), 0.1),
        "sgu_w_out": nrm(ks[13], (NS, SGU_HALF, D_MODEL), SGU_HALF ** -0.5),
        "moe_norm": 1.0 + nrm(ks[14], (DEPTH, D_MODEL), 0.02),
        "moe_w_group": nrm(ks[15], (DEPTH, D_MODEL, N_GROUPS), D_MODEL ** -0.5),
        "moe_b_group": nrm(ks[16], (DEPTH, N_GROUPS), 0.01),
        "moe_w_sub": nrm(ks[17], (DEPTH, N_GROUPS, D_MODEL, EXPERTS_PER_GROUP), D_MODEL ** -0.5),
        "moe_b_sub": nrm(ks[18], (DEPTH, N_GROUPS, EXPERTS_PER_GROUP), 0.01),
        "moe_w1": nrm(ks[19], (DEPTH, N_EXPERTS, D_MODEL, EXPERT_FF), D_MODEL ** -0.5),
        "moe_w3": nrm(ks[20], (DEPTH, N_EXPERTS, D_MODEL, EXPERT_FF), D_MODEL ** -0.5),
        "moe_w2": nrm(ks[21], (DEPTH, N_EXPERTS, EXPERT_FF, D_MODEL), EXPERT_FF ** -0.5),
        "final_norm": 1.0 + nrm(ks[22], (D_MODEL,), 0.02),
    }


def reference(x, gla_norm, gla_w_in, gla_w_gate_up, gla_b_gate, gla_head_g, gla_w_out,
              sgu_norm, sgu_w_in, sgu_ln_g, sgu_ln_b, sgu_w_s, sgu_b_s, sgu_w_out,
              moe_norm, moe_w_group, moe_b_group, moe_w_sub, moe_b_sub, moe_w1, moe_w3, moe_w2,
              final_norm):
    for i in range(DEPTH):
        j = i // N_MIXERS
        if i % N_MIXERS == 0:
            x = x + gla_mixer(rmsnorm(x, gla_norm[j]), gla_w_in[j], gla_w_gate_up[j],
                              gla_b_gate[j], gla_head_g[j], gla_w_out[j])
        else:
            x = x + sgu_mixer(rmsnorm(x, sgu_norm[j]), sgu_w_in[j], sgu_ln_g[j], sgu_ln_b[j],
                              sgu_w_s[j], sgu_b_s[j], sgu_w_out[j])
        x = x + hier_moe(rmsnorm(x, moe_norm[i]), moe_w_group[i], moe_b_group[i], moe_w_sub[i],
                         moe_b_sub[i], moe_w1[i], moe_w3[i], moe_w2[i])
    return rmsnorm(x, final_norm)
```

```python
import functools

import jax
import jax.numpy as jnp
from jax import lax
from jax.experimental import pallas as pl
from jax.experimental.pallas import tpu as pltpu

D_MODEL = 1024
EPS = 1e-6
LANES = 128
SUBLANES = 8
SLAB = D_MODEL // LANES

CHUNK = 64
GLA_HEADS = 4
GLA_DK = 128
GLA_DV = 256
GLA_HK = GLA_HEADS * GLA_DK
GLA_HV = GLA_HEADS * GLA_DV
GLA_GATE_RANK = 16
GLA_TAU = 16.0

SGU_BLOCK = 128
SGU_GROUPS = 4
SGU_HALF = 2048
SGU_GC = SGU_HALF // SGU_GROUPS

N_GROUPS = 4
EXPERTS_PER_GROUP = 8
N_EXPERTS = N_GROUPS * EXPERTS_PER_GROUP
TOP_K = 2
EXPERT_FF = 512
ROUTE_ROWS = 64
META_ROWS = 8

VMEM_LIMIT = 56 * 1024 * 1024

f32 = jnp.float32
bf16 = jnp.bfloat16


def _dot(a, b):
    return jnp.dot(a, b, preferred_element_type=f32)


def _dot_tn(a, b):
    return lax.dot_general(a, b, (((0,), (0,)), ((), ())), preferred_element_type=f32)


def _dot_nt(a, b):
    return lax.dot_general(a, b, (((1,), (1,)), ((), ())), preferred_element_type=f32)


def _rms(x):
    return lax.rsqrt(jnp.mean(x * x, axis=-1, keepdims=True) + EPS)


def _slab_load(ref, n_rows, sub=0, stride=SLAB):
    cols = [ref[pl.ds(sub + j, n_rows, stride=stride), :] for j in range(SLAB)]
    return jnp.concatenate(cols, axis=-1)


def _slab_store(ref, val):
    for j in range(SLAB):
        ref[pl.ds(j, val.shape[0], stride=SLAB), :] = val[:, j * LANES:(j + 1) * LANES]


def _route(x1, norm_g, wr_t, br, meta_ref):
    n = x1.shape[0]
    hn = (x1 * _rms(x1) * norm_g).astype(bf16)
    lt = _dot_nt(wr_t, hn) + br
    rows = lax.broadcasted_iota(jnp.int32, (SUBLANES, n), 0)
    neg = jnp.float32(-jnp.inf)
    lg = jnp.where(rows < N_GROUPS, lt[0:SUBLANES], neg)
    gmax = jnp.max(lg, axis=0, keepdims=True)
    gidx = jnp.min(jnp.where(lg == gmax, rows, SUBLANES), axis=0, keepdims=True)
    g_w = 1.0 / jnp.sum(jnp.exp(lg - gmax), axis=0, keepdims=True)
    chosen = jnp.zeros((SUBLANES, n), f32)
    for g in range(N_GROUPS):
        chosen = jnp.where(gidx == g, lt[SUBLANES * (g + 1):SUBLANES * (g + 2)], chosen)
    m1 = jnp.max(chosen, axis=0, keepdims=True)
    i1 = jnp.min(jnp.where(chosen == m1, rows, SUBLANES), axis=0, keepdims=True)
    rest = jnp.where(rows == i1, neg, chosen)
    m2 = jnp.max(rest, axis=0, keepdims=True)
    i2 = jnp.min(jnp.where(rest == m2, rows, SUBLANES), axis=0, keepdims=True)
    t = jnp.exp(m2 - m1)
    s1 = 1.0 / (1.0 + t)
    s2 = t / (1.0 + t)
    e1 = (gidx * EXPERTS_PER_GROUP + i1).astype(f32)
    e2 = (gidx * EXPERTS_PER_GROUP + i2).astype(f32)
    zero = jnp.zeros((1, n), f32)
    meta_ref[...] = jnp.concatenate([e1, e2, g_w * s1, g_w * s2, zero, zero, zero, zero], axis=0)


def _gla_kernel(x_ref, ng_ref, wq_ref, wk_ref, wv_ref, wr_ref, wg_ref, wgu_ref, bg_ref,
                hg_ref, wo_ref, mng_ref, wrt_ref, br_ref,
                x1_ref, meta_ref, st_ref, o_ref):
    ts = x_ref.shape[1]

    @pl.when(pl.program_id(1) == 0)
    def _():
        st_ref[...] = jnp.zeros_like(st_ref)

    x = x_ref[0]
    h = (x * _rms(x) * ng_ref[...]).astype(bf16)
    q = _dot(h, wq_ref[...]) * (GLA_DK ** -0.5)
    k = _dot(h, wk_ref[...])
    v = _dot(h, wv_ref[...]).astype(bf16)
    glr = _dot(h, wg_ref[...]).astype(bf16)
    gp = _dot(glr, wgu_ref[...]) + bg_ref[...]
    log_a = (jnp.minimum(gp, 0.0) - jnp.log(1.0 + jnp.exp(-jnp.abs(gp)))) * (1.0 / GLA_TAU)

    row = lax.broadcasted_iota(jnp.int32, (CHUNK, GLA_HK), 0)
    for c in range(ts // CHUNK):
        rs = slice(c * CHUNK, (c + 1) * CHUNK)
        b = log_a[rs]
        sh = 1
        while sh < CHUNK:
            b = b + jnp.where(row >= sh, pltpu.roll(b, sh, axis=0), 0.0)
            sh *= 2
        b_end = b[CHUNK - 1:CHUNK]
        kdec = (k[rs] * jnp.exp(b_end - b)).astype(bf16)
        decay = jnp.exp(b_end)
        qc = q[rs].astype(bf16)
        vc = v[rs]
        for hd in range(GLA_HEADS):
            ks = slice(hd * GLA_DK, (hd + 1) * GLA_DK)
            vs = slice(hd * GLA_DV, (hd + 1) * GLA_DV)
            st = st_ref[hd] * decay[:, ks] + _dot_tn(vc[:, vs], kdec[:, ks])
            st_ref[hd] = st
            o_ref[rs, vs] = _dot_nt(qc[:, ks], st.astype(bf16))

    r = _dot(h, wr_ref[...])
    gated = []
    for hd in range(GLA_HEADS):
        vs = slice(hd * GLA_DV, (hd + 1) * GLA_DV)
        oh = o_ref[:, vs]
        rh = r[:, vs]
        gated.append(oh * _rms(oh) * hg_ref[:, vs] * (rh / (1.0 + jnp.exp(-rh))))
    y = _dot(jnp.concatenate(gated, axis=-1).astype(bf16), wo_ref[...])
    x1 = x + y
    _slab_store(x1_ref, x1)
    _route(x1, mng_ref[...], wrt_ref[...], br_ref[...], meta_ref)


def _const_spec(shape):
    return pl.BlockSpec(shape, lambda *_: (0,) * len(shape))


def _router_params(w_group, b_group, w_sub, b_sub):
    wrt = jnp.zeros((ROUTE_ROWS, D_MODEL), f32)
    wrt = wrt.at[0:N_GROUPS].set(w_group.T)
    wrt = wrt.at[SUBLANES:SUBLANES + N_EXPERTS].set(
        jnp.transpose(w_sub, (0, 2, 1)).reshape(N_EXPERTS, D_MODEL))
    br = jnp.zeros((ROUTE_ROWS, 1), f32)
    br = br.at[0:N_GROUPS, 0].set(b_group)
    br = br.at[SUBLANES:SUBLANES + N_EXPERTS, 0].set(b_sub.reshape(N_EXPERTS))
    return wrt.astype(bf16), br


def _gla_layer(x, norm_g, w_in, w_gate_up, b_gate, head_g, w_out, moe_norm_g, wrt, br, ts):
    bsz, seq, _ = x.shape
    t = bsz * seq
    wq = w_in[:, 0:GLA_HK].astype(bf16)
    wk = w_in[:, GLA_HK:2 * GLA_HK].astype(bf16)
    wv = w_in[:, 2 * GLA_HK:2 * GLA_HK + GLA_HV].astype(bf16)
    wr = w_in[:, 2 * GLA_HK + GLA_HV:2 * GLA_HK + 2 * GLA_HV].astype(bf16)
    wg = jnp.pad(w_in[:, 2 * GLA_HK + 2 * GLA_HV:], ((0, 0), (0, LANES - GLA_GATE_RANK))).astype(bf16)
    wgu = jnp.pad(w_gate_up, ((0, LANES - GLA_GATE_RANK), (0, 0))).astype(bf16)
    n_s = seq // ts
    args = (x, norm_g.reshape(1, D_MODEL), wq, wk, wv, wr, wg, wgu, b_gate.reshape(1, GLA_HK),
            head_g.reshape(1, GLA_HV), w_out.astype(bf16), moe_norm_g.reshape(1, D_MODEL), wrt, br)
    in_specs = [pl.BlockSpec((1, ts, D_MODEL), lambda b, s: (b, s, 0))]
    in_specs += [_const_spec(a.shape) for a in args[1:]]
    return pl.pallas_call(
        _gla_kernel,
        grid=(bsz, n_s),
        in_specs=in_specs,
        out_specs=[pl.BlockSpec((ts * SLAB, LANES), lambda b, s: (b * n_s + s, 0)),
                   pl.BlockSpec((META_ROWS, ts), lambda b, s: (0, b * n_s + s))],
        out_shape=[jax.ShapeDtypeStruct((t * SLAB, LANES), f32),
                   jax.ShapeDtypeStruct((META_ROWS, t), f32)],
        scratch_shapes=[pltpu.VMEM((GLA_HEADS, GLA_DV, GLA_DK), f32),
                        pltpu.VMEM((ts, GLA_HV), f32)],
        compiler_params=pltpu.CompilerParams(
            dimension_semantics=("arbitrary", "arbitrary"), vmem_limit_bytes=VMEM_LIMIT),
        name="gla_mixer",
    )(*args)


ISSUE_UNROLL = 8


def _issue_rows(src_hbm, dst_buf, idx_ref, n_rows, sem):
    def body(g, carry):
        for u in range(ISSUE_UNROLL):
            r = g * ISSUE_UNROLL + u
            src = pl.multiple_of(idx_ref[0, 0, r], SLAB)
            dst = pl.multiple_of(r * SLAB, SLAB)
            pltpu.make_async_copy(src_hbm.at[pl.ds(src, SLAB), :],
                                  dst_buf.at[pl.ds(dst, SLAB), :], sem).start()
        return carry
    lax.fori_loop(0, n_rows // ISSUE_UNROLL, body, 0)


def _wait_rows(src_hbm, dst_buf, sem):
    pltpu.make_async_copy(src_hbm.at[pl.ds(0, dst_buf.shape[0]), :], dst_buf, sem).wait()


def _expert_kernel(blk_e_ref, nvalid_ref, idx_ref, idx_next_ref, gate_ref, ng_ref,
                   w1_ref, w3_ref, w2_ref, x_hbm, ys_ref, gbuf, sems, w1b, w3b, w2b):
    i = pl.program_id(0)
    nvalid = nvalid_ref[0]
    rows = gate_ref.shape[0]
    slot = i % 2

    @pl.when(i == 0)
    def _():
        _issue_rows(x_hbm, gbuf.at[0], idx_ref, rows, sems.at[0])

    @pl.when(i + 1 < nvalid)
    def _():
        _issue_rows(x_hbm, gbuf.at[1 - slot], idx_next_ref, rows, sems.at[1 - slot])

    @pl.when(jnp.logical_or(i == 0, blk_e_ref[i] != blk_e_ref[jnp.maximum(i - 1, 0)]))
    def _():
        w1b[...] = w1_ref[0].astype(bf16)
        w3b[...] = w3_ref[0].astype(bf16)
        w2b[...] = w2_ref[0].astype(bf16)

    @pl.when(i < nvalid)
    def _():
        _wait_rows(x_hbm, gbuf.at[slot], sems.at[slot])
        g = gbuf.at[slot]
        cols = [g[pl.ds(j, rows, stride=SLAB), :] for j in range(SLAB)]
        ssq = cols[0] * cols[0]
        for j in range(1, SLAB):
            ssq = ssq + cols[j] * cols[j]
        rs = lax.rsqrt(jnp.sum(ssq, axis=-1, keepdims=True) * (1.0 / D_MODEL) + EPS)
        hn = jnp.concatenate(
            [(cols[j] * rs * ng_ref[:, j * LANES:(j + 1) * LANES]).astype(bf16) for j in range(SLAB)],
            axis=-1)
        h1 = _dot(hn, w1b[...])
        h3 = _dot(hn, w3b[...])
        hid = ((h1 / (1.0 + jnp.exp(-h1))) * h3).astype(bf16)
        y = _dot(hid, w2b[...]) * gate_ref[...]
        _slab_store(ys_ref, y)

    @pl.when(i >= nvalid)
    def _():
        ys_ref[...] = jnp.zeros_like(ys_ref)


def _experts(x1s, slab_idx, gate_col, blk_e, nvalid, norm_g, w1, w3, w2, rows):
    nb = blk_e.shape[0]
    idx3 = slab_idx.reshape(nb, 1, rows)
    last = lambda i, be, nv: jnp.maximum(jnp.minimum(i, nv[0] - 1), 0)
    grid_spec = pltpu.PrefetchScalarGridSpec(
        num_scalar_prefetch=2,
        grid=(nb,),
        in_specs=[
            pl.BlockSpec((1, 1, rows), lambda i, be, nv: (i, 0, 0), memory_space=pltpu.SMEM),
            pl.BlockSpec((1, 1, rows), lambda i, be, nv: (jnp.minimum(i + 1, nb - 1), 0, 0),
                         memory_space=pltpu.SMEM),
            pl.BlockSpec((rows, 1), lambda i, be, nv: (last(i, be, nv), 0)),
            pl.BlockSpec((1, D_MODEL), lambda i, be, nv: (0, 0)),
            pl.BlockSpec((1, D_MODEL, EXPERT_FF), lambda i, be, nv: (be[last(i, be, nv)], 0, 0)),
            pl.BlockSpec((1, D_MODEL, EXPERT_FF), lambda i, be, nv: (be[last(i, be, nv)], 0, 0)),
            pl.BlockSpec((1, EXPERT_FF, D_MODEL), lambda i, be, nv: (be[last(i, be, nv)], 0, 0)),
            pl.BlockSpec(memory_space=pl.ANY),
        ],
        out_specs=pl.BlockSpec((rows * SLAB, LANES), lambda i, be, nv: (i, 0)),
        scratch_shapes=[pltpu.VMEM((2, rows * SLAB, LANES), f32),
                        pltpu.SemaphoreType.DMA((2,)),
                        pltpu.VMEM((D_MODEL, EXPERT_FF), bf16),
                        pltpu.VMEM((D_MODEL, EXPERT_FF), bf16),
                        pltpu.VMEM((EXPERT_FF, D_MODEL), bf16)],
    )
    return pl.pallas_call(
        _expert_kernel,
        grid_spec=grid_spec,
        out_shape=jax.ShapeDtypeStruct((nb * rows * SLAB, LANES), f32),
        compiler_params=pltpu.CompilerParams(
            dimension_semantics=("arbitrary",), vmem_limit_bytes=VMEM_LIMIT),
        name="moe_experts",
    )(blk_e, nvalid, idx3, idx3, gate_col, norm_g.reshape(1, D_MODEL), w1, w3, w2, x1s)


def _dispatch_plan(meta, rows):
    t = meta.shape[1]
    a = t * TOP_K
    nb = a // rows + N_EXPERTS
    e = meta[0:TOP_K].T.astype(jnp.int32).reshape(a)
    gate = meta[TOP_K:2 * TOP_K].T.reshape(a)
    onehot = (e[:, None] == jnp.arange(N_EXPERTS, dtype=jnp.int32)[None, :]).astype(jnp.int32)
    csum = jnp.cumsum(onehot, axis=0)
    counts = csum[-1]
    rank = jnp.sum(csum * onehot, axis=1) - 1
    padded = ((counts + rows - 1) // rows) * rows
    pad_end = jnp.cumsum(padded)
    pad_start = pad_end - padded
    dest = pad_start[e] + rank
    slot_a = jnp.full((nb * rows,), -1, jnp.int32).at[dest].set(
        jnp.arange(a, dtype=jnp.int32), unique_indices=True)
    valid = slot_a >= 0
    slab_idx = jnp.where(valid, slot_a // TOP_K, 0) * SLAB
    gate_col = jnp.where(valid, gate[jnp.maximum(slot_a, 0)], 0.0).reshape(nb * rows, 1)
    blk_e = jnp.minimum(
        jnp.searchsorted(pad_end, jnp.arange(nb, dtype=jnp.int32) * rows, side='right'),
        N_EXPERTS - 1).astype(jnp.int32)
    nvalid = (pad_end[-1:] // rows).astype(jnp.int32)
    return slab_idx, gate_col, blk_e, nvalid, (dest * SLAB).reshape(t, TOP_K)


def _combine(i, n_steps, xs_ref, idx_ref, idx_next_ref, ys_hbm, ybuf, sems):
    ts = xs_ref.shape[0] // SLAB
    slot = i % 2

    @pl.when(i == 0)
    def _():
        _issue_rows(ys_hbm, ybuf.at[0], idx_ref, TOP_K * ts, sems.at[0])

    @pl.when(i + 1 < n_steps)
    def _():
        _issue_rows(ys_hbm, ybuf.at[1 - slot], idx_next_ref, TOP_K * ts, sems.at[1 - slot])

    _wait_rows(ys_hbm, ybuf.at[slot], sems.at[slot])
    yb = ybuf.at[slot]
    x = _slab_load(xs_ref, ts)
    for kk in range(TOP_K):
        x = x + _slab_load(yb, ts, sub=kk * SLAB, stride=TOP_K * SLAB)
    return x


def _combine_specs(ts, n_steps):
    return [
        pl.BlockSpec((ts * SLAB, LANES), lambda i: (i, 0)),
        pl.BlockSpec((1, 1, TOP_K * ts), lambda i: (i, 0, 0), memory_space=pltpu.SMEM),
        pl.BlockSpec((1, 1, TOP_K * ts), lambda i: (jnp.minimum(i + 1, n_steps - 1), 0, 0),
                     memory_space=pltpu.SMEM),
        pl.BlockSpec(memory_space=pl.ANY),
    ]


def _combine_scratch(ts):
    return [pltpu.VMEM((2, TOP_K * ts * SLAB, LANES), f32), pltpu.SemaphoreType.DMA((2,))]


def _gelu(z):
    return 0.5 * z * (1.0 + lax.erf(z * (2.0 ** -0.5)))


def _sgu_kernel(xs_ref, idx_ref, idx_next_ref, ys_hbm, ng_ref, wu_ref, wv_ref, lng_ref, lnb_ref,
                ws_ref, bs_ref, wo_ref, mng_ref, wrt_ref, br_ref,
                x3_ref, meta_ref, ybuf, sems, vn_ref):
    ts = xs_ref.shape[0] // SLAB
    x2 = _combine(pl.program_id(0), pl.num_programs(0), xs_ref, idx_ref, idx_next_ref, ys_hbm, ybuf, sems)
    h = (x2 * _rms(x2) * ng_ref[...]).astype(bf16)
    v = _gelu(_dot(h, wv_ref[...]))
    mu = jnp.mean(v, axis=-1, keepdims=True)
    vc = v - mu
    rstd = lax.rsqrt(jnp.mean(vc * vc, axis=-1, keepdims=True) + EPS)
    vn_ref[...] = (vc * rstd * lng_ref[...] + lnb_ref[...]).astype(bf16)

    pos = lax.broadcasted_iota(jnp.int32, (SGU_BLOCK, SGU_BLOCK), 0) // CHUNK
    src = lax.broadcasted_iota(jnp.int32, (SGU_BLOCK, SGU_BLOCK), 1) // CHUNK
    acc = x2
    for g in range(SGU_GROUPS):
        cs = slice(g * SGU_GC, (g + 1) * SGU_GC)
        ws = jnp.where(pos >= src, ws_ref[g], jnp.zeros((), bf16))
        u = _gelu(_dot(h, wu_ref[:, cs]))
        mixed = [_dot(ws, vn_ref[nb * SGU_BLOCK:(nb + 1) * SGU_BLOCK, cs]) + bs_ref[:, g:g + 1]
                 for nb in range(ts // SGU_BLOCK)]
        out = (u * jnp.concatenate(mixed, axis=0)).astype(bf16)
        acc = acc + _dot(out, wo_ref[cs, :])
    _slab_store(x3_ref, acc)
    _route(acc, mng_ref[...], wrt_ref[...], br_ref[...], meta_ref)


def _sgu_layer(x1s, dest_idx, ys, norm_g, w_in, ln_g, ln_b, w_s, b_s, w_out, moe_norm_g, wrt, br, ts):
    t = x1s.shape[0] // SLAB
    n_steps = t // ts
    idx3 = dest_idx.reshape(n_steps, 1, TOP_K * ts)
    args = (norm_g.reshape(1, D_MODEL), w_in[:, :SGU_HALF].astype(bf16), w_in[:, SGU_HALF:].astype(bf16),
            ln_g.reshape(1, SGU_HALF), ln_b.reshape(1, SGU_HALF), w_s.astype(bf16), b_s.T,
            w_out.astype(bf16), moe_norm_g.reshape(1, D_MODEL), wrt, br)
    return pl.pallas_call(
        _sgu_kernel,
        grid=(n_steps,),
        in_specs=_combine_specs(ts, n_steps) + [_const_spec(a.shape) for a in args],
        out_specs=[pl.BlockSpec((ts * SLAB, LANES), lambda i: (i, 0)),
                   pl.BlockSpec((META_ROWS, ts), lambda i: (0, i))],
        out_shape=[jax.ShapeDtypeStruct((t * SLAB, LANES), f32),
                   jax.ShapeDtypeStruct((META_ROWS, t), f32)],
        scratch_shapes=_combine_scratch(ts) + [pltpu.VMEM((ts, SGU_HALF), bf16)],
        compiler_params=pltpu.CompilerParams(
            dimension_semantics=("arbitrary",), vmem_limit_bytes=VMEM_LIMIT),
        name="sgu_mixer",
    )(x1s, idx3, idx3, ys, *args)


def _final_kernel(xs_ref, idx_ref, idx_next_ref, ys_hbm, ng_ref, out_ref, ybuf, sems):
    x = _combine(pl.program_id(0), pl.num_programs(0), xs_ref, idx_ref, idx_next_ref, ys_hbm, ybuf, sems)
    out_ref[...] = x * _rms(x) * ng_ref[...]


def _final_layer(xs, dest_idx, ys, norm_g, ts):
    t = xs.shape[0] // SLAB
    n_steps = t // ts
    idx3 = dest_idx.reshape(n_steps, 1, TOP_K * ts)
    return pl.pallas_call(
        _final_kernel,
        grid=(n_steps,),
        in_specs=_combine_specs(ts, n_steps) + [_const_spec((1, D_MODEL))],
        out_specs=pl.BlockSpec((ts, D_MODEL), lambda i: (i, 0)),
        out_shape=jax.ShapeDtypeStruct((t, D_MODEL), f32),
        scratch_shapes=_combine_scratch(ts),
        compiler_params=pltpu.CompilerParams(
            dimension_semantics=("arbitrary",), vmem_limit_bytes=VMEM_LIMIT),
        name="final_norm",
    )(xs, idx3, idx3, ys, norm_g.reshape(1, D_MODEL))


def _forward(x, gla_norm, gla_w_in, gla_w_gate_up, gla_b_gate, gla_head_g, gla_w_out, sgu_norm, sgu_w_in,
             sgu_ln_g, sgu_ln_b, sgu_w_s, sgu_b_s, sgu_w_out, moe_norm, moe_w_group, moe_b_group,
             moe_w_sub, moe_b_sub, moe_w1, moe_w3, moe_w2, final_norm, *, ts_gla, ts_sgu, ts_fin, rows):
    wrt0, br0 = _router_params(moe_w_group[0], moe_b_group[0], moe_w_sub[0], moe_b_sub[0])
    wrt1, br1 = _router_params(moe_w_group[1], moe_b_group[1], moe_w_sub[1], moe_b_sub[1])
    x1s, meta0 = _gla_layer(x, gla_norm[0], gla_w_in[0], gla_w_gate_up[0], gla_b_gate[0], gla_head_g[0],
                            gla_w_out[0], moe_norm[0], wrt0, br0, ts_gla)
    slab_idx, gate_col, blk_e, nvalid, dest0 = _dispatch_plan(meta0, rows)
    ys0 = _experts(x1s, slab_idx, gate_col, blk_e, nvalid, moe_norm[0], moe_w1[0], moe_w3[0], moe_w2[0], rows)
    x3s, meta1 = _sgu_layer(x1s, dest0, ys0, sgu_norm[0], sgu_w_in[0], sgu_ln_g[0], sgu_ln_b[0],
                            sgu_w_s[0], sgu_b_s[0], sgu_w_out[0], moe_norm[1], wrt1, br1, ts_sgu)
    slab_idx, gate_col, blk_e, nvalid, dest1 = _dispatch_plan(meta1, rows)
    ys1 = _experts(x3s, slab_idx, gate_col, blk_e, nvalid, moe_norm[1], moe_w1[1], moe_w3[1], moe_w2[1], rows)
    out = _final_layer(x3s, dest1, ys1, final_norm, ts_fin)
    return out.reshape(x.shape)


def kernel(x, gla_norm, gla_w_in, gla_w_gate_up, gla_b_gate, gla_head_g, gla_w_out, sgu_norm, sgu_w_in, sgu_ln_g, sgu_ln_b, sgu_w_s, sgu_b_s, sgu_w_out, moe_norm, moe_w_group, moe_b_group, moe_w_sub, moe_b_sub, moe_w1, moe_w3, moe_w2, final_norm):
    return _forward(x, gla_norm, gla_w_in, gla_w_gate_up, gla_b_gate, gla_head_g, gla_w_out, sgu_norm,
                    sgu_w_in, sgu_ln_g, sgu_ln_b, sgu_w_s, sgu_b_s, sgu_w_out, moe_norm, moe_w_group,
                    moe_b_group, moe_w_sub, moe_b_sub, moe_w1, moe_w3, moe_w2, final_norm,
                    ts_gla=512, ts_sgu=512, ts_fin=512, rows=256)
```

```python
import jax
import jax.numpy as jnp
from jax import lax
from jax.experimental import pallas as pl
from jax.experimental.pallas import tpu as pltpu

D_MODEL = 1024
EPS = 1e-6
LANES = 128
SUBLANES = 8
SLAB = D_MODEL // LANES

CHUNK = 64
GLA_HEADS = 4
GLA_DK = 128
GLA_DV = 256
GLA_HK = GLA_HEADS * GLA_DK
GLA_HV = GLA_HEADS * GLA_DV
GLA_GATE_RANK = 16
GLA_TAU = 16.0

SGU_BLOCK = 128
SGU_GROUPS = 4
SGU_HALF = 2048
SGU_GC = SGU_HALF // SGU_GROUPS

N_GROUPS = 4
EXPERTS_PER_GROUP = 8
N_EXPERTS = N_GROUPS * EXPERTS_PER_GROUP
TOP_K = 2
EXPERT_FF = 512
ROUTE_ROWS = 64
META_ROWS = 8

SORT_TILE = 256
RUN_CHUNK = 8
LOCAL_CAP = 768
MAX_CHUNKS = LOCAL_CAP // RUN_CHUNK
EXPERT_ROWS = 256
ZERO_CHUNK = 64
TABLE_LEN = 256
assert LOCAL_CAP >= TOP_K * SORT_TILE + N_EXPERTS * (RUN_CHUNK - 1)
assert 2 * MAX_CHUNKS < TABLE_LEN and EXPERT_ROWS % ZERO_CHUNK == 0

VMEM_LIMIT = 56 * 1024 * 1024

f32 = jnp.float32
bf16 = jnp.bfloat16
i32 = jnp.int32


def _dot(a, b):
    return jnp.dot(a, b, preferred_element_type=f32)


def _dot_tn(a, b):
    return lax.dot_general(a, b, (((0,), (0,)), ((), ())), preferred_element_type=f32)


def _dot_nt(a, b):
    return lax.dot_general(a, b, (((1,), (1,)), ((), ())), preferred_element_type=f32)


def _rms(x):
    return lax.rsqrt(jnp.mean(x * x, axis=-1, keepdims=True) + EPS)


def _slab_load(ref, n_rows, first=0):
    cols = [ref[pl.ds(first * SLAB + j, n_rows, stride=SLAB), :] for j in range(SLAB)]
    return jnp.concatenate(cols, axis=-1)


def _slab_store(ref, val):
    for j in range(SLAB):
        ref[pl.ds(j, val.shape[0], stride=SLAB), :] = val[:, j * LANES:(j + 1) * LANES]


def _const_spec(shape):
    return pl.BlockSpec(shape, lambda *_: (0,) * len(shape))


def _route(x1, norm_g, wr_t, br, meta_ref, cnt_ref):
    n = x1.shape[0]
    hn = (x1 * _rms(x1) * norm_g).astype(bf16)
    lt = _dot_nt(wr_t, hn) + br
    rows = lax.broadcasted_iota(i32, (SUBLANES, n), 0)
    neg = jnp.float32(-jnp.inf)
    lg = jnp.where(rows < N_GROUPS, lt[0:SUBLANES], neg)
    gmax = jnp.max(lg, axis=0, keepdims=True)
    gidx = jnp.min(jnp.where(lg == gmax, rows, SUBLANES), axis=0, keepdims=True)
    g_w = 1.0 / jnp.sum(jnp.exp(lg - gmax), axis=0, keepdims=True)
    chosen = jnp.zeros((SUBLANES, n), f32)
    for g in range(N_GROUPS):
        chosen = jnp.where(gidx == g, lt[SUBLANES * (g + 1):SUBLANES * (g + 2)], chosen)
    m1 = jnp.max(chosen, axis=0, keepdims=True)
    i1 = jnp.min(jnp.where(chosen == m1, rows, SUBLANES), axis=0, keepdims=True)
    rest = jnp.where(rows == i1, neg, chosen)
    m2 = jnp.max(rest, axis=0, keepdims=True)
    i2 = jnp.min(jnp.where(rest == m2, rows, SUBLANES), axis=0, keepdims=True)
    t = jnp.exp(m2 - m1)
    s1 = 1.0 / (1.0 + t)
    s2 = t / (1.0 + t)
    e1 = gidx * EXPERTS_PER_GROUP + i1
    e2 = gidx * EXPERTS_PER_GROUP + i2
    zero = jnp.zeros((1, n), f32)
    meta_ref[...] = jnp.concatenate(
        [e1.astype(f32), e2.astype(f32), g_w * s1, g_w * s2, zero, zero, zero, zero], axis=0)
    ids = lax.broadcasted_iota(i32, (N_EXPERTS, n), 0)
    hits = (ids == e1).astype(f32) + (ids == e2).astype(f32)
    for u in range(n // SORT_TILE):
        cnt_ref[u * N_EXPERTS:(u + 1) * N_EXPERTS, :] = jnp.sum(
            hits[:, u * SORT_TILE:(u + 1) * SORT_TILE], axis=1, keepdims=True)


def _router_params(w_group, b_group, w_sub, b_sub):
    wrt = jnp.zeros((ROUTE_ROWS, D_MODEL), f32)
    wrt = wrt.at[0:N_GROUPS].set(w_group.T)
    wrt = wrt.at[SUBLANES:SUBLANES + N_EXPERTS].set(
        jnp.transpose(w_sub, (0, 2, 1)).reshape(N_EXPERTS, D_MODEL))
    br = jnp.zeros((ROUTE_ROWS, 1), f32)
    br = br.at[0:N_GROUPS, 0].set(b_group)
    br = br.at[SUBLANES:SUBLANES + N_EXPERTS, 0].set(b_sub.reshape(N_EXPERTS))
    return wrt.astype(bf16), br


def _route_out_specs(ts, index):
    tiles = ts // SORT_TILE
    return [pl.BlockSpec((ts * SLAB, LANES), lambda *g: (index(*g), 0)),
            pl.BlockSpec((META_ROWS, ts), lambda *g: (0, index(*g))),
            pl.BlockSpec((tiles * N_EXPERTS, 1), lambda *g: (index(*g), 0))]


def _route_out_shapes(t):
    return [jax.ShapeDtypeStruct((t * SLAB, LANES), f32),
            jax.ShapeDtypeStruct((META_ROWS, t), f32),
            jax.ShapeDtypeStruct((t // SORT_TILE * N_EXPERTS, 1), f32)]


def _gla_kernel(x_ref, ng_ref, wq_ref, wk_ref, wv_ref, wr_ref, wg_ref, wgu_ref, bg_ref,
                hg_ref, wo_ref, mng_ref, wrt_ref, br_ref,
                x1_ref, meta_ref, cnt_ref, st_ref, o_ref):
    ts = x_ref.shape[1]

    @pl.when(pl.program_id(1) == 0)
    def _():
        st_ref[...] = jnp.zeros_like(st_ref)

    x = x_ref[0]
    h = (x * _rms(x) * ng_ref[...]).astype(bf16)
    q = _dot(h, wq_ref[...]) * (GLA_DK ** -0.5)
    k = _dot(h, wk_ref[...])
    v = _dot(h, wv_ref[...]).astype(bf16)
    glr = _dot(h, wg_ref[...]).astype(bf16)
    gp = _dot(glr, wgu_ref[...]) + bg_ref[...]
    log_a = (jnp.minimum(gp, 0.0) - jnp.log(1.0 + jnp.exp(-jnp.abs(gp)))) * (1.0 / GLA_TAU)

    row = lax.broadcasted_iota(i32, (CHUNK, GLA_HK), 0)
    for c in range(ts // CHUNK):
        rs = slice(c * CHUNK, (c + 1) * CHUNK)
        b = log_a[rs]
        sh = 1
        while sh < CHUNK:
            b = b + jnp.where(row >= sh, pltpu.roll(b, sh, axis=0), 0.0)
            sh *= 2
        b_end = b[CHUNK - 1:CHUNK]
        kdec = (k[rs] * jnp.exp(b_end - b)).astype(bf16)
        decay = jnp.exp(b_end)
        qc = q[rs].astype(bf16)
        vc = v[rs]
        for hd in range(GLA_HEADS):
            ks = slice(hd * GLA_DK, (hd + 1) * GLA_DK)
            vs = slice(hd * GLA_DV, (hd + 1) * GLA_DV)
            st = st_ref[hd] * decay[:, ks] + _dot_tn(vc[:, vs], kdec[:, ks])
            st_ref[hd] = st
            o_ref[rs, vs] = _dot_nt(qc[:, ks], st.astype(bf16))

    r = _dot(h, wr_ref[...])
    gated = []
    for hd in range(GLA_HEADS):
        vs = slice(hd * GLA_DV, (hd + 1) * GLA_DV)
        oh = o_ref[:, vs]
        rh = r[:, vs]
        gated.append(oh * _rms(oh) * hg_ref[:, vs] * (rh / (1.0 + jnp.exp(-rh))))
    y = _dot(jnp.concatenate(gated, axis=-1).astype(bf16), wo_ref[...])
    x1 = x + y
    _slab_store(x1_ref, x1)
    _route(x1, mng_ref[...], wrt_ref[...], br_ref[...], meta_ref, cnt_ref)


def _gla_layer(x, norm_g, w_in, w_gate_up, b_gate, head_g, w_out, moe_norm_g, wrt, br, ts):
    bsz, seq, _ = x.shape
    t = bsz * seq
    wq = w_in[:, 0:GLA_HK].astype(bf16)
    wk = w_in[:, GLA_HK:2 * GLA_HK].astype(bf16)
    wv = w_in[:, 2 * GLA_HK:2 * GLA_HK + GLA_HV].astype(bf16)
    wr = w_in[:, 2 * GLA_HK + GLA_HV:2 * GLA_HK + 2 * GLA_HV].astype(bf16)
    wg = jnp.pad(w_in[:, 2 * GLA_HK + 2 * GLA_HV:], ((0, 0), (0, LANES - GLA_GATE_RANK))).astype(bf16)
    wgu = jnp.pad(w_gate_up, ((0, LANES - GLA_GATE_RANK), (0, 0))).astype(bf16)
    n_s = seq // ts
    args = (x, norm_g.reshape(1, D_MODEL), wq, wk, wv, wr, wg, wgu, b_gate.reshape(1, GLA_HK),
            head_g.reshape(1, GLA_HV), w_out.astype(bf16), moe_norm_g.reshape(1, D_MODEL), wrt, br)
    in_specs = [pl.BlockSpec((1, ts, D_MODEL), lambda b, s: (b, s, 0))]
    in_specs += [_const_spec(a.shape) for a in args[1:]]
    return pl.pallas_call(
        _gla_kernel,
        grid=(bsz, n_s),
        in_specs=in_specs,
        out_specs=_route_out_specs(ts, lambda b, s: b * n_s + s),
        out_shape=_route_out_shapes(t),
        scratch_shapes=[pltpu.VMEM((GLA_HEADS, GLA_DV, GLA_DK), f32),
                        pltpu.VMEM((ts, GLA_HV), f32)],
        compiler_params=pltpu.CompilerParams(
            dimension_semantics=("arbitrary", "arbitrary"), vmem_limit_bytes=VMEM_LIMIT),
        name="gla_mixer",
    )(*args)


def _n_expert_blocks(t):
    worst = t * TOP_K + N_EXPERTS * (RUN_CHUNK + EXPERT_ROWS - 2)
    return -(-worst // EXPERT_ROWS)


def _zero_table_len(t):
    tail = _n_expert_blocks(t) * EXPERT_ROWS - t * TOP_K
    per_expert = -(-(EXPERT_ROWS + RUN_CHUNK) // ZERO_CHUNK)
    n = (N_EXPERTS - 1) * per_expert + -(-tail // ZERO_CHUNK) + per_expert
    return -(-(n + 1) // LANES) * LANES


def _flat_chunks(n_per, max_n):
    cum = jnp.cumsum(n_per, axis=-1)
    c = jnp.arange(max_n, dtype=i32)
    seg = jnp.sum((c[:, None] >= cum[..., None, :]).astype(i32), axis=-1)
    seg = jnp.minimum(seg, n_per.shape[-1] - 1)
    start = jnp.take_along_axis(cum - n_per, seg, axis=-1)
    return seg, c - start, cum[..., -1]


def _plan(cnt, t):
    tiles = t // SORT_TILE
    cnt = cnt.reshape(tiles, N_EXPERTS).astype(i32)
    nch = (cnt + RUN_CHUNK - 1) // RUN_CHUNK
    counts = jnp.sum(cnt, axis=0)
    padded = ((counts + RUN_CHUNK - 1 + EXPERT_ROWS - 1) // EXPERT_ROWS) * EXPERT_ROWS
    pad_end = jnp.cumsum(padded)
    pad_start = pad_end - padded
    base = pad_start[None, :] + jnp.cumsum(cnt, axis=0) - cnt
    seg, j, ntot = _flat_chunks(nch, MAX_CHUNKS)
    off = (jnp.cumsum(nch, axis=1) - nch) * RUN_CHUNK
    loc = (jnp.take_along_axis(off, seg, axis=1) + j * RUN_CHUNK) * SLAB
    glb = (jnp.take_along_axis(base, seg, axis=1) + j * RUN_CHUNK) * SLAB
    table = jnp.zeros((tiles, TABLE_LEN), i32)
    table = table.at[:, 0:MAX_CHUNKS].set(loc).at[:, MAX_CHUNKS:2 * MAX_CHUNKS].set(glb)
    table = table.at[:, 2 * MAX_CHUNKS].set(ntot).reshape(tiles, 1, TABLE_LEN)
    off_col = off.astype(f32).reshape(tiles, N_EXPERTS, 1)

    nb = _n_expert_blocks(t)
    blk_e = jnp.minimum(
        jnp.searchsorted(pad_end, jnp.arange(nb, dtype=i32) * EXPERT_ROWS, side='right'),
        N_EXPERTS - 1).astype(i32)
    nvalid = (pad_end[-1:] // EXPERT_ROWS).astype(i32)

    region_end = jnp.concatenate([pad_start[1:], jnp.full((1,), nb * EXPERT_ROWS, i32)])
    zlen = region_end - (pad_start + counts)
    nz = (zlen + ZERO_CHUNK - 1) // ZERO_CHUNK
    zl = _zero_table_len(t)
    zseg, zj, nztot = _flat_chunks(nz, zl - 1)
    zstart = (region_end[zseg] - (zj + 1) * ZERO_CHUNK) * SLAB
    ztable = jnp.concatenate([zstart, nztot[None]]).astype(i32).reshape(1, 1, zl)
    return table, off_col, blk_e, nvalid, ztable


def _chunk_copy(stage, out_hbm, loc, glb, sem):
    return pltpu.make_async_copy(stage.at[pl.ds(pl.multiple_of(loc, SLAB), RUN_CHUNK * SLAB), :],
                                 out_hbm.at[pl.ds(pl.multiple_of(glb, SLAB), RUN_CHUNK * SLAB), :], sem)


def _dispatch_kernel(tab_ref, ztab_ref, xs_ref, meta_ref, offc_ref, ng_ref,
                     out_hbm, cmeta_ref, stage, zbuf, sems, zsem, prev_n):
    t = pl.program_id(0)
    slot = t % 2
    zl = ztab_ref.shape[2]

    def zero_copy(z):
        dst = pl.multiple_of(ztab_ref[0, 0, z], SLAB)
        return pltpu.make_async_copy(zbuf, out_hbm.at[pl.ds(dst, ZERO_CHUNK * SLAB), :], zsem)

    @pl.when(t == 0)
    def _():
        zbuf[...] = jnp.zeros_like(zbuf)
        nz = ztab_ref[0, 0, zl - 1]
        lax.fori_loop(0, nz, lambda z, c: (zero_copy(z).start(), c)[1], 0)
        lax.fori_loop(0, nz, lambda z, c: (zero_copy(z).wait(), c)[1], 0)

    x = _slab_load(xs_ref, SORT_TILE)
    hn = (x * _rms(x) * ng_ref[...]).astype(bf16)
    e1 = meta_ref[0:1, :].astype(i32)
    e2 = meta_ref[1:2, :].astype(i32)
    ids = lax.broadcasted_iota(i32, (N_EXPERTS, SORT_TILE), 0)
    oh0 = ids == e1
    oh1 = ids == e2
    before = (lax.broadcasted_iota(i32, (SORT_TILE, SORT_TILE), 0)
              < lax.broadcasted_iota(i32, (SORT_TILE, SORT_TILE), 1)).astype(bf16)
    c0 = _dot(oh0.astype(bf16), before)
    c1 = _dot(oh1.astype(bf16), before)
    n0 = jnp.sum(oh0.astype(f32), axis=1, keepdims=True)
    offc = offc_ref[0]
    pos0 = jnp.sum(jnp.where(oh0, offc + c0, 0.0), axis=0, keepdims=True)
    pos1 = jnp.sum(jnp.where(oh1, offc + n0 + c1, 0.0), axis=0, keepdims=True)
    zero = jnp.zeros((1, SORT_TILE), f32)
    cmeta_ref[...] = jnp.concatenate(
        [pos0, pos1, meta_ref[2:3, :], meta_ref[3:4, :], zero, zero, zero, zero], axis=0)
    p = lax.broadcasted_iota(i32, (LOCAL_CAP, SORT_TILE), 0).astype(f32)
    perm = jnp.logical_or(p == pos0, p == pos1).astype(bf16)
    _slab_store(stage.at[slot], _dot(perm, hn))

    def wait_chunks(n, s):
        def body(c, carry):
            _chunk_copy(stage.at[s], out_hbm, 0, 0, sems.at[s]).wait()
            return carry
        lax.fori_loop(0, n, body, 0)

    @pl.when(t > 0)
    def _():
        wait_chunks(prev_n[0], 1 - slot)

    ntot = tab_ref[0, 0, 2 * MAX_CHUNKS]

    def issue(c, carry):
        _chunk_copy(stage.at[slot], out_hbm, tab_ref[0, 0, c], tab_ref[0, 0, MAX_CHUNKS + c],
                    sems.at[slot]).start()
        return carry
    lax.fori_loop(0, ntot, issue, 0)
    prev_n[0] = ntot

    @pl.when(t == pl.num_programs(0) - 1)
    def _():
        wait_chunks(ntot, slot)


def _dispatch(xs, meta, table, off_col, ztable, norm_g):
    t = xs.shape[0] // SLAB
    tiles = t // SORT_TILE
    zl = ztable.shape[2]
    nb = _n_expert_blocks(t)
    return pl.pallas_call(
        _dispatch_kernel,
        grid=(tiles,),
        in_specs=[
            pl.BlockSpec((1, 1, TABLE_LEN), lambda i: (i, 0, 0), memory_space=pltpu.SMEM),
            pl.BlockSpec((1, 1, zl), lambda i: (0, 0, 0), memory_space=pltpu.SMEM),
            pl.BlockSpec((SORT_TILE * SLAB, LANES), lambda i: (i, 0)),
            pl.BlockSpec((META_ROWS, SORT_TILE), lambda i: (0, i)),
            pl.BlockSpec((1, N_EXPERTS, 1), lambda i: (i, 0, 0)),
            _const_spec((1, D_MODEL)),
        ],
        out_specs=[pl.BlockSpec(memory_space=pl.ANY),
                   pl.BlockSpec((META_ROWS, SORT_TILE), lambda i: (0, i))],
        out_shape=[jax.ShapeDtypeStruct((nb * EXPERT_ROWS * SLAB, LANES), f32),
                   jax.ShapeDtypeStruct((META_ROWS, t), f32)],
        scratch_shapes=[pltpu.VMEM((2, LOCAL_CAP * SLAB, LANES), f32),
                        pltpu.VMEM((ZERO_CHUNK * SLAB, LANES), f32),
                        pltpu.SemaphoreType.DMA((2,)),
                        pltpu.SemaphoreType.DMA(()),
                        pltpu.SMEM((1,), i32)],
        compiler_params=pltpu.CompilerParams(
            dimension_semantics=("arbitrary",), vmem_limit_bytes=VMEM_LIMIT),
        name="moe_dispatch",
    )(table, ztable, xs, meta, off_col, norm_g.reshape(1, D_MODEL))


def _expert_kernel(blk_e_ref, nvalid_ref, xs_ref, w1_ref, w3_ref, w2_ref, ys_ref, w1b, w3b, w2b):
    i = pl.program_id(0)
    nvalid = nvalid_ref[0]

    @pl.when(jnp.logical_or(i == 0, blk_e_ref[i] != blk_e_ref[jnp.maximum(i - 1, 0)]))
    def _():
        w1b[...] = w1_ref[0].astype(bf16)
        w3b[...] = w3_ref[0].astype(bf16)
        w2b[...] = w2_ref[0].astype(bf16)

    @pl.when(i < nvalid)
    def _():
        hn = _slab_load(xs_ref, EXPERT_ROWS).astype(bf16)
        h1 = _dot(hn, w1b[...])
        h3 = _dot(hn, w3b[...])
        hid = ((h1 / (1.0 + jnp.exp(-h1))) * h3).astype(bf16)
        _slab_store(ys_ref, _dot(hid, w2b[...]))

    @pl.when(i >= nvalid)
    def _():
        ys_ref[...] = jnp.zeros_like(ys_ref)


def _experts(xs, blk_e, nvalid, w1, w3, w2):
    nb = blk_e.shape[0]
    last = lambda i, be, nv: jnp.maximum(jnp.minimum(i, nv[0] - 1), 0)
    grid_spec = pltpu.PrefetchScalarGridSpec(
        num_scalar_prefetch=2,
        grid=(nb,),
        in_specs=[
            pl.BlockSpec((EXPERT_ROWS * SLAB, LANES), lambda i, be, nv: (last(i, be, nv), 0)),
            pl.BlockSpec((1, D_MODEL, EXPERT_FF), lambda i, be, nv: (be[last(i, be, nv)], 0, 0)),
            pl.BlockSpec((1, D_MODEL, EXPERT_FF), lambda i, be, nv: (be[last(i, be, nv)], 0, 0)),
            pl.BlockSpec((1, EXPERT_FF, D_MODEL), lambda i, be, nv: (be[last(i, be, nv)], 0, 0)),
        ],
        out_specs=pl.BlockSpec((EXPERT_ROWS * SLAB, LANES), lambda i, be, nv: (i, 0)),
        scratch_shapes=[pltpu.VMEM((D_MODEL, EXPERT_FF), bf16),
                        pltpu.VMEM((D_MODEL, EXPERT_FF), bf16),
                        pltpu.VMEM((EXPERT_FF, D_MODEL), bf16)],
    )
    return pl.pallas_call(
        _expert_kernel,
        grid_spec=grid_spec,
        out_shape=jax.ShapeDtypeStruct((nb * EXPERT_ROWS * SLAB, LANES), f32),
        compiler_params=pltpu.CompilerParams(
            dimension_semantics=("arbitrary",), vmem_limit_bytes=VMEM_LIMIT),
        name="moe_experts",
    )(blk_e, nvalid, xs, w1, w3, w2)


def _combine(xs_ref, tab_ref, tab_next_ref, cmeta_ref, ys_hbm, ybuf, sems):
    i = pl.program_id(0)
    n_steps = pl.num_programs(0)
    tiles = xs_ref.shape[0] // (SORT_TILE * SLAB)
    slot = i % 2

    def copy(s, u, loc, glb):
        dst = pl.multiple_of(u * LOCAL_CAP * SLAB + loc, SLAB)
        return pltpu.make_async_copy(ys_hbm.at[pl.ds(pl.multiple_of(glb, SLAB), RUN_CHUNK * SLAB), :],
                                     ybuf.at[s, pl.ds(dst, RUN_CHUNK * SLAB), :], sems.at[s])

    def fetch(tab, s):
        ybuf[s] = jnp.zeros(ybuf.shape[1:], f32)
        for u in range(tiles):
            def body(c, carry):
                copy(s, u, tab[u, 0, c], tab[u, 0, MAX_CHUNKS + c]).start()
                return carry
            lax.fori_loop(0, tab[u, 0, 2 * MAX_CHUNKS], body, 0)

    @pl.when(i == 0)
    def _():
        fetch(tab_ref, 0)

    @pl.when(i + 1 < n_steps)
    def _():
        fetch(tab_next_ref, 1 - slot)

    for u in range(tiles):
        def body(c, carry):
            copy(slot, u, 0, 0).wait()
            return carry
        lax.fori_loop(0, tab_ref[u, 0, 2 * MAX_CHUNKS], body, 0)

    yb = ybuf.at[slot]
    p = lax.broadcasted_iota(i32, (LOCAL_CAP, SORT_TILE), 0).astype(f32)
    parts = []
    for u in range(tiles):
        cs = slice(u * SORT_TILE, (u + 1) * SORT_TILE)
        pos0, pos1 = cmeta_ref[0:1, cs], cmeta_ref[1:2, cs]
        g = (jnp.where(p == pos0, cmeta_ref[2:3, cs], 0.0)
             + jnp.where(p == pos1, cmeta_ref[3:4, cs], 0.0))
        g_hi = g.astype(bf16)
        g_lo = (g - g_hi.astype(f32)).astype(bf16)
        y = _slab_load(yb, LOCAL_CAP, first=u * LOCAL_CAP).astype(bf16)
        parts.append(_slab_load(xs_ref, SORT_TILE, first=u * SORT_TILE)
                     + _dot_tn(g_hi, y) + _dot_tn(g_lo, y))
    return parts[0] if tiles == 1 else jnp.concatenate(parts, axis=0)


def _combine_specs(ts, n_steps):
    tiles = ts // SORT_TILE
    return [
        pl.BlockSpec((ts * SLAB, LANES), lambda i: (i, 0)),
        pl.BlockSpec((tiles, 1, TABLE_LEN), lambda i: (i, 0, 0), memory_space=pltpu.SMEM),
        pl.BlockSpec((tiles, 1, TABLE_LEN), lambda i: (jnp.minimum(i + 1, n_steps - 1), 0, 0),
                     memory_space=pltpu.SMEM),
        pl.BlockSpec((META_ROWS, ts), lambda i: (0, i)),
        pl.BlockSpec(memory_space=pl.ANY),
    ]


def _combine_scratch(ts):
    tiles = ts // SORT_TILE
    return [pltpu.VMEM((2, tiles * LOCAL_CAP * SLAB, LANES), f32), pltpu.SemaphoreType.DMA((2,))]


def _gelu(z):
    return 0.5 * z * (1.0 + lax.erf(z * (2.0 ** -0.5)))


def _sgu_kernel(xs_ref, ng_ref, wu_ref, wv_ref, lng_ref, lnb_ref,
                ws_ref, bs_ref, wo_ref, mng_ref, wrt_ref, br_ref,
                x3_ref, meta_ref, cnt_ref, vn_ref):
    ts = xs_ref.shape[0] // SLAB
    x2 = _slab_load(xs_ref, ts)
    h = (x2 * _rms(x2) * ng_ref[...]).astype(bf16)
    v = _gelu(_dot(h, wv_ref[...]))
    mu = jnp.mean(v, axis=-1, keepdims=True)
    vc = v - mu
    rstd = lax.rsqrt(jnp.mean(vc * vc, axis=-1, keepdims=True) + EPS)
    vn_ref[...] = (vc * rstd * lng_ref[...] + lnb_ref[...]).astype(bf16)

    pos = lax.broadcasted_iota(i32, (SGU_BLOCK, SGU_BLOCK), 0) // CHUNK
    src = lax.broadcasted_iota(i32, (SGU_BLOCK, SGU_BLOCK), 1) // CHUNK
    acc = x2
    for g in range(SGU_GROUPS):
        cs = slice(g * SGU_GC, (g + 1) * SGU_GC)
        ws = jnp.where(pos >= src, ws_ref[g], jnp.zeros((), bf16))
        u = _gelu(_dot(h, wu_ref[:, cs]))
        mixed = [_dot(ws, vn_ref[nb * SGU_BLOCK:(nb + 1) * SGU_BLOCK, cs]) + bs_ref[:, g:g + 1]
                 for nb in range(ts // SGU_BLOCK)]
        out = (u * jnp.concatenate(mixed, axis=0)).astype(bf16)
        acc = acc + _dot(out, wo_ref[cs, :])
    _slab_store(x3_ref, acc)
    _route(acc, mng_ref[...], wrt_ref[...], br_ref[...], meta_ref, cnt_ref)


def _sgu_layer(x2s, norm_g, w_in, ln_g, ln_b, w_s, b_s, w_out, moe_norm_g, wrt, br, ts):
    t = x2s.shape[0] // SLAB
    n_steps = t // ts
    args = (norm_g.reshape(1, D_MODEL), w_in[:, :SGU_HALF].astype(bf16), w_in[:, SGU_HALF:].astype(bf16),
            ln_g.reshape(1, SGU_HALF), ln_b.reshape(1, SGU_HALF), w_s.astype(bf16), b_s.T,
            w_out.astype(bf16), moe_norm_g.reshape(1, D_MODEL), wrt, br)
    return pl.pallas_call(
        _sgu_kernel,
        grid=(n_steps,),
        in_specs=[pl.BlockSpec((ts * SLAB, LANES), lambda i: (i, 0))] + [_const_spec(a.shape) for a in args],
        out_specs=_route_out_specs(ts, lambda i: i),
        out_shape=_route_out_shapes(t),
        scratch_shapes=[pltpu.VMEM((ts, SGU_HALF), bf16)],
        compiler_params=pltpu.CompilerParams(
            dimension_semantics=("arbitrary",), vmem_limit_bytes=VMEM_LIMIT),
        name="sgu_mixer",
    )(x2s, *args)


def _combine_kernel(xs_ref, tab_ref, tab_next_ref, cmeta_ref, ys_hbm, out_ref, ybuf, sems):
    _slab_store(out_ref, _combine(xs_ref, tab_ref, tab_next_ref, cmeta_ref, ys_hbm, ybuf, sems))


def _final_kernel(xs_ref, tab_ref, tab_next_ref, cmeta_ref, ys_hbm, ng_ref, out_ref, ybuf, sems):
    x = _combine(xs_ref, tab_ref, tab_next_ref, cmeta_ref, ys_hbm, ybuf, sems)
    out_ref[...] = x * _rms(x) * ng_ref[...]


def _combine_layer(xs, table, cmeta, ys, norm_g, ts):
    t = xs.shape[0] // SLAB
    n_steps = t // ts
    final = norm_g is not None
    extra = (norm_g.reshape(1, D_MODEL),) if final else ()
    return pl.pallas_call(
        _final_kernel if final else _combine_kernel,
        grid=(n_steps,),
        in_specs=_combine_specs(ts, n_steps) + [_const_spec(a.shape) for a in extra],
        out_specs=(pl.BlockSpec((ts, D_MODEL), lambda i: (i, 0)) if final
                   else pl.BlockSpec((ts * SLAB, LANES), lambda i: (i, 0))),
        out_shape=jax.ShapeDtypeStruct((t, D_MODEL) if final else (t * SLAB, LANES), f32),
        scratch_shapes=_combine_scratch(ts),
        compiler_params=pltpu.CompilerParams(
            dimension_semantics=("arbitrary",), vmem_limit_bytes=VMEM_LIMIT),
        name="final_norm" if final else "moe_combine",
    )(xs, table, table, cmeta, ys, *extra)


def _moe(xs, meta, cnt, norm_g, w1, w3, w2):
    t = xs.shape[0] // SLAB
    table, off_col, blk_e, nvalid, ztable = _plan(cnt, t)
    sorted_x, cmeta = _dispatch(xs, meta, table, off_col, ztable, norm_g)
    return _experts(sorted_x, blk_e, nvalid, w1, w3, w2), table, cmeta


def _forward(x, gla_norm, gla_w_in, gla_w_gate_up, gla_b_gate, gla_head_g, gla_w_out, sgu_norm, sgu_w_in,
             sgu_ln_g, sgu_ln_b, sgu_w_s, sgu_b_s, sgu_w_out, moe_norm, moe_w_group, moe_b_group,
             moe_w_sub, moe_b_sub, moe_w1, moe_w3, moe_w2, final_norm, *, ts_gla, ts_sgu, ts_fin):
    wrt0, br0 = _router_params(moe_w_group[0], moe_b_group[0], moe_w_sub[0], moe_b_sub[0])
    wrt1, br1 = _router_params(moe_w_group[1], moe_b_group[1], moe_w_sub[1], moe_b_sub[1])
    x1s, meta0, cnt0 = _gla_layer(x, gla_norm[0], gla_w_in[0], gla_w_gate_up[0], gla_b_gate[0], gla_head_g[0],
                                  gla_w_out[0], moe_norm[0], wrt0, br0, ts_gla)
    ys0, table0, cmeta0 = _moe(x1s, meta0, cnt0, moe_norm[0], moe_w1[0], moe_w3[0], moe_w2[0])
    x2s = _combine_layer(x1s, table0, cmeta0, ys0, None, ts_fin)
    x3s, meta1, cnt1 = _sgu_layer(x2s, sgu_norm[0], sgu_w_in[0], sgu_ln_g[0], sgu_ln_b[0],
                                  sgu_w_s[0], sgu_b_s[0], sgu_w_out[0], moe_norm[1], wrt1, br1, ts_sgu)
    ys1, table1, cmeta1 = _moe(x3s, meta1, cnt1, moe_norm[1], moe_w1[1], moe_w3[1], moe_w2[1])
    out = _combine_layer(x3s, table1, cmeta1, ys1, final_norm, ts_fin)
    return out.reshape(x.shape)


def kernel(x, gla_norm, gla_w_in, gla_w_gate_up, gla_b_gate, gla_head_g, gla_w_out, sgu_norm, sgu_w_in, sgu_ln_g, sgu_ln_b, sgu_w_s, sgu_b_s, sgu_w_out, moe_norm, moe_w_group, moe_b_group, moe_w_sub, moe_b_sub, moe_w1, moe_w3, moe_w2, final_norm):
    return _forward(x, gla_norm, gla_w_in, gla_w_gate_up, gla_b_gate, gla_head_g, gla_w_out, sgu_norm,
                    sgu_w_in, sgu_ln_g, sgu_ln_b, sgu_w_s, sgu_b_s, sgu_w_out, moe_norm, moe_w_group,
                    moe_b_group, moe_w_sub, moe_b_sub, moe_w1, moe_w3, moe_w2, final_norm,
                    ts_gla=512, ts_sgu=512, ts_fin=512)
```

```python
import jax
import jax.numpy as jnp
from jax import lax
from jax.experimental import pallas as pl
from jax.experimental.pallas import tpu as pltpu

D_MODEL = 1024
EPS = 1e-6
LANES = 128
SUBLANES = 8
SLAB = D_MODEL // LANES

CHUNK = 64
GLA_HEADS = 4
GLA_DK = 128
GLA_DV = 256
GLA_HK = GLA_HEADS * GLA_DK
GLA_HV = GLA_HEADS * GLA_DV
GLA_GATE_RANK = 16
GLA_TAU = 16.0

SGU_BLOCK = 128
SGU_GROUPS = 4
SGU_HALF = 2048
SGU_GC = SGU_HALF // SGU_GROUPS

N_GROUPS = 4
EXPERTS_PER_GROUP = 8
N_EXPERTS = N_GROUPS * EXPERTS_PER_GROUP
TOP_K = 2
EXPERT_FF = 512
ROUTE_ROWS = 64
META_ROWS = 8

SORT_TILE = 256
RUN_CHUNK = 8
LOCAL_CAP = 768
MAX_CHUNKS = LOCAL_CAP // RUN_CHUNK
EXPERT_ROWS = 256
ZERO_CHUNK = 64
TABLE_LEN = 256
assert LOCAL_CAP >= TOP_K * SORT_TILE + N_EXPERTS * (RUN_CHUNK - 1)
assert 2 * MAX_CHUNKS < TABLE_LEN and EXPERT_ROWS % ZERO_CHUNK == 0

VMEM_LIMIT = 56 * 1024 * 1024

f32 = jnp.float32
bf16 = jnp.bfloat16
i32 = jnp.int32


def _dot(a, b):
    return jnp.dot(a, b, preferred_element_type=f32)


def _dot_tn(a, b):
    return lax.dot_general(a, b, (((0,), (0,)), ((), ())), preferred_element_type=f32)


def _dot_nt(a, b):
    return lax.dot_general(a, b, (((1,), (1,)), ((), ())), preferred_element_type=f32)


def _rms(x):
    return lax.rsqrt(jnp.mean(x * x, axis=-1, keepdims=True) + EPS)


def _slab_load(ref, n_rows, first=0):
    cols = [ref[pl.ds(first * SLAB + j, n_rows, stride=SLAB), :] for j in range(SLAB)]
    return jnp.concatenate(cols, axis=-1)


def _slab_store(ref, val):
    for j in range(SLAB):
        ref[pl.ds(j, val.shape[0], stride=SLAB), :] = val[:, j * LANES:(j + 1) * LANES]


def _const_spec(shape):
    return pl.BlockSpec(shape, lambda *_: (0,) * len(shape))


def _route(x1, norm_g, wr_t, br, meta_ref, cnt_ref):
    n = x1.shape[0]
    hn = (x1 * _rms(x1) * norm_g).astype(bf16)
    lt = _dot_nt(wr_t, hn) + br
    rows = lax.broadcasted_iota(i32, (SUBLANES, n), 0)
    neg = jnp.float32(-jnp.inf)
    lg = jnp.where(rows < N_GROUPS, lt[0:SUBLANES], neg)
    gmax = jnp.max(lg, axis=0, keepdims=True)
    gidx = jnp.min(jnp.where(lg == gmax, rows, SUBLANES), axis=0, keepdims=True)
    g_w = 1.0 / jnp.sum(jnp.exp(lg - gmax), axis=0, keepdims=True)
    chosen = jnp.zeros((SUBLANES, n), f32)
    for g in range(N_GROUPS):
        chosen = jnp.where(gidx == g, lt[SUBLANES * (g + 1):SUBLANES * (g + 2)], chosen)
    m1 = jnp.max(chosen, axis=0, keepdims=True)
    i1 = jnp.min(jnp.where(chosen == m1, rows, SUBLANES), axis=0, keepdims=True)
    rest = jnp.where(rows == i1, neg, chosen)
    m2 = jnp.max(rest, axis=0, keepdims=True)
    i2 = jnp.min(jnp.where(rest == m2, rows, SUBLANES), axis=0, keepdims=True)
    t = jnp.exp(m2 - m1)
    s1 = 1.0 / (1.0 + t)
    s2 = t / (1.0 + t)
    e1 = gidx * EXPERTS_PER_GROUP + i1
    e2 = gidx * EXPERTS_PER_GROUP + i2
    zero = jnp.zeros((1, n), f32)
    meta_ref[...] = jnp.concatenate(
        [e1.astype(f32), e2.astype(f32), g_w * s1, g_w * s2, zero, zero, zero, zero], axis=0)
    ids = lax.broadcasted_iota(i32, (N_EXPERTS, n), 0)
    hits = (ids == e1).astype(f32) + (ids == e2).astype(f32)
    for u in range(n // SORT_TILE):
        cnt_ref[u * N_EXPERTS:(u + 1) * N_EXPERTS, :] = jnp.sum(
            hits[:, u * SORT_TILE:(u + 1) * SORT_TILE], axis=1, keepdims=True)


def _router_params(w_group, b_group, w_sub, b_sub):
    tail = ROUTE_ROWS - SUBLANES - N_EXPERTS
    wrt = jnp.concatenate([
        w_group.T, jnp.zeros((SUBLANES - N_GROUPS, D_MODEL), f32),
        jnp.transpose(w_sub, (0, 2, 1)).reshape(N_EXPERTS, D_MODEL),
        jnp.zeros((tail, D_MODEL), f32)], axis=0)
    br = jnp.concatenate([b_group, jnp.zeros((SUBLANES - N_GROUPS,), f32), b_sub.reshape(N_EXPERTS),
                          jnp.zeros((tail,), f32)]).reshape(ROUTE_ROWS, 1)
    return wrt.astype(bf16), br


def _route_out_specs(ts, index):
    tiles = ts // SORT_TILE
    return [pl.BlockSpec((ts * SLAB, LANES), lambda *g: (index(*g), 0)),
            pl.BlockSpec((META_ROWS, ts), lambda *g: (0, index(*g))),
            pl.BlockSpec((tiles * N_EXPERTS, 1), lambda *g: (index(*g), 0))]


def _route_out_shapes(t):
    return [jax.ShapeDtypeStruct((t * SLAB, LANES), f32),
            jax.ShapeDtypeStruct((META_ROWS, t), f32),
            jax.ShapeDtypeStruct((t // SORT_TILE * N_EXPERTS, 1), f32)]


def _gla_kernel(x_ref, ng_ref, wq_ref, wk_ref, wv_ref, wr_ref, wg_ref, wgu_ref, bg_ref,
                hg_ref, wo_ref, mng_ref, wrt_ref, br_ref,
                x1_ref, meta_ref, cnt_ref, st_ref, o_ref):
    ts = x_ref.shape[1]

    @pl.when(pl.program_id(1) == 0)
    def _():
        st_ref[...] = jnp.zeros_like(st_ref)

    x = x_ref[0]
    h = (x * _rms(x) * ng_ref[...]).astype(bf16)
    q = _dot(h, wq_ref[...]) * (GLA_DK ** -0.5)
    k = _dot(h, wk_ref[...])
    v = _dot(h, wv_ref[...]).astype(bf16)
    glr = _dot(h, wg_ref[...]).astype(bf16)
    gp = _dot(glr, wgu_ref[...]) + bg_ref[...]
    log_a = (jnp.minimum(gp, 0.0) - jnp.log(1.0 + jnp.exp(-jnp.abs(gp)))) * (1.0 / GLA_TAU)

    row = lax.broadcasted_iota(i32, (CHUNK, GLA_HK), 0)
    for c in range(ts // CHUNK):
        rs = slice(c * CHUNK, (c + 1) * CHUNK)
        b = log_a[rs]
        sh = 1
        while sh < CHUNK:
            b = b + jnp.where(row >= sh, pltpu.roll(b, sh, axis=0), 0.0)
            sh *= 2
        b_end = b[CHUNK - 1:CHUNK]
        kdec = (k[rs] * jnp.exp(b_end - b)).astype(bf16)
        decay = jnp.exp(b_end)
        qc = q[rs].astype(bf16)
        vc = v[rs]
        for hd in range(GLA_HEADS):
            ks = slice(hd * GLA_DK, (hd + 1) * GLA_DK)
            vs = slice(hd * GLA_DV, (hd + 1) * GLA_DV)
            st = st_ref[hd] * decay[:, ks] + _dot_tn(vc[:, vs], kdec[:, ks])
            st_ref[hd] = st
            o_ref[rs, vs] = _dot_nt(qc[:, ks], st.astype(bf16))

    r = _dot(h, wr_ref[...])
    gated = []
    for hd in range(GLA_HEADS):
        vs = slice(hd * GLA_DV, (hd + 1) * GLA_DV)
        oh = o_ref[:, vs]
        rh = r[:, vs]
        gated.append(oh * _rms(oh) * hg_ref[:, vs] * (rh / (1.0 + jnp.exp(-rh))))
    y = _dot(jnp.concatenate(gated, axis=-1).astype(bf16), wo_ref[...])
    x1 = x + y
    _slab_store(x1_ref, x1)
    _route(x1, mng_ref[...], wrt_ref[...], br_ref[...], meta_ref, cnt_ref)


def _gla_layer(x, norm_g, w_in, w_gate_up, b_gate, head_g, w_out, moe_norm_g, wrt, br, ts):
    bsz, seq, _ = x.shape
    t = bsz * seq
    wq = w_in[:, 0:GLA_HK].astype(bf16)
    wk = w_in[:, GLA_HK:2 * GLA_HK].astype(bf16)
    wv = w_in[:, 2 * GLA_HK:2 * GLA_HK + GLA_HV].astype(bf16)
    wr = w_in[:, 2 * GLA_HK + GLA_HV:2 * GLA_HK + 2 * GLA_HV].astype(bf16)
    wg = jnp.pad(w_in[:, 2 * GLA_HK + 2 * GLA_HV:], ((0, 0), (0, LANES - GLA_GATE_RANK))).astype(bf16)
    wgu = jnp.pad(w_gate_up, ((0, LANES - GLA_GATE_RANK), (0, 0))).astype(bf16)
    n_s = seq // ts
    args = (x, norm_g.reshape(1, D_MODEL), wq, wk, wv, wr, wg, wgu, b_gate.reshape(1, GLA_HK),
            head_g.reshape(1, GLA_HV), w_out.astype(bf16), moe_norm_g.reshape(1, D_MODEL), wrt, br)
    in_specs = [pl.BlockSpec((1, ts, D_MODEL), lambda b, s: (b, s, 0))]
    in_specs += [_const_spec(a.shape) for a in args[1:]]
    return pl.pallas_call(
        _gla_kernel,
        grid=(bsz, n_s),
        in_specs=in_specs,
        out_specs=_route_out_specs(ts, lambda b, s: b * n_s + s),
        out_shape=_route_out_shapes(t),
        scratch_shapes=[pltpu.VMEM((GLA_HEADS, GLA_DV, GLA_DK), f32),
                        pltpu.VMEM((ts, GLA_HV), f32)],
        compiler_params=pltpu.CompilerParams(
            dimension_semantics=("arbitrary", "arbitrary"), vmem_limit_bytes=VMEM_LIMIT),
        name="gla_mixer",
    )(*args)


def _n_expert_blocks(t):
    worst = t * TOP_K + N_EXPERTS * (RUN_CHUNK + EXPERT_ROWS - 2)
    return -(-worst // EXPERT_ROWS)


def _zero_table_len(t):
    tail = _n_expert_blocks(t) * EXPERT_ROWS - t * TOP_K
    per_expert = -(-(EXPERT_ROWS + RUN_CHUNK) // ZERO_CHUNK)
    n = (N_EXPERTS - 1) * per_expert + -(-tail // ZERO_CHUNK) + per_expert
    return -(-(n + 1) // LANES) * LANES


def _cumsum(x, axis):
    x = jnp.moveaxis(x, axis, -1)
    n = x.shape[-1]
    upto = jnp.arange(n, dtype=i32)[:, None] <= jnp.arange(n, dtype=i32)[None, :]
    return jnp.moveaxis(jnp.sum(x[..., :, None] * upto.astype(i32), axis=-2), -1, axis)


def _flat_chunks(n_per, max_n):
    cum = _cumsum(n_per, -1)
    c = jnp.arange(max_n, dtype=i32)
    seg = jnp.minimum(jnp.sum((c[:, None] >= cum[..., None, :]).astype(i32), axis=-1), n_per.shape[-1] - 1)
    onehot = (seg[..., None] == jnp.arange(n_per.shape[-1], dtype=i32)).astype(i32)
    return onehot, c - _pick(onehot, cum - n_per), cum[..., -1]


def _pick(onehot, per_segment):
    return jnp.sum(onehot * per_segment[..., None, :], axis=-1)


def _plan(cnt, t):
    tiles = t // SORT_TILE
    cnt = cnt.reshape(tiles, N_EXPERTS).astype(i32)
    nch = (cnt + RUN_CHUNK - 1) // RUN_CHUNK
    counts = jnp.sum(cnt, axis=0)
    padded = ((counts + RUN_CHUNK - 1 + EXPERT_ROWS - 1) // EXPERT_ROWS) * EXPERT_ROWS
    pad_end = _cumsum(padded, 0)
    pad_start = pad_end - padded
    base = pad_start[None, :] + _cumsum(cnt, 0) - cnt
    onehot, j, ntot = _flat_chunks(nch, MAX_CHUNKS)
    off = (_cumsum(nch, 1) - nch) * RUN_CHUNK
    loc = (_pick(onehot, off) + j * RUN_CHUNK) * SLAB
    glb = (_pick(onehot, base) + j * RUN_CHUNK) * SLAB
    fill = jnp.zeros((tiles, TABLE_LEN - 2 * MAX_CHUNKS - 1), i32)
    table = jnp.concatenate([loc, glb, ntot[:, None], fill], axis=1).reshape(tiles, 1, TABLE_LEN)
    off_col = off.astype(f32).reshape(tiles, N_EXPERTS, 1)

    nb = _n_expert_blocks(t)
    blk_start = jnp.arange(nb, dtype=i32) * EXPERT_ROWS
    blk_e = jnp.minimum(jnp.sum((pad_end[None, :] <= blk_start[:, None]).astype(i32), axis=1), N_EXPERTS - 1)
    nvalid = pad_end[-1:] // EXPERT_ROWS

    region_end = jnp.concatenate([pad_start[1:], jnp.full((1,), nb * EXPERT_ROWS, i32)])
    zlen = region_end - (pad_start + counts)
    nz = (zlen + ZERO_CHUNK - 1) // ZERO_CHUNK
    zl = _zero_table_len(t)
    zonehot, zj, nztot = _flat_chunks(nz, zl - 1)
    zstart = (_pick(zonehot, region_end) - (zj + 1) * ZERO_CHUNK) * SLAB
    ztable = jnp.concatenate([zstart, nztot[None]]).reshape(1, 1, zl)
    return table, off_col, blk_e, nvalid, ztable


def _chunk_copy(stage, out_hbm, loc, glb, sem):
    return pltpu.make_async_copy(stage.at[pl.ds(pl.multiple_of(loc, SLAB), RUN_CHUNK * SLAB), :],
                                 out_hbm.at[pl.ds(pl.multiple_of(glb, SLAB), RUN_CHUNK * SLAB), :], sem)


def _dispatch_kernel(tab_ref, ztab_ref, xs_ref, meta_ref, offc_ref, ng_ref,
                     out_hbm, cmeta_ref, stage, zbuf, sems, zsem, prev_n):
    t = pl.program_id(0)
    slot = t % 2
    zl = ztab_ref.shape[2]

    def zero_copy(z):
        dst = pl.multiple_of(ztab_ref[0, 0, z], SLAB)
        return pltpu.make_async_copy(zbuf, out_hbm.at[pl.ds(dst, ZERO_CHUNK * SLAB), :], zsem)

    @pl.when(t == 0)
    def _():
        zbuf[...] = jnp.zeros_like(zbuf)
        nz = ztab_ref[0, 0, zl - 1]
        lax.fori_loop(0, nz, lambda z, c: (zero_copy(z).start(), c)[1], 0)
        lax.fori_loop(0, nz, lambda z, c: (zero_copy(z).wait(), c)[1], 0)

    x = _slab_load(xs_ref, SORT_TILE)
    hn = (x * _rms(x) * ng_ref[...]).astype(bf16)
    e1 = meta_ref[0:1, :].astype(i32)
    e2 = meta_ref[1:2, :].astype(i32)
    ids = lax.broadcasted_iota(i32, (N_EXPERTS, SORT_TILE), 0)
    oh0 = ids == e1
    oh1 = ids == e2
    before = (lax.broadcasted_iota(i32, (SORT_TILE, SORT_TILE), 0)
              < lax.broadcasted_iota(i32, (SORT_TILE, SORT_TILE), 1)).astype(bf16)
    c0 = _dot(oh0.astype(bf16), before)
    c1 = _dot(oh1.astype(bf16), before)
    n0 = jnp.sum(oh0.astype(f32), axis=1, keepdims=True)
    offc = offc_ref[0]
    pos0 = jnp.sum(jnp.where(oh0, offc + c0, 0.0), axis=0, keepdims=True)
    pos1 = jnp.sum(jnp.where(oh1, offc + n0 + c1, 0.0), axis=0, keepdims=True)
    zero = jnp.zeros((1, SORT_TILE), f32)
    cmeta_ref[...] = jnp.concatenate(
        [pos0, pos1, meta_ref[2:3, :], meta_ref[3:4, :], zero, zero, zero, zero], axis=0)
    p = lax.broadcasted_iota(i32, (LOCAL_CAP, SORT_TILE), 0).astype(f32)
    perm = jnp.logical_or(p == pos0, p == pos1).astype(bf16)
    _slab_store(stage.at[slot], _dot(perm, hn))

    def wait_chunks(n, s):
        def body(c, carry):
            _chunk_copy(stage.at[s], out_hbm, 0, 0, sems.at[s]).wait()
            return carry
        lax.fori_loop(0, n, body, 0)

    @pl.when(t > 0)
    def _():
        wait_chunks(prev_n[0], 1 - slot)

    ntot = tab_ref[0, 0, 2 * MAX_CHUNKS]

    def issue(c, carry):
        _chunk_copy(stage.at[slot], out_hbm, tab_ref[0, 0, c], tab_ref[0, 0, MAX_CHUNKS + c],
                    sems.at[slot]).start()
        return carry
    lax.fori_loop(0, ntot, issue, 0)
    prev_n[0] = ntot

    @pl.when(t == pl.num_programs(0) - 1)
    def _():
        wait_chunks(ntot, slot)


def _dispatch(xs, meta, table, off_col, ztable, norm_g):
    t = xs.shape[0] // SLAB
    tiles = t // SORT_TILE
    zl = ztable.shape[2]
    nb = _n_expert_blocks(t)
    return pl.pallas_call(
        _dispatch_kernel,
        grid=(tiles,),
        in_specs=[
            pl.BlockSpec((1, 1, TABLE_LEN), lambda i: (i, 0, 0), memory_space=pltpu.SMEM),
            pl.BlockSpec((1, 1, zl), lambda i: (0, 0, 0), memory_space=pltpu.SMEM),
            pl.BlockSpec((SORT_TILE * SLAB, LANES), lambda i: (i, 0)),
            pl.BlockSpec((META_ROWS, SORT_TILE), lambda i: (0, i)),
            pl.BlockSpec((1, N_EXPERTS, 1), lambda i: (i, 0, 0)),
            _const_spec((1, D_MODEL)),
        ],
        out_specs=[pl.BlockSpec(memory_space=pl.ANY),
                   pl.BlockSpec((META_ROWS, SORT_TILE), lambda i: (0, i))],
        out_shape=[jax.ShapeDtypeStruct((nb * EXPERT_ROWS * SLAB, LANES), f32),
                   jax.ShapeDtypeStruct((META_ROWS, t), f32)],
        scratch_shapes=[pltpu.VMEM((2, LOCAL_CAP * SLAB, LANES), f32),
                        pltpu.VMEM((ZERO_CHUNK * SLAB, LANES), f32),
                        pltpu.SemaphoreType.DMA((2,)),
                        pltpu.SemaphoreType.DMA(()),
                        pltpu.SMEM((1,), i32)],
        compiler_params=pltpu.CompilerParams(
            dimension_semantics=("arbitrary",), vmem_limit_bytes=VMEM_LIMIT),
        name="moe_dispatch",
    )(table, ztable, xs, meta, off_col, norm_g.reshape(1, D_MODEL))


def _expert_kernel(blk_e_ref, nvalid_ref, xs_ref, w1_ref, w3_ref, w2_ref, ys_ref, w1b, w3b, w2b):
    i = pl.program_id(0)
    nvalid = nvalid_ref[0]

    @pl.when(jnp.logical_or(i == 0, blk_e_ref[i] != blk_e_ref[jnp.maximum(i - 1, 0)]))
    def _():
        w1b[...] = w1_ref[0, 0].astype(bf16)
        w3b[...] = w3_ref[0, 0].astype(bf16)
        w2b[...] = w2_ref[0, 0].astype(bf16)

    @pl.when(i < nvalid)
    def _():
        hn = _slab_load(xs_ref, EXPERT_ROWS).astype(bf16)
        h1 = _dot(hn, w1b[...])
        h3 = _dot(hn, w3b[...])
        hid = ((h1 / (1.0 + jnp.exp(-h1))) * h3).astype(bf16)
        _slab_store(ys_ref, _dot(hid, w2b[...]))

    @pl.when(i >= nvalid)
    def _():
        ys_ref[...] = jnp.zeros_like(ys_ref)


def _experts(xs, blk_e, nvalid, w1, w3, w2, layer):
    nb = blk_e.shape[0]
    last = lambda i, be, nv: jnp.maximum(jnp.minimum(i, nv[0] - 1), 0)
    w_index = lambda i, be, nv: (layer, be[last(i, be, nv)], 0, 0)
    grid_spec = pltpu.PrefetchScalarGridSpec(
        num_scalar_prefetch=2,
        grid=(nb,),
        in_specs=[
            pl.BlockSpec((EXPERT_ROWS * SLAB, LANES), lambda i, be, nv: (last(i, be, nv), 0)),
            pl.BlockSpec((1, 1, D_MODEL, EXPERT_FF), w_index),
            pl.BlockSpec((1, 1, D_MODEL, EXPERT_FF), w_index),
            pl.BlockSpec((1, 1, EXPERT_FF, D_MODEL), w_index),
        ],
        out_specs=pl.BlockSpec((EXPERT_ROWS * SLAB, LANES), lambda i, be, nv: (i, 0)),
        scratch_shapes=[pltpu.VMEM((D_MODEL, EXPERT_FF), bf16),
                        pltpu.VMEM((D_MODEL, EXPERT_FF), bf16),
                        pltpu.VMEM((EXPERT_FF, D_MODEL), bf16)],
    )
    return pl.pallas_call(
        _expert_kernel,
        grid_spec=grid_spec,
        out_shape=jax.ShapeDtypeStruct((nb * EXPERT_ROWS * SLAB, LANES), f32),
        compiler_params=pltpu.CompilerParams(
            dimension_semantics=("arbitrary",), vmem_limit_bytes=VMEM_LIMIT),
        name="moe_experts",
    )(blk_e, nvalid, xs, w1, w3, w2)


def _combine(xs_ref, tab_ref, tab_next_ref, cmeta_ref, ys_hbm, ybuf, sems):
    i = pl.program_id(0)
    n_steps = pl.num_programs(0)
    tiles = xs_ref.shape[0] // (SORT_TILE * SLAB)
    slot = i % 2

    def copy(s, u, loc, glb):
        dst = pl.multiple_of(u * LOCAL_CAP * SLAB + loc, SLAB)
        return pltpu.make_async_copy(ys_hbm.at[pl.ds(pl.multiple_of(glb, SLAB), RUN_CHUNK * SLAB), :],
                                     ybuf.at[s, pl.ds(dst, RUN_CHUNK * SLAB), :], sems.at[s])

    def fetch(tab, s):
        ybuf[s] = jnp.zeros(ybuf.shape[1:], f32)
        for u in range(tiles):
            def body(c, carry):
                copy(s, u, tab[u, 0, c], tab[u, 0, MAX_CHUNKS + c]).start()
                return carry
            lax.fori_loop(0, tab[u, 0, 2 * MAX_CHUNKS], body, 0)

    @pl.when(i == 0)
    def _():
        fetch(tab_ref, 0)

    @pl.when(i + 1 < n_steps)
    def _():
        fetch(tab_next_ref, 1 - slot)

    for u in range(tiles):
        def body(c, carry):
            copy(slot, u, 0, 0).wait()
            return carry
        lax.fori_loop(0, tab_ref[u, 0, 2 * MAX_CHUNKS], body, 0)

    yb = ybuf.at[slot]
    p = lax.broadcasted_iota(i32, (LOCAL_CAP, SORT_TILE), 0).astype(f32)
    parts = []
    for u in range(tiles):
        cs = slice(u * SORT_TILE, (u + 1) * SORT_TILE)
        pos0, pos1 = cmeta_ref[0:1, cs], cmeta_ref[1:2, cs]
        g = (jnp.where(p == pos0, cmeta_ref[2:3, cs], 0.0)
             + jnp.where(p == pos1, cmeta_ref[3:4, cs], 0.0))
        g_hi = g.astype(bf16)
        g_lo = (g - g_hi.astype(f32)).astype(bf16)
        y = _slab_load(yb, LOCAL_CAP, first=u * LOCAL_CAP).astype(bf16)
        parts.append(_slab_load(xs_ref, SORT_TILE, first=u * SORT_TILE)
                     + _dot_tn(g_hi, y) + _dot_tn(g_lo, y))
    return parts[0] if tiles == 1 else jnp.concatenate(parts, axis=0)


def _combine_specs(ts, n_steps):
    tiles = ts // SORT_TILE
    return [
        pl.BlockSpec((ts * SLAB, LANES), lambda i: (i, 0)),
        pl.BlockSpec((tiles, 1, TABLE_LEN), lambda i: (i, 0, 0), memory_space=pltpu.SMEM),
        pl.BlockSpec((tiles, 1, TABLE_LEN), lambda i: (jnp.minimum(i + 1, n_steps - 1), 0, 0),
                     memory_space=pltpu.SMEM),
        pl.BlockSpec((META_ROWS, ts), lambda i: (0, i)),
        pl.BlockSpec(memory_space=pl.ANY),
    ]


def _combine_scratch(ts):
    tiles = ts // SORT_TILE
    return [pltpu.VMEM((2, tiles * LOCAL_CAP * SLAB, LANES), f32), pltpu.SemaphoreType.DMA((2,))]


def _gelu(z):
    return 0.5 * z * (1.0 + lax.erf(z * (2.0 ** -0.5)))


def _sgu_kernel(xs_ref, ng_ref, wu_ref, wv_ref, lng_ref, lnb_ref,
                ws_ref, bs_ref, wo_ref, mng_ref, wrt_ref, br_ref,
                x3_ref, meta_ref, cnt_ref, vn_ref):
    ts = xs_ref.shape[0] // SLAB
    x2 = _slab_load(xs_ref, ts)
    h = (x2 * _rms(x2) * ng_ref[...]).astype(bf16)
    v = _gelu(_dot(h, wv_ref[...]))
    mu = jnp.mean(v, axis=-1, keepdims=True)
    vc = v - mu
    rstd = lax.rsqrt(jnp.mean(vc * vc, axis=-1, keepdims=True) + EPS)
    vn_ref[...] = (vc * rstd * lng_ref[...] + lnb_ref[...]).astype(bf16)

    pos = lax.broadcasted_iota(i32, (SGU_BLOCK, SGU_BLOCK), 0) // CHUNK
    src = lax.broadcasted_iota(i32, (SGU_BLOCK, SGU_BLOCK), 1) // CHUNK
    acc = x2
    for g in range(SGU_GROUPS):
        cs = slice(g * SGU_GC, (g + 1) * SGU_GC)
        ws = jnp.where(pos >= src, ws_ref[g], jnp.zeros((), bf16))
        u = _gelu(_dot(h, wu_ref[:, cs]))
        mixed = [_dot(ws, vn_ref[nb * SGU_BLOCK:(nb + 1) * SGU_BLOCK, cs]) + bs_ref[:, g:g + 1]
                 for nb in range(ts // SGU_BLOCK)]
        out = (u * jnp.concatenate(mixed, axis=0)).astype(bf16)
        acc = acc + _dot(out, wo_ref[cs, :])
    _slab_store(x3_ref, acc)
    _route(acc, mng_ref[...], wrt_ref[...], br_ref[...], meta_ref, cnt_ref)


def _sgu_layer(x2s, norm_g, w_in, ln_g, ln_b, w_s, b_s, w_out, moe_norm_g, wrt, br, ts):
    t = x2s.shape[0] // SLAB
    n_steps = t // ts
    args = (norm_g.reshape(1, D_MODEL), w_in[:, :SGU_HALF].astype(bf16), w_in[:, SGU_HALF:].astype(bf16),
            ln_g.reshape(1, SGU_HALF), ln_b.reshape(1, SGU_HALF), w_s.astype(bf16), b_s.T,
            w_out.astype(bf16), moe_norm_g.reshape(1, D_MODEL), wrt, br)
    return pl.pallas_call(
        _sgu_kernel,
        grid=(n_steps,),
        in_specs=[pl.BlockSpec((ts * SLAB, LANES), lambda i: (i, 0))] + [_const_spec(a.shape) for a in args],
        out_specs=_route_out_specs(ts, lambda i: i),
        out_shape=_route_out_shapes(t),
        scratch_shapes=[pltpu.VMEM((ts, SGU_HALF), bf16)],
        compiler_params=pltpu.CompilerParams(
            dimension_semantics=("arbitrary",), vmem_limit_bytes=VMEM_LIMIT),
        name="sgu_mixer",
    )(x2s, *args)


def _combine_kernel(xs_ref, tab_ref, tab_next_ref, cmeta_ref, ys_hbm, out_ref, ybuf, sems):
    _slab_store(out_ref, _combine(xs_ref, tab_ref, tab_next_ref, cmeta_ref, ys_hbm, ybuf, sems))


def _final_kernel(xs_ref, tab_ref, tab_next_ref, cmeta_ref, ys_hbm, ng_ref, out_ref, ybuf, sems):
    x = _combine(xs_ref, tab_ref, tab_next_ref, cmeta_ref, ys_hbm, ybuf, sems)
    out_ref[...] = x * _rms(x) * ng_ref[...]


def _combine_layer(xs, table, cmeta, ys, norm_g, ts):
    t = xs.shape[0] // SLAB
    n_steps = t // ts
    final = norm_g is not None
    extra = (norm_g.reshape(1, D_MODEL),) if final else ()
    return pl.pallas_call(
        _final_kernel if final else _combine_kernel,
        grid=(n_steps,),
        in_specs=_combine_specs(ts, n_steps) + [_const_spec(a.shape) for a in extra],
        out_specs=(pl.BlockSpec((ts, D_MODEL), lambda i: (i, 0)) if final
                   else pl.BlockSpec((ts * SLAB, LANES), lambda i: (i, 0))),
        out_shape=jax.ShapeDtypeStruct((t, D_MODEL) if final else (t * SLAB, LANES), f32),
        scratch_shapes=_combine_scratch(ts),
        compiler_params=pltpu.CompilerParams(
            dimension_semantics=("arbitrary",), vmem_limit_bytes=VMEM_LIMIT),
        name="final_norm" if final else "moe_combine",
    )(xs, table, table, cmeta, ys, *extra)


def _moe(xs, meta, cnt, norm_g, w1, w3, w2, layer):
    t = xs.shape[0] // SLAB
    table, off_col, blk_e, nvalid, ztable = _plan(cnt, t)
    sorted_x, cmeta = _dispatch(xs, meta, table, off_col, ztable, norm_g)
    return _experts(sorted_x, blk_e, nvalid, w1, w3, w2, layer), table, cmeta


def _forward(x, gla_norm, gla_w_in, gla_w_gate_up, gla_b_gate, gla_head_g, gla_w_out, sgu_norm, sgu_w_in,
             sgu_ln_g, sgu_ln_b, sgu_w_s, sgu_b_s, sgu_w_out, moe_norm, moe_w_group, moe_b_group,
             moe_w_sub, moe_b_sub, moe_w1, moe_w3, moe_w2, final_norm, *, ts_gla, ts_sgu, ts_fin):
    wrt0, br0 = _router_params(moe_w_group[0], moe_b_group[0], moe_w_sub[0], moe_b_sub[0])
    wrt1, br1 = _router_params(moe_w_group[1], moe_b_group[1], moe_w_sub[1], moe_b_sub[1])
    x1s, meta0, cnt0 = _gla_layer(x, gla_norm[0], gla_w_in[0], gla_w_gate_up[0], gla_b_gate[0], gla_head_g[0],
                                  gla_w_out[0], moe_norm[0], wrt0, br0, ts_gla)
    ys0, table0, cmeta0 = _moe(x1s, meta0, cnt0, moe_norm[0], moe_w1, moe_w3, moe_w2, 0)
    x2s = _combine_layer(x1s, table0, cmeta0, ys0, None, ts_fin)
    x3s, meta1, cnt1 = _sgu_layer(x2s, sgu_norm[0], sgu_w_in[0], sgu_ln_g[0], sgu_ln_b[0],
                                  sgu_w_s[0], sgu_b_s[0], sgu_w_out[0], moe_norm[1], wrt1, br1, ts_sgu)
    ys1, table1, cmeta1 = _moe(x3s, meta1, cnt1, moe_norm[1], moe_w1, moe_w3, moe_w2, 1)
    out = _combine_layer(x3s, table1, cmeta1, ys1, final_norm, ts_fin)
    return out.reshape(x.shape)


def kernel(x, gla_norm, gla_w_in, gla_w_gate_up, gla_b_gate, gla_head_g, gla_w_out, sgu_norm, sgu_w_in, sgu_ln_g, sgu_ln_b, sgu_w_s, sgu_b_s, sgu_w_out, moe_norm, moe_w_group, moe_b_group, moe_w_sub, moe_b_sub, moe_w1, moe_w3, moe_w2, final_norm):
    return _forward(x, gla_norm, gla_w_in, gla_w_gate_up, gla_b_gate, gla_head_g, gla_w_out, sgu_norm,
                    sgu_w_in, sgu_ln_g, sgu_ln_b, sgu_w_s, sgu_b_s, sgu_w_out, moe_norm, moe_w_group,
                    moe_b_group, moe_w_sub, moe_b_sub, moe_w1, moe_w3, moe_w2, final_norm,
                    ts_gla=512, ts_sgu=512, ts_fin=512)
```

```python
import jax
import jax.numpy as jnp
from jax import lax
from jax.experimental import pallas as pl
from jax.experimental.pallas import tpu as pltpu

D_MODEL = 1024
EPS = 1e-6
LANES = 128
SUBLANES = 8
SLAB = D_MODEL // LANES

CHUNK = 64
GLA_HEADS = 4
GLA_DK = 128
GLA_DV = 256
GLA_HK = GLA_HEADS * GLA_DK
GLA_HV = GLA_HEADS * GLA_DV
GLA_GATE_RANK = 16
GLA_TAU = 16.0

SGU_BLOCK = 128
SGU_GROUPS = 4
SGU_HALF = 2048
SGU_GC = SGU_HALF // SGU_GROUPS

N_GROUPS = 4
EXPERTS_PER_GROUP = 8
N_EXPERTS = N_GROUPS * EXPERTS_PER_GROUP
TOP_K = 2
EXPERT_FF = 512
ROUTE_ROWS = 64
META_ROWS = 8

SORT_TILE = 256
RUN_CHUNK = 8
LOCAL_CAP = 768
MAX_CHUNKS = LOCAL_CAP // RUN_CHUNK
EXPERT_ROWS = 256
ZERO_CHUNK = 64
TABLE_LEN = 256
assert LOCAL_CAP >= TOP_K * SORT_TILE + N_EXPERTS * (RUN_CHUNK - 1)
assert 2 * MAX_CHUNKS < TABLE_LEN and EXPERT_ROWS % ZERO_CHUNK == 0

VMEM_LIMIT = 56 * 1024 * 1024

f32 = jnp.float32
bf16 = jnp.bfloat16
i32 = jnp.int32


def _dot(a, b):
    return jnp.dot(a, b, preferred_element_type=f32)


def _dot_tn(a, b):
    return lax.dot_general(a, b, (((0,), (0,)), ((), ())), preferred_element_type=f32)


def _dot_nt(a, b):
    return lax.dot_general(a, b, (((1,), (1,)), ((), ())), preferred_element_type=f32)


def _rms(x):
    return lax.rsqrt(jnp.mean(x * x, axis=-1, keepdims=True) + EPS)


def _slab_load(ref, n_rows, first=0):
    cols = [ref[pl.ds(first * SLAB + j, n_rows, stride=SLAB), :] for j in range(SLAB)]
    return jnp.concatenate(cols, axis=-1)


def _slab_store(ref, val):
    for j in range(SLAB):
        ref[pl.ds(j, val.shape[0], stride=SLAB), :] = val[:, j * LANES:(j + 1) * LANES]


def _const_spec(shape):
    return pl.BlockSpec(shape, lambda *_: (0,) * len(shape))


def _route(x1, norm_g, wr_t, br, meta_ref, cnt_ref):
    n = x1.shape[0]
    hn = (x1 * _rms(x1) * norm_g).astype(bf16)
    lt = _dot_nt(wr_t, hn) + br
    rows = lax.broadcasted_iota(i32, (SUBLANES, n), 0)
    neg = jnp.float32(-jnp.inf)
    lg = jnp.where(rows < N_GROUPS, lt[0:SUBLANES], neg)
    gmax = jnp.max(lg, axis=0, keepdims=True)
    gidx = jnp.min(jnp.where(lg == gmax, rows, SUBLANES), axis=0, keepdims=True)
    g_w = 1.0 / jnp.sum(jnp.exp(lg - gmax), axis=0, keepdims=True)
    chosen = jnp.zeros((SUBLANES, n), f32)
    for g in range(N_GROUPS):
        chosen = jnp.where(gidx == g, lt[SUBLANES * (g + 1):SUBLANES * (g + 2)], chosen)
    m1 = jnp.max(chosen, axis=0, keepdims=True)
    i1 = jnp.min(jnp.where(chosen == m1, rows, SUBLANES), axis=0, keepdims=True)
    rest = jnp.where(rows == i1, neg, chosen)
    m2 = jnp.max(rest, axis=0, keepdims=True)
    i2 = jnp.min(jnp.where(rest == m2, rows, SUBLANES), axis=0, keepdims=True)
    t = jnp.exp(m2 - m1)
    s1 = 1.0 / (1.0 + t)
    s2 = t / (1.0 + t)
    e1 = gidx * EXPERTS_PER_GROUP + i1
    e2 = gidx * EXPERTS_PER_GROUP + i2
    zero = jnp.zeros((1, n), f32)
    meta_ref[...] = jnp.concatenate(
        [e1.astype(f32), e2.astype(f32), g_w * s1, g_w * s2, zero, zero, zero, zero], axis=0)
    ids = lax.broadcasted_iota(i32, (N_EXPERTS, n), 0)
    hits = (ids == e1).astype(f32) + (ids == e2).astype(f32)
    for u in range(n // SORT_TILE):
        cnt_ref[u * N_EXPERTS:(u + 1) * N_EXPERTS, :] = jnp.sum(
            hits[:, u * SORT_TILE:(u + 1) * SORT_TILE], axis=1, keepdims=True)


def _router_params(w_group, b_group, w_sub, b_sub):
    tail = ROUTE_ROWS - SUBLANES - N_EXPERTS
    wrt = jnp.concatenate([
        w_group.T, jnp.zeros((SUBLANES - N_GROUPS, D_MODEL), f32),
        jnp.transpose(w_sub, (0, 2, 1)).reshape(N_EXPERTS, D_MODEL),
        jnp.zeros((tail, D_MODEL), f32)], axis=0)
    br = jnp.concatenate([b_group, jnp.zeros((SUBLANES - N_GROUPS,), f32), b_sub.reshape(N_EXPERTS),
                          jnp.zeros((tail,), f32)]).reshape(ROUTE_ROWS, 1)
    return wrt.astype(bf16), br


def _route_out_specs(ts, index):
    tiles = ts // SORT_TILE
    return [pl.BlockSpec((ts * SLAB, LANES), lambda *g: (index(*g), 0)),
            pl.BlockSpec((META_ROWS, ts), lambda *g: (0, index(*g))),
            pl.BlockSpec((tiles * N_EXPERTS, 1), lambda *g: (index(*g), 0))]


def _route_out_shapes(t):
    return [jax.ShapeDtypeStruct((t * SLAB, LANES), f32),
            jax.ShapeDtypeStruct((META_ROWS, t), f32),
            jax.ShapeDtypeStruct((t // SORT_TILE * N_EXPERTS, 1), f32)]


def _gla_kernel(x_ref, ng_ref, wq_ref, wk_ref, wv_ref, wr_ref, wg_ref, wgu_ref, bg_ref,
                hg_ref, wo_ref, mng_ref, wrt_ref, br_ref,
                x1_ref, meta_ref, cnt_ref, st_ref, o_ref):
    ts = x_ref.shape[1]

    @pl.when(pl.program_id(1) == 0)
    def _():
        st_ref[...] = jnp.zeros_like(st_ref)

    x = x_ref[0]
    h = (x * _rms(x) * ng_ref[...]).astype(bf16)
    q = _dot(h, wq_ref[...]) * (GLA_DK ** -0.5)
    k = _dot(h, wk_ref[...])
    v = _dot(h, wv_ref[...]).astype(bf16)
    glr = _dot(h, wg_ref[...]).astype(bf16)
    gp = _dot(glr, wgu_ref[...]) + bg_ref[...]
    log_a = (jnp.minimum(gp, 0.0) - jnp.log(1.0 + jnp.exp(-jnp.abs(gp)))) * (1.0 / GLA_TAU)

    row = lax.broadcasted_iota(i32, (CHUNK, GLA_HK), 0)
    for c in range(ts // CHUNK):
        rs = slice(c * CHUNK, (c + 1) * CHUNK)
        b = log_a[rs]
        sh = 1
        while sh < CHUNK:
            b = b + jnp.where(row >= sh, pltpu.roll(b, sh, axis=0), 0.0)
            sh *= 2
        b_end = b[CHUNK - 1:CHUNK]
        kdec = (k[rs] * jnp.exp(b_end - b)).astype(bf16)
        decay = jnp.exp(b_end)
        qc = q[rs].astype(bf16)
        vc = v[rs]
        for hd in range(GLA_HEADS):
            ks = slice(hd * GLA_DK, (hd + 1) * GLA_DK)
            vs = slice(hd * GLA_DV, (hd + 1) * GLA_DV)
            st = st_ref[hd] * decay[:, ks] + _dot_tn(vc[:, vs], kdec[:, ks])
            st_ref[hd] = st
            o_ref[rs, vs] = _dot_nt(qc[:, ks], st.astype(bf16))

    r = _dot(h, wr_ref[...])
    gated = []
    for hd in range(GLA_HEADS):
        vs = slice(hd * GLA_DV, (hd + 1) * GLA_DV)
        oh = o_ref[:, vs]
        rh = r[:, vs]
        gated.append(oh * _rms(oh) * hg_ref[:, vs] * (rh / (1.0 + jnp.exp(-rh))))
    y = _dot(jnp.concatenate(gated, axis=-1).astype(bf16), wo_ref[...])
    x1 = x + y
    _slab_store(x1_ref, x1)
    _route(x1, mng_ref[...], wrt_ref[...], br_ref[...], meta_ref, cnt_ref)


def _gla_layer(x, norm_g, w_in, w_gate_up, b_gate, head_g, w_out, moe_norm_g, wrt, br, ts):
    bsz, seq, _ = x.shape
    t = bsz * seq
    wq = w_in[:, 0:GLA_HK].astype(bf16)
    wk = w_in[:, GLA_HK:2 * GLA_HK].astype(bf16)
    wv = w_in[:, 2 * GLA_HK:2 * GLA_HK + GLA_HV].astype(bf16)
    wr = w_in[:, 2 * GLA_HK + GLA_HV:2 * GLA_HK + 2 * GLA_HV].astype(bf16)
    wg = jnp.pad(w_in[:, 2 * GLA_HK + 2 * GLA_HV:], ((0, 0), (0, LANES - GLA_GATE_RANK))).astype(bf16)
    wgu = jnp.pad(w_gate_up, ((0, LANES - GLA_GATE_RANK), (0, 0))).astype(bf16)
    n_s = seq // ts
    args = (x, norm_g.reshape(1, D_MODEL), wq, wk, wv, wr, wg, wgu, b_gate.reshape(1, GLA_HK),
            head_g.reshape(1, GLA_HV), w_out.astype(bf16), moe_norm_g.reshape(1, D_MODEL), wrt, br)
    in_specs = [pl.BlockSpec((1, ts, D_MODEL), lambda b, s: (b, s, 0))]
    in_specs += [_const_spec(a.shape) for a in args[1:]]
    return pl.pallas_call(
        _gla_kernel,
        grid=(bsz, n_s),
        in_specs=in_specs,
        out_specs=_route_out_specs(ts, lambda b, s: b * n_s + s),
        out_shape=_route_out_shapes(t),
        scratch_shapes=[pltpu.VMEM((GLA_HEADS, GLA_DV, GLA_DK), f32),
                        pltpu.VMEM((ts, GLA_HV), f32)],
        compiler_params=pltpu.CompilerParams(
            dimension_semantics=("arbitrary", "arbitrary"), vmem_limit_bytes=VMEM_LIMIT),
        name="gla_mixer",
    )(*args)


def _n_expert_blocks(t):
    worst = t * TOP_K + N_EXPERTS * (RUN_CHUNK + EXPERT_ROWS - 2)
    return -(-worst // EXPERT_ROWS)


def _zero_table_len(t):
    tail = _n_expert_blocks(t) * EXPERT_ROWS - t * TOP_K
    per_expert = -(-(EXPERT_ROWS + RUN_CHUNK) // ZERO_CHUNK)
    n = (N_EXPERTS - 1) * per_expert + -(-tail // ZERO_CHUNK) + per_expert
    return -(-(n + 1) // LANES) * LANES


def _cumsum(x, axis):
    x = jnp.moveaxis(x, axis, -1)
    n = x.shape[-1]
    upto = jnp.arange(n, dtype=i32)[:, None] <= jnp.arange(n, dtype=i32)[None, :]
    return jnp.moveaxis(jnp.sum(x[..., :, None] * upto.astype(i32), axis=-2), -1, axis)


def _flat_chunks(n_per, max_n):
    cum = _cumsum(n_per, -1)
    c = jnp.arange(max_n, dtype=i32)
    seg = jnp.minimum(jnp.sum((c[:, None] >= cum[..., None, :]).astype(i32), axis=-1), n_per.shape[-1] - 1)
    onehot = (seg[..., None] == jnp.arange(n_per.shape[-1], dtype=i32)).astype(i32)
    return onehot, c - _pick(onehot, cum - n_per), cum[..., -1]


def _pick(onehot, per_segment):
    return jnp.sum(onehot * per_segment[..., None, :], axis=-1)


def _plan(cnt, t):
    tiles = t // SORT_TILE
    cnt = cnt.reshape(tiles, N_EXPERTS).astype(i32)
    nch = (cnt + RUN_CHUNK - 1) // RUN_CHUNK
    counts = jnp.sum(cnt, axis=0)
    padded = ((counts + RUN_CHUNK - 1 + EXPERT_ROWS - 1) // EXPERT_ROWS) * EXPERT_ROWS
    pad_end = _cumsum(padded, 0)
    pad_start = pad_end - padded
    base = pad_start[None, :] + _cumsum(cnt, 0) - cnt
    onehot, j, ntot = _flat_chunks(nch, MAX_CHUNKS)
    off = (_cumsum(nch, 1) - nch) * RUN_CHUNK
    loc = (_pick(onehot, off) + j * RUN_CHUNK) * SLAB
    glb = (_pick(onehot, base) + j * RUN_CHUNK) * SLAB
    fill = jnp.zeros((tiles, TABLE_LEN - 2 * MAX_CHUNKS - 1), i32)
    table = jnp.concatenate([loc, glb, ntot[:, None], fill], axis=1).reshape(tiles, 1, TABLE_LEN)
    off_col = off.astype(f32).reshape(tiles, N_EXPERTS, 1)

    nb = _n_expert_blocks(t)
    n_blk = padded // EXPERT_ROWS
    n_big = n_blk // BIG_BLOCKS
    n_items = n_big + n_blk % BIG_BLOCKS
    ionehot, k, n_total = _flat_chunks(n_items, _n_work_items(t))
    k_big = _pick(ionehot, n_big)
    item_big = (k < k_big).astype(i32)
    item_start = _pick(ionehot, pad_start // EXPERT_ROWS) + jnp.where(
        k < k_big, BIG_BLOCKS * k, BIG_BLOCKS * k_big + k - k_big)
    icum = _cumsum(n_items, 0)
    item_first = jnp.concatenate([icum - n_items, n_total[None], pad_end[-1:] // EXPERT_ROWS])
    items = (item_first, item_start, item_big)

    region_end = jnp.concatenate([pad_start[1:], jnp.full((1,), nb * EXPERT_ROWS, i32)])
    zlen = region_end - (pad_start + counts)
    nz = (zlen + ZERO_CHUNK - 1) // ZERO_CHUNK
    zl = _zero_table_len(t)
    zonehot, zj, nztot = _flat_chunks(nz, zl - 1)
    zstart = (_pick(zonehot, region_end) - (zj + 1) * ZERO_CHUNK) * SLAB
    ztable = jnp.concatenate([zstart, nztot[None]]).reshape(1, 1, zl)
    return table, off_col, items, ztable


def _chunk_copy(stage, out_hbm, loc, glb, sem):
    return pltpu.make_async_copy(stage.at[pl.ds(pl.multiple_of(loc, SLAB), RUN_CHUNK * SLAB), :],
                                 out_hbm.at[pl.ds(pl.multiple_of(glb, SLAB), RUN_CHUNK * SLAB), :], sem)


def _dispatch_kernel(tab_ref, ztab_ref, xs_ref, meta_ref, offc_ref, ng_ref,
                     out_hbm, cmeta_ref, stage, zbuf, sems, zsem, prev_n):
    t = pl.program_id(0)
    slot = t % 2
    zl = ztab_ref.shape[2]

    def zero_copy(z):
        dst = pl.multiple_of(ztab_ref[0, 0, z], SLAB)
        return pltpu.make_async_copy(zbuf, out_hbm.at[pl.ds(dst, ZERO_CHUNK * SLAB), :], zsem)

    @pl.when(t == 0)
    def _():
        zbuf[...] = jnp.zeros_like(zbuf)
        nz = ztab_ref[0, 0, zl - 1]
        lax.fori_loop(0, nz, lambda z, c: (zero_copy(z).start(), c)[1], 0)
        lax.fori_loop(0, nz, lambda z, c: (zero_copy(z).wait(), c)[1], 0)

    x = _slab_load(xs_ref, SORT_TILE)
    hn = (x * _rms(x) * ng_ref[...]).astype(bf16)
    e1 = meta_ref[0:1, :].astype(i32)
    e2 = meta_ref[1:2, :].astype(i32)
    ids = lax.broadcasted_iota(i32, (N_EXPERTS, SORT_TILE), 0)
    oh0 = ids == e1
    oh1 = ids == e2
    before = (lax.broadcasted_iota(i32, (SORT_TILE, SORT_TILE), 0)
              < lax.broadcasted_iota(i32, (SORT_TILE, SORT_TILE), 1)).astype(bf16)
    c0 = _dot(oh0.astype(bf16), before)
    c1 = _dot(oh1.astype(bf16), before)
    n0 = jnp.sum(oh0.astype(f32), axis=1, keepdims=True)
    offc = offc_ref[0]
    pos0 = jnp.sum(jnp.where(oh0, offc + c0, 0.0), axis=0, keepdims=True)
    pos1 = jnp.sum(jnp.where(oh1, offc + n0 + c1, 0.0), axis=0, keepdims=True)
    zero = jnp.zeros((1, SORT_TILE), f32)
    cmeta_ref[...] = jnp.concatenate(
        [pos0, pos1, meta_ref[2:3, :], meta_ref[3:4, :], zero, zero, zero, zero], axis=0)
    p = lax.broadcasted_iota(i32, (LOCAL_CAP, SORT_TILE), 0).astype(f32)
    perm = jnp.logical_or(p == pos0, p == pos1).astype(bf16)
    _slab_store(stage.at[slot], _dot(perm, hn))

    def wait_chunks(n, s):
        def body(c, carry):
            _chunk_copy(stage.at[s], out_hbm, 0, 0, sems.at[s]).wait()
            return carry
        lax.fori_loop(0, n, body, 0)

    @pl.when(t > 0)
    def _():
        wait_chunks(prev_n[0], 1 - slot)

    ntot = tab_ref[0, 0, 2 * MAX_CHUNKS]

    def issue(c, carry):
        _chunk_copy(stage.at[slot], out_hbm, tab_ref[0, 0, c], tab_ref[0, 0, MAX_CHUNKS + c],
                    sems.at[slot]).start()
        return carry
    lax.fori_loop(0, ntot, issue, 0)
    prev_n[0] = ntot

    @pl.when(t == pl.num_programs(0) - 1)
    def _():
        wait_chunks(ntot, slot)


def _dispatch(xs, meta, table, off_col, ztable, norm_g):
    t = xs.shape[0] // SLAB
    tiles = t // SORT_TILE
    zl = ztable.shape[2]
    nb = _n_expert_blocks(t)
    return pl.pallas_call(
        _dispatch_kernel,
        grid=(tiles,),
        in_specs=[
            pl.BlockSpec((1, 1, TABLE_LEN), lambda i: (i, 0, 0), memory_space=pltpu.SMEM),
            pl.BlockSpec((1, 1, zl), lambda i: (0, 0, 0), memory_space=pltpu.SMEM),
            pl.BlockSpec((SORT_TILE * SLAB, LANES), lambda i: (i, 0)),
            pl.BlockSpec((META_ROWS, SORT_TILE), lambda i: (0, i)),
            pl.BlockSpec((1, N_EXPERTS, 1), lambda i: (i, 0, 0)),
            _const_spec((1, D_MODEL)),
        ],
        out_specs=[pl.BlockSpec(memory_space=pl.ANY),
                   pl.BlockSpec((META_ROWS, SORT_TILE), lambda i: (0, i))],
        out_shape=[jax.ShapeDtypeStruct((nb * EXPERT_ROWS * SLAB, LANES), f32),
                   jax.ShapeDtypeStruct((META_ROWS, t), f32)],
        scratch_shapes=[pltpu.VMEM((2, LOCAL_CAP * SLAB, LANES), f32),
                        pltpu.VMEM((ZERO_CHUNK * SLAB, LANES), f32),
                        pltpu.SemaphoreType.DMA((2,)),
                        pltpu.SemaphoreType.DMA(()),
                        pltpu.SMEM((1,), i32)],
        compiler_params=pltpu.CompilerParams(
            dimension_semantics=("arbitrary",), vmem_limit_bytes=VMEM_LIMIT),
        name="moe_dispatch",
    )(table, ztable, xs, meta, off_col, norm_g.reshape(1, D_MODEL))


X_BUFFERS = 3
Y_BUFFERS = 2
BIG_BLOCKS = 4


def _n_work_items(t):
    return _n_expert_blocks(t) // BIG_BLOCKS + (BIG_BLOCKS - 1) * N_EXPERTS + 1


def _expert_kernel(ifirst_ref, istart_ref, ibig_ref, w1_ref, w3_ref, w2_ref, xs_hbm, ys_hbm,
                   xbuf, ybuf, xsem, ysem, w1b, w3b, w2b):
    e = pl.program_id(0)
    lo = ifirst_ref[e]
    hi = ifirst_ref[e + 1]
    total = ifirst_ref[N_EXPERTS]
    nvalid = ifirst_ref[N_EXPERTS + 1]
    blk = EXPERT_ROWS * SLAB
    n_blocks = ys_hbm.shape[0] // blk

    def hbm_rows(ref, i, big):
        return ref.at[pl.ds(pl.multiple_of(istart_ref[i] * blk, blk), (BIG_BLOCKS if big else 1) * blk), :]

    def x_copy(i, big):
        slot = i % X_BUFFERS
        return pltpu.make_async_copy(hbm_rows(xs_hbm, i, big),
                                     xbuf.at[slot, pl.ds(0, (BIG_BLOCKS if big else 1) * blk), :], xsem.at[slot])

    def y_copy(i, big):
        slot = i % Y_BUFFERS
        return pltpu.make_async_copy(ybuf.at[slot, pl.ds(0, (BIG_BLOCKS if big else 1) * blk), :],
                                     hbm_rows(ys_hbm, i, big), ysem.at[slot])

    def by_size(i, fn):
        @pl.when(ibig_ref[i] != 0)
        def _():
            fn(True)

        @pl.when(ibig_ref[i] == 0)
        def _():
            fn(False)

    @pl.when(e == 0)
    def _():
        for k in range(X_BUFFERS - 1):
            @pl.when(k < total)
            def _():
                by_size(k, lambda big: x_copy(k, big).start())

    @pl.when(hi > lo)
    def _():
        w1b[...] = w1_ref[0, 0].astype(bf16)
        w3b[...] = w3_ref[0, 0].astype(bf16)
        w2b[...] = w2_ref[0, 0].astype(bf16)

    def item(i, carry):
        ahead = i + X_BUFFERS - 1

        @pl.when(ahead < total)
        def _():
            by_size(ahead, lambda big: x_copy(ahead, big).start())

        def run(big):
            rows = (BIG_BLOCKS if big else 1) * EXPERT_ROWS
            x_copy(i, big).wait()
            hn = _slab_load(xbuf.at[i % X_BUFFERS], rows).astype(bf16)
            h1 = _dot(hn, w1b[...])
            h3 = _dot(hn, w3b[...])
            hid = ((h1 / (1.0 + jnp.exp(-h1))) * h3).astype(bf16)
            y = _dot(hid, w2b[...])

            @pl.when(i >= Y_BUFFERS)
            def _():
                by_size(i - Y_BUFFERS, lambda b: y_copy(i - Y_BUFFERS, b).wait())

            _slab_store(ybuf.at[i % Y_BUFFERS], y)
            y_copy(i, big).start()

        by_size(i, run)
        return carry

    lax.fori_loop(lo, hi, item, 0)

    @pl.when(e == pl.num_programs(0) - 1)
    def _():
        for k in range(1, Y_BUFFERS + 1):
            @pl.when(total >= k)
            def _():
                by_size(total - k, lambda big: y_copy(total - k, big).wait())
        ybuf[0, pl.ds(0, blk), :] = jnp.zeros((blk, LANES), f32)

        def tail(g):
            return pltpu.make_async_copy(ybuf.at[0, pl.ds(0, blk), :],
                                         ys_hbm.at[pl.ds(pl.multiple_of(g * blk, blk), blk), :], ysem.at[0])
        lax.fori_loop(nvalid, n_blocks, lambda g, c: (tail(g).start(), c)[1], 0)
        lax.fori_loop(nvalid, n_blocks, lambda g, c: (tail(g).wait(), c)[1], 0)


def _experts(xs, item_first, item_start, item_big, w1, w3, w2, layer):
    w_index = lambda e, *_: (layer, e, 0, 0)
    big_rows = BIG_BLOCKS * EXPERT_ROWS * SLAB
    grid_spec = pltpu.PrefetchScalarGridSpec(
        num_scalar_prefetch=3,
        grid=(N_EXPERTS,),
        in_specs=[
            pl.BlockSpec((1, 1, D_MODEL, EXPERT_FF), w_index),
            pl.BlockSpec((1, 1, D_MODEL, EXPERT_FF), w_index),
            pl.BlockSpec((1, 1, EXPERT_FF, D_MODEL), w_index),
            pl.BlockSpec(memory_space=pl.ANY),
        ],
        out_specs=pl.BlockSpec(memory_space=pl.ANY),
        scratch_shapes=[pltpu.VMEM((X_BUFFERS, big_rows, LANES), f32),
                        pltpu.VMEM((Y_BUFFERS, big_rows, LANES), f32),
                        pltpu.SemaphoreType.DMA((X_BUFFERS,)),
                        pltpu.SemaphoreType.DMA((Y_BUFFERS,)),
                        pltpu.VMEM((D_MODEL, EXPERT_FF), bf16),
                        pltpu.VMEM((D_MODEL, EXPERT_FF), bf16),
                        pltpu.VMEM((EXPERT_FF, D_MODEL), bf16)],
    )
    return pl.pallas_call(
        _expert_kernel,
        grid_spec=grid_spec,
        out_shape=jax.ShapeDtypeStruct(xs.shape, f32),
        compiler_params=pltpu.CompilerParams(
            dimension_semantics=("arbitrary",), vmem_limit_bytes=VMEM_LIMIT),
        name="moe_experts",
    )(item_first, item_start, item_big, w1, w3, w2, xs)


def _combine(xs_ref, tab_ref, tab_next_ref, cmeta_ref, ys_hbm, ybuf, sems):
    i = pl.program_id(0)
    n_steps = pl.num_programs(0)
    tiles = xs_ref.shape[0] // (SORT_TILE * SLAB)
    slot = i % 2

    def copy(s, u, loc, glb):
        dst = pl.multiple_of(u * LOCAL_CAP * SLAB + loc, SLAB)
        return pltpu.make_async_copy(ys_hbm.at[pl.ds(pl.multiple_of(glb, SLAB), RUN_CHUNK * SLAB), :],
                                     ybuf.at[s, pl.ds(dst, RUN_CHUNK * SLAB), :], sems.at[s])

    def fetch(tab, s):
        ybuf[s] = jnp.zeros(ybuf.shape[1:], f32)
        for u in range(tiles):
            def body(c, carry):
                copy(s, u, tab[u, 0, c], tab[u, 0, MAX_CHUNKS + c]).start()
                return carry
            lax.fori_loop(0, tab[u, 0, 2 * MAX_CHUNKS], body, 0)

    @pl.when(i == 0)
    def _():
        fetch(tab_ref, 0)

    @pl.when(i + 1 < n_steps)
    def _():
        fetch(tab_next_ref, 1 - slot)

    for u in range(tiles):
        def body(c, carry):
            copy(slot, u, 0, 0).wait()
            return carry
        lax.fori_loop(0, tab_ref[u, 0, 2 * MAX_CHUNKS], body, 0)

    yb = ybuf.at[slot]
    p = lax.broadcasted_iota(i32, (LOCAL_CAP, SORT_TILE), 0).astype(f32)
    parts = []
    for u in range(tiles):
        cs = slice(u * SORT_TILE, (u + 1) * SORT_TILE)
        pos0, pos1 = cmeta_ref[0:1, cs], cmeta_ref[1:2, cs]
        g = (jnp.where(p == pos0, cmeta_ref[2:3, cs], 0.0)
             + jnp.where(p == pos1, cmeta_ref[3:4, cs], 0.0))
        g_hi = g.astype(bf16)
        g_lo = (g - g_hi.astype(f32)).astype(bf16)
        y = _slab_load(yb, LOCAL_CAP, first=u * LOCAL_CAP).astype(bf16)
        parts.append(_slab_load(xs_ref, SORT_TILE, first=u * SORT_TILE)
                     + _dot_tn(g_hi, y) + _dot_tn(g_lo, y))
    return parts[0] if tiles == 1 else jnp.concatenate(parts, axis=0)


def _combine_specs(ts, n_steps):
    tiles = ts // SORT_TILE
    return [
        pl.BlockSpec((ts * SLAB, LANES), lambda i: (i, 0)),
        pl.BlockSpec((tiles, 1, TABLE_LEN), lambda i: (i, 0, 0), memory_space=pltpu.SMEM),
        pl.BlockSpec((tiles, 1, TABLE_LEN), lambda i: (jnp.minimum(i + 1, n_steps - 1), 0, 0),
                     memory_space=pltpu.SMEM),
        pl.BlockSpec((META_ROWS, ts), lambda i: (0, i)),
        pl.BlockSpec(memory_space=pl.ANY),
    ]


def _combine_scratch(ts):
    tiles = ts // SORT_TILE
    return [pltpu.VMEM((2, tiles * LOCAL_CAP * SLAB, LANES), f32), pltpu.SemaphoreType.DMA((2,))]


def _gelu(z):
    return 0.5 * z * (1.0 + lax.erf(z * (2.0 ** -0.5)))


def _sgu_kernel(xs_ref, ng_ref, wu_ref, wv_ref, lng_ref, lnb_ref,
                ws_ref, bs_ref, wo_ref, mng_ref, wrt_ref, br_ref,
                x3_ref, meta_ref, cnt_ref, vn_ref):
    ts = xs_ref.shape[0] // SLAB
    x2 = _slab_load(xs_ref, ts)
    h = (x2 * _rms(x2) * ng_ref[...]).astype(bf16)
    v = _gelu(_dot(h, wv_ref[...]))
    mu = jnp.mean(v, axis=-1, keepdims=True)
    vc = v - mu
    rstd = lax.rsqrt(jnp.mean(vc * vc, axis=-1, keepdims=True) + EPS)
    vn_ref[...] = (vc * rstd * lng_ref[...] + lnb_ref[...]).astype(bf16)

    pos = lax.broadcasted_iota(i32, (SGU_BLOCK, SGU_BLOCK), 0) // CHUNK
    src = lax.broadcasted_iota(i32, (SGU_BLOCK, SGU_BLOCK), 1) // CHUNK
    acc = x2
    for g in range(SGU_GROUPS):
        cs = slice(g * SGU_GC, (g + 1) * SGU_GC)
        ws = jnp.where(pos >= src, ws_ref[g], jnp.zeros((), bf16))
        u = _gelu(_dot(h, wu_ref[:, cs]))
        mixed = [_dot(ws, vn_ref[nb * SGU_BLOCK:(nb + 1) * SGU_BLOCK, cs]) + bs_ref[:, g:g + 1]
                 for nb in range(ts // SGU_BLOCK)]
        out = (u * jnp.concatenate(mixed, axis=0)).astype(bf16)
        acc = acc + _dot(out, wo_ref[cs, :])
    _slab_store(x3_ref, acc)
    _route(acc, mng_ref[...], wrt_ref[...], br_ref[...], meta_ref, cnt_ref)


def _sgu_layer(x2s, norm_g, w_in, ln_g, ln_b, w_s, b_s, w_out, moe_norm_g, wrt, br, ts):
    t = x2s.shape[0] // SLAB
    n_steps = t // ts
    args = (norm_g.reshape(1, D_MODEL), w_in[:, :SGU_HALF].astype(bf16), w_in[:, SGU_HALF:].astype(bf16),
            ln_g.reshape(1, SGU_HALF), ln_b.reshape(1, SGU_HALF), w_s.astype(bf16), b_s.T,
            w_out.astype(bf16), moe_norm_g.reshape(1, D_MODEL), wrt, br)
    return pl.pallas_call(
        _sgu_kernel,
        grid=(n_steps,),
        in_specs=[pl.BlockSpec((ts * SLAB, LANES), lambda i: (i, 0))] + [_const_spec(a.shape) for a in args],
        out_specs=_route_out_specs(ts, lambda i: i),
        out_shape=_route_out_shapes(t),
        scratch_shapes=[pltpu.VMEM((ts, SGU_HALF), bf16)],
        compiler_params=pltpu.CompilerParams(
            dimension_semantics=("arbitrary",), vmem_limit_bytes=VMEM_LIMIT),
        name="sgu_mixer",
    )(x2s, *args)


def _combine_kernel(xs_ref, tab_ref, tab_next_ref, cmeta_ref, ys_hbm, out_ref, ybuf, sems):
    _slab_store(out_ref, _combine(xs_ref, tab_ref, tab_next_ref, cmeta_ref, ys_hbm, ybuf, sems))


def _final_kernel(xs_ref, tab_ref, tab_next_ref, cmeta_ref, ys_hbm, ng_ref, out_ref, ybuf, sems):
    x = _combine(xs_ref, tab_ref, tab_next_ref, cmeta_ref, ys_hbm, ybuf, sems)
    out_ref[...] = x * _rms(x) * ng_ref[...]


def _combine_layer(xs, table, cmeta, ys, norm_g, ts):
    t = xs.shape[0] // SLAB
    n_steps = t // ts
    final = norm_g is not None
    extra = (norm_g.reshape(1, D_MODEL),) if final else ()
    return pl.pallas_call(
        _final_kernel if final else _combine_kernel,
        grid=(n_steps,),
        in_specs=_combine_specs(ts, n_steps) + [_const_spec(a.shape) for a in extra],
        out_specs=(pl.BlockSpec((ts, D_MODEL), lambda i: (i, 0)) if final
                   else pl.BlockSpec((ts * SLAB, LANES), lambda i: (i, 0))),
        out_shape=jax.ShapeDtypeStruct((t, D_MODEL) if final else (t * SLAB, LANES), f32),
        scratch_shapes=_combine_scratch(ts),
        compiler_params=pltpu.CompilerParams(
            dimension_semantics=("arbitrary",), vmem_limit_bytes=VMEM_LIMIT),
        name="final_norm" if final else "moe_combine",
    )(xs, table, table, cmeta, ys, *extra)


def _moe(xs, meta, cnt, norm_g, w1, w3, w2, layer):
    t = xs.shape[0] // SLAB
    table, off_col, items, ztable = _plan(cnt, t)
    sorted_x, cmeta = _dispatch(xs, meta, table, off_col, ztable, norm_g)
    return _experts(sorted_x, *items, w1, w3, w2, layer), table, cmeta


def _forward(x, gla_norm, gla_w_in, gla_w_gate_up, gla_b_gate, gla_head_g, gla_w_out, sgu_norm, sgu_w_in,
             sgu_ln_g, sgu_ln_b, sgu_w_s, sgu_b_s, sgu_w_out, moe_norm, moe_w_group, moe_b_group,
             moe_w_sub, moe_b_sub, moe_w1, moe_w3, moe_w2, final_norm, *, ts_gla, ts_sgu, ts_fin):
    wrt0, br0 = _router_params(moe_w_group[0], moe_b_group[0], moe_w_sub[0], moe_b_sub[0])
    wrt1, br1 = _router_params(moe_w_group[1], moe_b_group[1], moe_w_sub[1], moe_b_sub[1])
    x1s, meta0, cnt0 = _gla_layer(x, gla_norm[0], gla_w_in[0], gla_w_gate_up[0], gla_b_gate[0], gla_head_g[0],
                                  gla_w_out[0], moe_norm[0], wrt0, br0, ts_gla)
    ys0, table0, cmeta0 = _moe(x1s, meta0, cnt0, moe_norm[0], moe_w1, moe_w3, moe_w2, 0)
    x2s = _combine_layer(x1s, table0, cmeta0, ys0, None, ts_fin)
    x3s, meta1, cnt1 = _sgu_layer(x2s, sgu_norm[0], sgu_w_in[0], sgu_ln_g[0], sgu_ln_b[0],
                                  sgu_w_s[0], sgu_b_s[0], sgu_w_out[0], moe_norm[1], wrt1, br1, ts_sgu)
    ys1, table1, cmeta1 = _moe(x3s, meta1, cnt1, moe_norm[1], moe_w1, moe_w3, moe_w2, 1)
    out = _combine_layer(x3s, table1, cmeta1, ys1, final_norm, ts_fin)
    return out.reshape(x.shape)


def kernel(x, gla_norm, gla_w_in, gla_w_gate_up, gla_b_gate, gla_head_g, gla_w_out, sgu_norm, sgu_w_in, sgu_ln_g, sgu_ln_b, sgu_w_s, sgu_b_s, sgu_w_out, moe_norm, moe_w_group, moe_b_group, moe_w_sub, moe_b_sub, moe_w1, moe_w3, moe_w2, final_norm):
    return _forward(x, gla_norm, gla_w_in, gla_w_gate_up, gla_b_gate, gla_head_g, gla_w_out, sgu_norm,
                    sgu_w_in, sgu_ln_g, sgu_ln_b, sgu_w_s, sgu_b_s, sgu_w_out, moe_norm, moe_w_group,
                    moe_b_group, moe_w_sub, moe_b_sub, moe_w1, moe_w3, moe_w2, final_norm,
                    ts_gla=1024, ts_sgu=1024, ts_fin=512)
```

```python
import jax
import jax.numpy as jnp
from jax import lax
from jax.experimental import pallas as pl
from jax.experimental.pallas import tpu as pltpu

D_MODEL = 1024
HALF = D_MODEL // 2
EPS = 1e-6
LANES = 128
SUBLANES = 8

CHUNK = 64
GLA_HEADS = 4
GLA_DK = 128
GLA_DV = 256
GLA_HK = GLA_HEADS * GLA_DK
GLA_HV = GLA_HEADS * GLA_DV
GLA_GATE_RANK = 16
GLA_TAU = 16.0

SGU_BLOCK = 128
SGU_GROUPS = 4
SGU_HALF = 2048
SGU_GC = SGU_HALF // SGU_GROUPS

N_GROUPS = 4
EXPERTS_PER_GROUP = 8
N_EXPERTS = N_GROUPS * EXPERTS_PER_GROUP
TOP_K = 2
EXPERT_FF = 512
ROUTE_ROWS = 64
META_ROWS = 8

SORT_TILE = 256
RUN_CHUNK = SUBLANES
LOCAL_CAP = 768
MAX_CHUNKS = LOCAL_CAP // RUN_CHUNK
EXPERT_ROWS = 256
ZERO_CHUNK = EXPERT_ROWS
TABLE_LEN = 256
assert LOCAL_CAP >= TOP_K * SORT_TILE + N_EXPERTS * (RUN_CHUNK - 1)
assert 2 * MAX_CHUNKS < TABLE_LEN

VMEM_LIMIT = 56 * 1024 * 1024

f32 = jnp.float32
bf16 = jnp.bfloat16
i32 = jnp.int32
u32 = jnp.uint32


def _dot(a, b):
    return jnp.dot(a, b, preferred_element_type=f32)


def _dot_tn(a, b):
    return lax.dot_general(a, b, (((0,), (0,)), ((), ())), preferred_element_type=f32)


def _dot_nt(a, b):
    return lax.dot_general(a, b, (((1,), (1,)), ((), ())), preferred_element_type=f32)


def _rms(x):
    return lax.rsqrt(jnp.mean(x * x, axis=-1, keepdims=True) + EPS)


def _pack_rows(v):
    lo = lax.bitcast_convert_type(v[:, :HALF], u32)
    hi = lax.bitcast_convert_type(v[:, HALF:], u32)
    return lax.shift_right_logical(lo, jnp.uint32(16)) | (hi & jnp.uint32(0xFFFF0000))


def _unpack_rows(w):
    lo = lax.bitcast_convert_type(lax.shift_left(w, jnp.uint32(16)), f32)
    hi = lax.bitcast_convert_type(w & jnp.uint32(0xFFFF0000), f32)
    return jnp.concatenate([lo, hi], axis=-1).astype(bf16)


def _const_spec(shape):
    return pl.BlockSpec(shape, lambda *_: (0,) * len(shape))


def _route(x1, norm_g, wr_t, br, meta_ref, cnt_ref):
    n = x1.shape[0]
    hn = (x1 * _rms(x1) * norm_g).astype(bf16)
    lt = _dot_nt(wr_t, hn) + br
    rows = lax.broadcasted_iota(i32, (SUBLANES, n), 0)
    neg = jnp.float32(-jnp.inf)
    lg = jnp.where(rows < N_GROUPS, lt[0:SUBLANES], neg)
    gmax = jnp.max(lg, axis=0, keepdims=True)
    gidx = jnp.min(jnp.where(lg == gmax, rows, SUBLANES), axis=0, keepdims=True)
    g_w = 1.0 / jnp.sum(jnp.exp(lg - gmax), axis=0, keepdims=True)
    chosen = jnp.zeros((SUBLANES, n), f32)
    for g in range(N_GROUPS):
        chosen = jnp.where(gidx == g, lt[SUBLANES * (g + 1):SUBLANES * (g + 2)], chosen)
    m1 = jnp.max(chosen, axis=0, keepdims=True)
    i1 = jnp.min(jnp.where(chosen == m1, rows, SUBLANES), axis=0, keepdims=True)
    rest = jnp.where(rows == i1, neg, chosen)
    m2 = jnp.max(rest, axis=0, keepdims=True)
    i2 = jnp.min(jnp.where(rest == m2, rows, SUBLANES), axis=0, keepdims=True)
    t = jnp.exp(m2 - m1)
    s1 = 1.0 / (1.0 + t)
    s2 = t / (1.0 + t)
    e1 = gidx * EXPERTS_PER_GROUP + i1
    e2 = gidx * EXPERTS_PER_GROUP + i2
    zero = jnp.zeros((1, n), f32)
    meta_ref[...] = jnp.concatenate(
        [e1.astype(f32), e2.astype(f32), g_w * s1, g_w * s2, zero, zero, zero, zero], axis=0)
    ids = lax.broadcasted_iota(i32, (N_EXPERTS, n), 0)
    hits = (ids == e1).astype(f32) + (ids == e2).astype(f32)
    for u in range(n // SORT_TILE):
        cnt_ref[u * N_EXPERTS:(u + 1) * N_EXPERTS, :] = jnp.sum(
            hits[:, u * SORT_TILE:(u + 1) * SORT_TILE], axis=1, keepdims=True)


def _router_params(w_group, b_group, w_sub, b_sub):
    tail = ROUTE_ROWS - SUBLANES - N_EXPERTS
    wrt = jnp.concatenate([
        w_group.T, jnp.zeros((SUBLANES - N_GROUPS, D_MODEL), f32),
        jnp.transpose(w_sub, (0, 2, 1)).reshape(N_EXPERTS, D_MODEL),
        jnp.zeros((tail, D_MODEL), f32)], axis=0)
    br = jnp.concatenate([b_group, jnp.zeros((SUBLANES - N_GROUPS,), f32), b_sub.reshape(N_EXPERTS),
                          jnp.zeros((tail,), f32)]).reshape(ROUTE_ROWS, 1)
    return wrt.astype(bf16), br


def _route_out_specs(ts, index):
    tiles = ts // SORT_TILE
    return [pl.BlockSpec((ts, D_MODEL), lambda *g: (index(*g), 0)),
            pl.BlockSpec((META_ROWS, ts), lambda *g: (0, index(*g))),
            pl.BlockSpec((tiles * N_EXPERTS, 1), lambda *g: (index(*g), 0))]


def _route_out_shapes(t):
    return [jax.ShapeDtypeStruct((t, D_MODEL), f32),
            jax.ShapeDtypeStruct((META_ROWS, t), f32),
            jax.ShapeDtypeStruct((t // SORT_TILE * N_EXPERTS, 1), f32)]


def _gla_kernel(x_ref, ng_ref, wq_ref, wk_ref, wv_ref, wr_ref, wg_ref, wgu_ref, bg_ref,
                hg_ref, wo_ref, mng_ref, wrt_ref, br_ref,
                x1_ref, meta_ref, cnt_ref, st_ref, o_ref):
    ts = x_ref.shape[1]

    @pl.when(pl.program_id(1) == 0)
    def _():
        st_ref[...] = jnp.zeros_like(st_ref)

    x = x_ref[0]
    h = (x * _rms(x) * ng_ref[...]).astype(bf16)
    q = _dot(h, wq_ref[...]) * (GLA_DK ** -0.5)
    k = _dot(h, wk_ref[...])
    v = _dot(h, wv_ref[...]).astype(bf16)
    glr = _dot(h, wg_ref[...]).astype(bf16)
    gp = _dot(glr, wgu_ref[...]) + bg_ref[...]
    log_a = (jnp.minimum(gp, 0.0) - jnp.log(1.0 + jnp.exp(-jnp.abs(gp)))) * (1.0 / GLA_TAU)

    row = lax.broadcasted_iota(i32, (CHUNK, GLA_HK), 0)
    for c in range(ts // CHUNK):
        rs = slice(c * CHUNK, (c + 1) * CHUNK)
        b = log_a[rs]
        sh = 1
        while sh < CHUNK:
            b = b + jnp.where(row >= sh, pltpu.roll(b, sh, axis=0), 0.0)
            sh *= 2
        b_end = b[CHUNK - 1:CHUNK]
        kdec = (k[rs] * jnp.exp(b_end - b)).astype(bf16)
        decay = jnp.exp(b_end)
        qc = q[rs].astype(bf16)
        vc = v[rs]
        for hd in range(GLA_HEADS):
            ks = slice(hd * GLA_DK, (hd + 1) * GLA_DK)
            vs = slice(hd * GLA_DV, (hd + 1) * GLA_DV)
            st = st_ref[hd] * decay[:, ks] + _dot_tn(vc[:, vs], kdec[:, ks])
            st_ref[hd] = st
            o_ref[rs, vs] = _dot_nt(qc[:, ks], st.astype(bf16))

    r = _dot(h, wr_ref[...])
    gated = []
    for hd in range(GLA_HEADS):
        vs = slice(hd * GLA_DV, (hd + 1) * GLA_DV)
        oh = o_ref[:, vs]
        rh = r[:, vs]
        gated.append(oh * _rms(oh) * hg_ref[:, vs] * (rh / (1.0 + jnp.exp(-rh))))
    y = _dot(jnp.concatenate(gated, axis=-1).astype(bf16), wo_ref[...])
    x1 = x + y
    x1_ref[...] = x1
    _route(x1, mng_ref[...], wrt_ref[...], br_ref[...], meta_ref, cnt_ref)


def _gla_layer(x, norm_g, w_in, w_gate_up, b_gate, head_g, w_out, moe_norm_g, wrt, br, ts):
    bsz, seq, _ = x.shape
    t = bsz * seq
    wq = w_in[:, 0:GLA_HK].astype(bf16)
    wk = w_in[:, GLA_HK:2 * GLA_HK].astype(bf16)
    wv = w_in[:, 2 * GLA_HK:2 * GLA_HK + GLA_HV].astype(bf16)
    wr = w_in[:, 2 * GLA_HK + GLA_HV:2 * GLA_HK + 2 * GLA_HV].astype(bf16)
    wg = jnp.pad(w_in[:, 2 * GLA_HK + 2 * GLA_HV:], ((0, 0), (0, LANES - GLA_GATE_RANK))).astype(bf16)
    wgu = jnp.pad(w_gate_up, ((0, LANES - GLA_GATE_RANK), (0, 0))).astype(bf16)
    n_s = seq // ts
    args = (x, norm_g.reshape(1, D_MODEL), wq, wk, wv, wr, wg, wgu, b_gate.reshape(1, GLA_HK),
            head_g.reshape(1, GLA_HV), w_out.astype(bf16), moe_norm_g.reshape(1, D_MODEL), wrt, br)
    in_specs = [pl.BlockSpec((1, ts, D_MODEL), lambda b, s: (b, s, 0))]
    in_specs += [_const_spec(a.shape) for a in args[1:]]
    return pl.pallas_call(
        _gla_kernel,
        grid=(bsz, n_s),
        in_specs=in_specs,
        out_specs=_route_out_specs(ts, lambda b, s: b * n_s + s),
        out_shape=_route_out_shapes(t),
        scratch_shapes=[pltpu.VMEM((GLA_HEADS, GLA_DV, GLA_DK), f32),
                        pltpu.VMEM((ts, GLA_HV), f32)],
        compiler_params=pltpu.CompilerParams(
            dimension_semantics=("arbitrary", "arbitrary"), vmem_limit_bytes=VMEM_LIMIT),
        name="gla_mixer",
    )(*args)


def _n_expert_blocks(t):
    tiles = t // SORT_TILE
    worst = t * TOP_K + tiles * N_EXPERTS * (RUN_CHUNK - 1) + N_EXPERTS * (EXPERT_ROWS - 1)
    return -(-worst // EXPERT_ROWS)


def _zero_table_len(t):
    tail = _n_expert_blocks(t) * EXPERT_ROWS - t * TOP_K
    n = N_EXPERTS + -(-tail // ZERO_CHUNK) + 1
    return -(-(n + 1) // LANES) * LANES


def _cumsum(x, axis):
    x = jnp.moveaxis(x, axis, -1)
    n = x.shape[-1]
    upto = jnp.arange(n, dtype=i32)[:, None] <= jnp.arange(n, dtype=i32)[None, :]
    return jnp.moveaxis(jnp.sum(x[..., :, None] * upto.astype(i32), axis=-2), -1, axis)


def _flat_chunks(n_per, max_n):
    cum = _cumsum(n_per, -1)
    c = jnp.arange(max_n, dtype=i32)
    seg = jnp.minimum(jnp.sum((c[:, None] >= cum[..., None, :]).astype(i32), axis=-1), n_per.shape[-1] - 1)
    onehot = (seg[..., None] == jnp.arange(n_per.shape[-1], dtype=i32)).astype(i32)
    return onehot, c - _pick(onehot, cum - n_per), cum[..., -1]


def _pick(onehot, per_segment):
    return jnp.sum(onehot * per_segment[..., None, :], axis=-1)


def _plan(cnt, t):
    tiles = t // SORT_TILE
    cnt = cnt.reshape(tiles, N_EXPERTS).astype(i32)
    nch = (cnt + RUN_CHUNK - 1) // RUN_CHUNK
    run = nch * RUN_CHUNK
    counts = jnp.sum(run, axis=0)
    padded = ((counts + EXPERT_ROWS - 1) // EXPERT_ROWS) * EXPERT_ROWS
    pad_end = _cumsum(padded, 0)
    pad_start = pad_end - padded
    base = pad_start[None, :] + _cumsum(run, 0) - run
    onehot, j, ntot = _flat_chunks(nch, MAX_CHUNKS)
    off = _cumsum(run, 1) - run
    loc = _pick(onehot, off) + j * RUN_CHUNK
    glb = _pick(onehot, base) + j * RUN_CHUNK
    fill = jnp.zeros((tiles, TABLE_LEN - 2 * MAX_CHUNKS - 1), i32)
    table = jnp.concatenate([loc, glb, ntot[:, None], fill], axis=1).reshape(tiles, 1, TABLE_LEN)
    off_col = off.astype(f32).reshape(tiles, N_EXPERTS, 1)

    nb = _n_expert_blocks(t)
    n_blk = padded // EXPERT_ROWS
    n_big = n_blk // BIG_BLOCKS
    n_items = n_big + n_blk % BIG_BLOCKS
    ionehot, k, n_total = _flat_chunks(n_items, _n_work_items(t))
    k_big = _pick(ionehot, n_big)
    item_big = (k < k_big).astype(i32)
    item_start = _pick(ionehot, pad_start // EXPERT_ROWS) + jnp.where(
        k < k_big, BIG_BLOCKS * k, BIG_BLOCKS * k_big + k - k_big)
    icum = _cumsum(n_items, 0)
    item_first = jnp.concatenate([icum - n_items, n_total[None], pad_end[-1:] // EXPERT_ROWS])
    items = (item_first, item_start, item_big)

    region_end = jnp.concatenate([pad_start[1:], jnp.full((1,), nb * EXPERT_ROWS, i32)])
    zlen = region_end - (pad_start + counts)
    nz = (zlen + ZERO_CHUNK - 1) // ZERO_CHUNK
    zl = _zero_table_len(t)
    zonehot, zj, nztot = _flat_chunks(nz, zl - 1)
    zstart = _pick(zonehot, region_end) - (zj + 1) * ZERO_CHUNK
    ztable = jnp.concatenate([zstart, nztot[None]]).reshape(1, 1, zl)
    return table, off_col, items, ztable


def _chunk_copy(stage, out_hbm, loc, glb, sem):
    return pltpu.make_async_copy(stage.at[pl.ds(pl.multiple_of(loc, RUN_CHUNK), RUN_CHUNK), :],
                                 out_hbm.at[pl.ds(pl.multiple_of(glb, RUN_CHUNK), RUN_CHUNK), :], sem)


def _dispatch_kernel(tab_ref, ztab_ref, x_ref, meta_ref, offc_ref, ng_ref,
                     out_hbm, cmeta_ref, stage, zbuf, sems, zsem, prev_n):
    t = pl.program_id(0)
    slot = t % 2
    zl = ztab_ref.shape[2]

    def zero_copy(z):
        dst = pl.multiple_of(ztab_ref[0, 0, z], RUN_CHUNK)
        return pltpu.make_async_copy(zbuf, out_hbm.at[pl.ds(dst, ZERO_CHUNK), :], zsem)

    @pl.when(t == 0)
    def _():
        zbuf[...] = jnp.zeros_like(zbuf)
        nz = ztab_ref[0, 0, zl - 1]
        lax.fori_loop(0, nz, lambda z, c: (zero_copy(z).start(), c)[1], 0)
        lax.fori_loop(0, nz, lambda z, c: (zero_copy(z).wait(), c)[1], 0)

    x = x_ref[...]
    hn = (x * _rms(x) * ng_ref[...]).astype(bf16)
    e1 = meta_ref[0:1, :].astype(i32)
    e2 = meta_ref[1:2, :].astype(i32)
    ids = lax.broadcasted_iota(i32, (N_EXPERTS, SORT_TILE), 0)
    oh0 = ids == e1
    oh1 = ids == e2
    before = (lax.broadcasted_iota(i32, (SORT_TILE, SORT_TILE), 0)
              < lax.broadcasted_iota(i32, (SORT_TILE, SORT_TILE), 1)).astype(bf16)
    c0 = _dot(oh0.astype(bf16), before)
    c1 = _dot(oh1.astype(bf16), before)
    n0 = jnp.sum(oh0.astype(f32), axis=1, keepdims=True)
    offc = offc_ref[0]
    pos0 = jnp.sum(jnp.where(oh0, offc + c0, 0.0), axis=0, keepdims=True)
    pos1 = jnp.sum(jnp.where(oh1, offc + n0 + c1, 0.0), axis=0, keepdims=True)
    zero = jnp.zeros((1, SORT_TILE), f32)
    cmeta_ref[...] = jnp.concatenate(
        [pos0, pos1, meta_ref[2:3, :], meta_ref[3:4, :], zero, zero, zero, zero], axis=0)
    p = lax.broadcasted_iota(i32, (LOCAL_CAP, SORT_TILE), 0).astype(f32)
    perm = jnp.logical_or(p == pos0, p == pos1).astype(bf16)
    stage[slot] = _pack_rows(_dot(perm, hn))

    def wait_chunks(n, s):
        def body(c, carry):
            _chunk_copy(stage.at[s], out_hbm, 0, 0, sems.at[s]).wait()
            return carry
        lax.fori_loop(0, n, body, 0)

    @pl.when(t > 0)
    def _():
        wait_chunks(prev_n[0], 1 - slot)

    ntot = tab_ref[0, 0, 2 * MAX_CHUNKS]

    def issue(c, carry):
        _chunk_copy(stage.at[slot], out_hbm, tab_ref[0, 0, c], tab_ref[0, 0, MAX_CHUNKS + c],
                    sems.at[slot]).start()
        return carry
    lax.fori_loop(0, ntot, issue, 0)
    prev_n[0] = ntot

    @pl.when(t == pl.num_programs(0) - 1)
    def _():
        wait_chunks(ntot, slot)


def _dispatch(x, meta, table, off_col, ztable, norm_g):
    t = x.shape[0]
    tiles = t // SORT_TILE
    zl = ztable.shape[2]
    nb = _n_expert_blocks(t)
    return pl.pallas_call(
        _dispatch_kernel,
        grid=(tiles,),
        in_specs=[
            pl.BlockSpec((1, 1, TABLE_LEN), lambda i: (i, 0, 0), memory_space=pltpu.SMEM),
            pl.BlockSpec((1, 1, zl), lambda i: (0, 0, 0), memory_space=pltpu.SMEM),
            pl.BlockSpec((SORT_TILE, D_MODEL), lambda i: (i, 0)),
            pl.BlockSpec((META_ROWS, SORT_TILE), lambda i: (0, i)),
            pl.BlockSpec((1, N_EXPERTS, 1), lambda i: (i, 0, 0)),
            _const_spec((1, D_MODEL)),
        ],
        out_specs=[pl.BlockSpec(memory_space=pl.ANY),
                   pl.BlockSpec((META_ROWS, SORT_TILE), lambda i: (0, i))],
        out_shape=[jax.ShapeDtypeStruct((nb * EXPERT_ROWS, HALF), u32),
                   jax.ShapeDtypeStruct((META_ROWS, t), f32)],
        scratch_shapes=[pltpu.VMEM((2, LOCAL_CAP, HALF), u32),
                        pltpu.VMEM((ZERO_CHUNK, HALF), u32),
                        pltpu.SemaphoreType.DMA((2,)),
                        pltpu.SemaphoreType.DMA(()),
                        pltpu.SMEM((1,), i32)],
        compiler_params=pltpu.CompilerParams(
            dimension_semantics=("arbitrary",), vmem_limit_bytes=VMEM_LIMIT),
        name="moe_dispatch",
    )(table, ztable, x, meta, off_col, norm_g.reshape(1, D_MODEL))


X_BUFFERS = 3
Y_BUFFERS = 2
BIG_BLOCKS = 4


def _n_work_items(t):
    return _n_expert_blocks(t) // BIG_BLOCKS + (BIG_BLOCKS - 1) * N_EXPERTS + 1


def _expert_kernel(ifirst_ref, istart_ref, ibig_ref, w1_ref, w3_ref, w2_ref, xs_hbm, ys_hbm,
                   xbuf, ybuf, xsem, ysem, w1b, w3b, w2b):
    e = pl.program_id(0)
    lo = ifirst_ref[e]
    hi = ifirst_ref[e + 1]
    total = ifirst_ref[N_EXPERTS]
    nvalid = ifirst_ref[N_EXPERTS + 1]
    n_blocks = ys_hbm.shape[0] // EXPERT_ROWS

    def rows_of(big):
        return (BIG_BLOCKS if big else 1) * EXPERT_ROWS

    def hbm_rows(ref, i, big):
        return ref.at[pl.ds(pl.multiple_of(istart_ref[i] * EXPERT_ROWS, EXPERT_ROWS), rows_of(big)), :]

    def x_copy(i, big):
        slot = i % X_BUFFERS
        return pltpu.make_async_copy(hbm_rows(xs_hbm, i, big),
                                     xbuf.at[slot, pl.ds(0, rows_of(big)), :], xsem.at[slot])

    def y_copy(i, big):
        slot = i % Y_BUFFERS
        return pltpu.make_async_copy(ybuf.at[slot, pl.ds(0, rows_of(big)), :],
                                     hbm_rows(ys_hbm, i, big), ysem.at[slot])

    def by_size(i, fn):
        @pl.when(ibig_ref[i] != 0)
        def _():
            fn(True)

        @pl.when(ibig_ref[i] == 0)
        def _():
            fn(False)

    @pl.when(e == 0)
    def _():
        for k in range(X_BUFFERS - 1):
            @pl.when(k < total)
            def _():
                by_size(k, lambda big: x_copy(k, big).start())

    @pl.when(hi > lo)
    def _():
        w1b[...] = w1_ref[0, 0].astype(bf16)
        w3b[...] = w3_ref[0, 0].astype(bf16)
        w2b[...] = w2_ref[0, 0].astype(bf16)

    def item(i, carry):
        ahead = i + X_BUFFERS - 1

        @pl.when(ahead < total)
        def _():
            by_size(ahead, lambda big: x_copy(ahead, big).start())

        def run(big):
            rows = rows_of(big)
            x_copy(i, big).wait()
            hn = _unpack_rows(xbuf[i % X_BUFFERS, pl.ds(0, rows), :])
            h1 = _dot(hn, w1b[...])
            h3 = _dot(hn, w3b[...])
            hid = ((h1 / (1.0 + jnp.exp(-h1))) * h3).astype(bf16)
            y = _dot(hid, w2b[...])

            @pl.when(i >= Y_BUFFERS)
            def _():
                by_size(i - Y_BUFFERS, lambda b: y_copy(i - Y_BUFFERS, b).wait())

            ybuf[i % Y_BUFFERS, pl.ds(0, rows), :] = _pack_rows(y.astype(bf16).astype(f32))
            y_copy(i, big).start()

        by_size(i, run)
        return carry

    lax.fori_loop(lo, hi, item, 0)

    @pl.when(e == pl.num_programs(0) - 1)
    def _():
        for k in range(1, Y_BUFFERS + 1):
            @pl.when(total >= k)
            def _():
                by_size(total - k, lambda big: y_copy(total - k, big).wait())
        ybuf[0, pl.ds(0, EXPERT_ROWS), :] = jnp.zeros((EXPERT_ROWS, HALF), u32)

        def tail(g):
            dst = pl.multiple_of(g * EXPERT_ROWS, EXPERT_ROWS)
            return pltpu.make_async_copy(ybuf.at[0, pl.ds(0, EXPERT_ROWS), :],
                                         ys_hbm.at[pl.ds(dst, EXPERT_ROWS), :], ysem.at[0])
        lax.fori_loop(nvalid, n_blocks, lambda g, c: (tail(g).start(), c)[1], 0)
        lax.fori_loop(nvalid, n_blocks, lambda g, c: (tail(g).wait(), c)[1], 0)


def _experts(xs, item_first, item_start, item_big, w1, w3, w2, layer):
    w_index = lambda e, *_: (layer, e, 0, 0)
    big_rows = BIG_BLOCKS * EXPERT_ROWS
    grid_spec = pltpu.PrefetchScalarGridSpec(
        num_scalar_prefetch=3,
        grid=(N_EXPERTS,),
        in_specs=[
            pl.BlockSpec((1, 1, D_MODEL, EXPERT_FF), w_index),
            pl.BlockSpec((1, 1, D_MODEL, EXPERT_FF), w_index),
            pl.BlockSpec((1, 1, EXPERT_FF, D_MODEL), w_index),
            pl.BlockSpec(memory_space=pl.ANY),
        ],
        out_specs=pl.BlockSpec(memory_space=pl.ANY),
        scratch_shapes=[pltpu.VMEM((X_BUFFERS, big_rows, HALF), u32),
                        pltpu.VMEM((Y_BUFFERS, big_rows, HALF), u32),
                        pltpu.SemaphoreType.DMA((X_BUFFERS,)),
                        pltpu.SemaphoreType.DMA((Y_BUFFERS,)),
                        pltpu.VMEM((D_MODEL, EXPERT_FF), bf16),
                        pltpu.VMEM((D_MODEL, EXPERT_FF), bf16),
                        pltpu.VMEM((EXPERT_FF, D_MODEL), bf16)],
    )
    return pl.pallas_call(
        _expert_kernel,
        grid_spec=grid_spec,
        out_shape=jax.ShapeDtypeStruct(xs.shape, u32),
        compiler_params=pltpu.CompilerParams(
            dimension_semantics=("arbitrary",), vmem_limit_bytes=VMEM_LIMIT),
        name="moe_experts",
    )(item_first, item_start, item_big, w1, w3, w2, xs)


def _combine(x_ref, tab_ref, tab_next_ref, cmeta_ref, ys_hbm, ybuf, sems):
    i = pl.program_id(0)
    n_steps = pl.num_programs(0)
    tiles = x_ref.shape[0] // SORT_TILE
    slot = i % 2

    def copy(s, u, loc, glb):
        dst = pl.multiple_of(u * LOCAL_CAP + loc, RUN_CHUNK)
        return pltpu.make_async_copy(ys_hbm.at[pl.ds(pl.multiple_of(glb, RUN_CHUNK), RUN_CHUNK), :],
                                     ybuf.at[s, pl.ds(dst, RUN_CHUNK), :], sems.at[s])

    def fetch(tab, s):
        for u in range(tiles):
            n = tab[u, 0, 2 * MAX_CHUNKS]

            def body(c, carry):
                copy(s, u, tab[u, 0, c], tab[u, 0, MAX_CHUNKS + c]).start()
                return carry
            lax.fori_loop(0, n, body, 0)

            def clear(c, carry):
                dst = pl.multiple_of(u * LOCAL_CAP + c * RUN_CHUNK, RUN_CHUNK)
                ybuf[s, pl.ds(dst, RUN_CHUNK), :] = jnp.zeros((RUN_CHUNK, HALF), u32)
                return carry
            lax.fori_loop(n, MAX_CHUNKS, clear, 0)

    @pl.when(i == 0)
    def _():
        fetch(tab_ref, 0)

    @pl.when(i + 1 < n_steps)
    def _():
        fetch(tab_next_ref, 1 - slot)

    for u in range(tiles):
        def body(c, carry):
            copy(slot, u, 0, 0).wait()
            return carry
        lax.fori_loop(0, tab_ref[u, 0, 2 * MAX_CHUNKS], body, 0)

    p = lax.broadcasted_iota(i32, (LOCAL_CAP, SORT_TILE), 0).astype(f32)
    parts = []
    for u in range(tiles):
        cs = slice(u * SORT_TILE, (u + 1) * SORT_TILE)
        pos0, pos1 = cmeta_ref[0:1, cs], cmeta_ref[1:2, cs]
        g = (jnp.where(p == pos0, cmeta_ref[2:3, cs], 0.0)
             + jnp.where(p == pos1, cmeta_ref[3:4, cs], 0.0))
        g_hi = g.astype(bf16)
        g_lo = (g - g_hi.astype(f32)).astype(bf16)
        y = _unpack_rows(ybuf[slot, pl.ds(u * LOCAL_CAP, LOCAL_CAP), :])
        both = _dot_tn(jnp.concatenate([g_hi, g_lo], axis=1), y)
        parts.append(x_ref[cs, :] + both[:SORT_TILE] + both[SORT_TILE:])
    return parts[0] if tiles == 1 else jnp.concatenate(parts, axis=0)


def _combine_specs(ts, n_steps):
    tiles = ts // SORT_TILE
    return [
        pl.BlockSpec((ts, D_MODEL), lambda i: (i, 0)),
        pl.BlockSpec((tiles, 1, TABLE_LEN), lambda i: (i, 0, 0), memory_space=pltpu.SMEM),
        pl.BlockSpec((tiles, 1, TABLE_LEN), lambda i: (jnp.minimum(i + 1, n_steps - 1), 0, 0),
                     memory_space=pltpu.SMEM),
        pl.BlockSpec((META_ROWS, ts), lambda i: (0, i)),
        pl.BlockSpec(memory_space=pl.ANY),
    ]


def _combine_scratch(ts):
    tiles = ts // SORT_TILE
    return [pltpu.VMEM((2, tiles * LOCAL_CAP, HALF), u32), pltpu.SemaphoreType.DMA((2,))]


def _gelu(z):
    return 0.5 * z * (1.0 + lax.erf(z * (2.0 ** -0.5)))


def _sgu_kernel(x_ref, ng_ref, wu_ref, wv_ref, lng_ref, lnb_ref,
                ws_ref, bs_ref, wo_ref, mng_ref, wrt_ref, br_ref,
                x3_ref, meta_ref, cnt_ref, vn_ref):
    ts = x_ref.shape[0]
    x2 = x_ref[...]
    h = (x2 * _rms(x2) * ng_ref[...]).astype(bf16)
    v = _gelu(_dot(h, wv_ref[...]))
    mu = jnp.mean(v, axis=-1, keepdims=True)
    vc = v - mu
    rstd = lax.rsqrt(jnp.mean(vc * vc, axis=-1, keepdims=True) + EPS)
    vn_ref[...] = (vc * rstd * lng_ref[...] + lnb_ref[...]).astype(bf16)

    pos = lax.broadcasted_iota(i32, (SGU_BLOCK, SGU_BLOCK), 0) // CHUNK
    src = lax.broadcasted_iota(i32, (SGU_BLOCK, SGU_BLOCK), 1) // CHUNK
    acc = x2
    for g in range(SGU_GROUPS):
        cs = slice(g * SGU_GC, (g + 1) * SGU_GC)
        ws = jnp.where(pos >= src, ws_ref[g], jnp.zeros((), bf16))
        u = _gelu(_dot(h, wu_ref[:, cs]))
        mixed = [_dot(ws, vn_ref[nb * SGU_BLOCK:(nb + 1) * SGU_BLOCK, cs]) + bs_ref[:, g:g + 1]
                 for nb in range(ts // SGU_BLOCK)]
        out = (u * jnp.concatenate(mixed, axis=0)).astype(bf16)
        acc = acc + _dot(out, wo_ref[cs, :])
    x3_ref[...] = acc
    _route(acc, mng_ref[...], wrt_ref[...], br_ref[...], meta_ref, cnt_ref)


def _sgu_layer(x2, norm_g, w_in, ln_g, ln_b, w_s, b_s, w_out, moe_norm_g, wrt, br, ts):
    t = x2.shape[0]
    n_steps = t // ts
    args = (norm_g.reshape(1, D_MODEL), w_in[:, :SGU_HALF].astype(bf16), w_in[:, SGU_HALF:].astype(bf16),
            ln_g.reshape(1, SGU_HALF), ln_b.reshape(1, SGU_HALF), w_s.astype(bf16), b_s.T,
            w_out.astype(bf16), moe_norm_g.reshape(1, D_MODEL), wrt, br)
    return pl.pallas_call(
        _sgu_kernel,
        grid=(n_steps,),
        in_specs=[pl.BlockSpec((ts, D_MODEL), lambda i: (i, 0))] + [_const_spec(a.shape) for a in args],
        out_specs=_route_out_specs(ts, lambda i: i),
        out_shape=_route_out_shapes(t),
        scratch_shapes=[pltpu.VMEM((ts, SGU_HALF), bf16)],
        compiler_params=pltpu.CompilerParams(
            dimension_semantics=("arbitrary",), vmem_limit_bytes=VMEM_LIMIT),
        name="sgu_mixer",
    )(x2, *args)


def _combine_kernel(x_ref, tab_ref, tab_next_ref, cmeta_ref, ys_hbm, out_ref, ybuf, sems):
    out_ref[...] = _combine(x_ref, tab_ref, tab_next_ref, cmeta_ref, ys_hbm, ybuf, sems)


def _final_kernel(x_ref, tab_ref, tab_next_ref, cmeta_ref, ys_hbm, ng_ref, out_ref, ybuf, sems):
    x = _combine(x_ref, tab_ref, tab_next_ref, cmeta_ref, ys_hbm, ybuf, sems)
    out_ref[...] = x * _rms(x) * ng_ref[...]


def _combine_layer(x, table, cmeta, ys, norm_g, ts):
    t = x.shape[0]
    n_steps = t // ts
    final = norm_g is not None
    extra = (norm_g.reshape(1, D_MODEL),) if final else ()
    return pl.pallas_call(
        _final_kernel if final else _combine_kernel,
        grid=(n_steps,),
        in_specs=_combine_specs(ts, n_steps) + [_const_spec(a.shape) for a in extra],
        out_specs=pl.BlockSpec((ts, D_MODEL), lambda i: (i, 0)),
        out_shape=jax.ShapeDtypeStruct((t, D_MODEL), f32),
        scratch_shapes=_combine_scratch(ts),
        compiler_params=pltpu.CompilerParams(
            dimension_semantics=("arbitrary",), vmem_limit_bytes=VMEM_LIMIT),
        name="final_norm" if final else "moe_combine",
    )(x, table, table, cmeta, ys, *extra)


def _moe(x, meta, cnt, norm_g, w1, w3, w2, layer):
    table, off_col, items, ztable = _plan(cnt, x.shape[0])
    sorted_x, cmeta = _dispatch(x, meta, table, off_col, ztable, norm_g)
    return _experts(sorted_x, *items, w1, w3, w2, layer), table, cmeta


def _forward(x, gla_norm, gla_w_in, gla_w_gate_up, gla_b_gate, gla_head_g, gla_w_out, sgu_norm, sgu_w_in,
             sgu_ln_g, sgu_ln_b, sgu_w_s, sgu_b_s, sgu_w_out, moe_norm, moe_w_group, moe_b_group,
             moe_w_sub, moe_b_sub, moe_w1, moe_w3, moe_w2, final_norm, *, ts_gla, ts_sgu, ts_fin):
    wrt0, br0 = _router_params(moe_w_group[0], moe_b_group[0], moe_w_sub[0], moe_b_sub[0])
    wrt1, br1 = _router_params(moe_w_group[1], moe_b_group[1], moe_w_sub[1], moe_b_sub[1])
    x1, meta0, cnt0 = _gla_layer(x, gla_norm[0], gla_w_in[0], gla_w_gate_up[0], gla_b_gate[0], gla_head_g[0],
                                 gla_w_out[0], moe_norm[0], wrt0, br0, ts_gla)
    ys0, table0, cmeta0 = _moe(x1, meta0, cnt0, moe_norm[0], moe_w1, moe_w3, moe_w2, 0)
    x2 = _combine_layer(x1, table0, cmeta0, ys0, None, ts_fin)
    x3, meta1, cnt1 = _sgu_layer(x2, sgu_norm[0], sgu_w_in[0], sgu_ln_g[0], sgu_ln_b[0],
                                 sgu_w_s[0], sgu_b_s[0], sgu_w_out[0], moe_norm[1], wrt1, br1, ts_sgu)
    ys1, table1, cmeta1 = _moe(x3, meta1, cnt1, moe_norm[1], moe_w1, moe_w3, moe_w2, 1)
    out = _combine_layer(x3, table1, cmeta1, ys1, final_norm, ts_fin)
    return out.reshape(x.shape)


def kernel(x, gla_norm, gla_w_in, gla_w_gate_up, gla_b_gate, gla_head_g, gla_w_out, sgu_norm, sgu_w_in, sgu_ln_g, sgu_ln_b, sgu_w_s, sgu_b_s, sgu_w_out, moe_norm, moe_w_group, moe_b_group, moe_w_sub, moe_b_sub, moe_w1, moe_w3, moe_w2, final_norm):
    return _forward(x, gla_norm, gla_w_in, gla_w_gate_up, gla_b_gate, gla_head_g, gla_w_out, sgu_norm,
                    sgu_w_in, sgu_ln_g, sgu_ln_b, sgu_w_s, sgu_b_s, sgu_w_out, moe_norm, moe_w_group,
                    moe_b_group, moe_w_sub, moe_b_sub, moe_w1, moe_w3, moe_w2, final_norm,
                    ts_gla=1024, ts_sgu=1024, ts_fin=512)
```

```python
import jax
import jax.numpy as jnp
from jax import lax
from jax.experimental import pallas as pl
from jax.experimental.pallas import tpu as pltpu

D_MODEL = 1024
HALF = D_MODEL // 2
EPS = 1e-6
LANES = 128
SUBLANES = 8

CHUNK = 64
GLA_HEADS = 4
GLA_DK = 128
GLA_DV = 256
GLA_HK = GLA_HEADS * GLA_DK
GLA_HV = GLA_HEADS * GLA_DV
GLA_GATE_RANK = 16
GLA_TAU = 16.0

SGU_BLOCK = 128
SGU_GROUPS = 4
SGU_HALF = 2048
SGU_GC = SGU_HALF // SGU_GROUPS

N_GROUPS = 4
EXPERTS_PER_GROUP = 8
N_EXPERTS = N_GROUPS * EXPERTS_PER_GROUP
TOP_K = 2
EXPERT_FF = 512
ROUTE_ROWS = 64
META_ROWS = 8

SORT_TILE = 256
RUN_CHUNK = SUBLANES
LOCAL_CAP = 768
MAX_CHUNKS = LOCAL_CAP // RUN_CHUNK
EXPERT_ROWS = 256
ZERO_CHUNK = EXPERT_ROWS
TABLE_LEN = 384
TAB_LOC, TAB_GLB, TAB_BIG = 0, MAX_CHUNKS, 2 * MAX_CHUNKS
TAB_COPIES, TAB_UNITS = 3 * MAX_CHUNKS, 3 * MAX_CHUNKS + 1
assert LOCAL_CAP >= TOP_K * SORT_TILE + N_EXPERTS * (RUN_CHUNK - 1)
assert TAB_UNITS < TABLE_LEN

VMEM_LIMIT = 56 * 1024 * 1024

f32 = jnp.float32
bf16 = jnp.bfloat16
i32 = jnp.int32
u32 = jnp.uint32


def _dot(a, b):
    return jnp.dot(a, b, preferred_element_type=f32)


def _dot_tn(a, b):
    return lax.dot_general(a, b, (((0,), (0,)), ((), ())), preferred_element_type=f32)


def _dot_nt(a, b):
    return lax.dot_general(a, b, (((1,), (1,)), ((), ())), preferred_element_type=f32)


def _rms(x):
    return lax.rsqrt(jnp.mean(x * x, axis=-1, keepdims=True) + EPS)


def _pack_rows(v):
    lo = lax.bitcast_convert_type(v[:, :HALF], u32)
    hi = lax.bitcast_convert_type(v[:, HALF:], u32)
    return lax.shift_right_logical(lo, jnp.uint32(16)) | (hi & jnp.uint32(0xFFFF0000))


def _unpack_rows(w):
    lo = lax.bitcast_convert_type(lax.shift_left(w, jnp.uint32(16)), f32)
    hi = lax.bitcast_convert_type(w & jnp.uint32(0xFFFF0000), f32)
    return jnp.concatenate([lo, hi], axis=-1).astype(bf16)


def _const_spec(shape):
    return pl.BlockSpec(shape, lambda *_: (0,) * len(shape))


def _route(x1, norm_g, wr_t, br, meta_ref, cnt_ref):
    n = x1.shape[0]
    hn = (x1 * _rms(x1) * norm_g).astype(bf16)
    lt = _dot_nt(wr_t, hn) + br
    rows = lax.broadcasted_iota(i32, (SUBLANES, n), 0)
    neg = jnp.float32(-jnp.inf)
    lg = jnp.where(rows < N_GROUPS, lt[0:SUBLANES], neg)
    gmax = jnp.max(lg, axis=0, keepdims=True)
    gidx = jnp.min(jnp.where(lg == gmax, rows, SUBLANES), axis=0, keepdims=True)
    g_w = 1.0 / jnp.sum(jnp.exp(lg - gmax), axis=0, keepdims=True)
    chosen = jnp.zeros((SUBLANES, n), f32)
    for g in range(N_GROUPS):
        chosen = jnp.where(gidx == g, lt[SUBLANES * (g + 1):SUBLANES * (g + 2)], chosen)
    m1 = jnp.max(chosen, axis=0, keepdims=True)
    i1 = jnp.min(jnp.where(chosen == m1, rows, SUBLANES), axis=0, keepdims=True)
    rest = jnp.where(rows == i1, neg, chosen)
    m2 = jnp.max(rest, axis=0, keepdims=True)
    i2 = jnp.min(jnp.where(rest == m2, rows, SUBLANES), axis=0, keepdims=True)
    t = jnp.exp(m2 - m1)
    s1 = 1.0 / (1.0 + t)
    s2 = t / (1.0 + t)
    e1 = gidx * EXPERTS_PER_GROUP + i1
    e2 = gidx * EXPERTS_PER_GROUP + i2
    zero = jnp.zeros((1, n), f32)
    meta_ref[...] = jnp.concatenate(
        [e1.astype(f32), e2.astype(f32), g_w * s1, g_w * s2, zero, zero, zero, zero], axis=0)
    ids = lax.broadcasted_iota(i32, (N_EXPERTS, n), 0)
    hits = (ids == e1).astype(f32) + (ids == e2).astype(f32)
    for u in range(n // SORT_TILE):
        cnt_ref[u * N_EXPERTS:(u + 1) * N_EXPERTS, :] = jnp.sum(
            hits[:, u * SORT_TILE:(u + 1) * SORT_TILE], axis=1, keepdims=True)


def _router_params(w_group, b_group, w_sub, b_sub):
    tail = ROUTE_ROWS - SUBLANES - N_EXPERTS
    wrt = jnp.concatenate([
        w_group.T, jnp.zeros((SUBLANES - N_GROUPS, D_MODEL), f32),
        jnp.transpose(w_sub, (0, 2, 1)).reshape(N_EXPERTS, D_MODEL),
        jnp.zeros((tail, D_MODEL), f32)], axis=0)
    br = jnp.concatenate([b_group, jnp.zeros((SUBLANES - N_GROUPS,), f32), b_sub.reshape(N_EXPERTS),
                          jnp.zeros((tail,), f32)]).reshape(ROUTE_ROWS, 1)
    return wrt.astype(bf16), br


def _route_out_specs(ts, index):
    tiles = ts // SORT_TILE
    return [pl.BlockSpec((ts, D_MODEL), lambda *g: (index(*g), 0)),
            pl.BlockSpec((META_ROWS, ts), lambda *g: (0, index(*g))),
            pl.BlockSpec((tiles * N_EXPERTS, 1), lambda *g: (index(*g), 0))]


def _route_out_shapes(t):
    return [jax.ShapeDtypeStruct((t, D_MODEL), f32),
            jax.ShapeDtypeStruct((META_ROWS, t), f32),
            jax.ShapeDtypeStruct((t // SORT_TILE * N_EXPERTS, 1), f32)]


def _gla_kernel(x_ref, ng_ref, wq_ref, wk_ref, wv_ref, wr_ref, wg_ref, wgu_ref, bg_ref,
                hg_ref, wo_ref, mng_ref, wrt_ref, br_ref,
                x1_ref, meta_ref, cnt_ref, st_ref, o_ref):
    ts = x_ref.shape[1]

    @pl.when(pl.program_id(1) == 0)
    def _():
        st_ref[...] = jnp.zeros_like(st_ref)

    x = x_ref[0]
    h = (x * _rms(x) * ng_ref[...]).astype(bf16)
    q = _dot(h, wq_ref[...]) * (GLA_DK ** -0.5)
    k = _dot(h, wk_ref[...])
    v = _dot(h, wv_ref[...]).astype(bf16)
    glr = _dot(h, wg_ref[...]).astype(bf16)
    gp = _dot(glr, wgu_ref[...]) + bg_ref[...]
    log_a = (jnp.minimum(gp, 0.0) - jnp.log(1.0 + jnp.exp(-jnp.abs(gp)))) * (1.0 / GLA_TAU)

    row = lax.broadcasted_iota(i32, (CHUNK, GLA_HK), 0)
    for c in range(ts // CHUNK):
        rs = slice(c * CHUNK, (c + 1) * CHUNK)
        b = log_a[rs]
        sh = 1
        while sh < CHUNK:
            b = b + jnp.where(row >= sh, pltpu.roll(b, sh, axis=0), 0.0)
            sh *= 2
        b_end = b[CHUNK - 1:CHUNK]
        kdec = (k[rs] * jnp.exp(b_end - b)).astype(bf16)
        decay = jnp.exp(b_end)
        qc = q[rs].astype(bf16)
        vc = v[rs]
        for hd in range(GLA_HEADS):
            ks = slice(hd * GLA_DK, (hd + 1) * GLA_DK)
            vs = slice(hd * GLA_DV, (hd + 1) * GLA_DV)
            st = st_ref[hd] * decay[:, ks] + _dot_tn(vc[:, vs], kdec[:, ks])
            st_ref[hd] = st
            o_ref[rs, vs] = _dot_nt(qc[:, ks], st.astype(bf16))

    r = _dot(h, wr_ref[...])
    gated = []
    for hd in range(GLA_HEADS):
        vs = slice(hd * GLA_DV, (hd + 1) * GLA_DV)
        oh = o_ref[:, vs]
        rh = r[:, vs]
        gated.append(oh * _rms(oh) * hg_ref[:, vs] * (rh / (1.0 + jnp.exp(-rh))))
    y = _dot(jnp.concatenate(gated, axis=-1).astype(bf16), wo_ref[...])
    x1 = x + y
    x1_ref[...] = x1
    _route(x1, mng_ref[...], wrt_ref[...], br_ref[...], meta_ref, cnt_ref)


def _gla_layer(x, norm_g, w_in, w_gate_up, b_gate, head_g, w_out, moe_norm_g, wrt, br, ts):
    bsz, seq, _ = x.shape
    t = bsz * seq
    wq = w_in[:, 0:GLA_HK].astype(bf16)
    wk = w_in[:, GLA_HK:2 * GLA_HK].astype(bf16)
    wv = w_in[:, 2 * GLA_HK:2 * GLA_HK + GLA_HV].astype(bf16)
    wr = w_in[:, 2 * GLA_HK + GLA_HV:2 * GLA_HK + 2 * GLA_HV].astype(bf16)
    wg = jnp.pad(w_in[:, 2 * GLA_HK + 2 * GLA_HV:], ((0, 0), (0, LANES - GLA_GATE_RANK))).astype(bf16)
    wgu = jnp.pad(w_gate_up, ((0, LANES - GLA_GATE_RANK), (0, 0))).astype(bf16)
    n_s = seq // ts
    args = (x, norm_g.reshape(1, D_MODEL), wq, wk, wv, wr, wg, wgu, b_gate.reshape(1, GLA_HK),
            head_g.reshape(1, GLA_HV), w_out.astype(bf16), moe_norm_g.reshape(1, D_MODEL), wrt, br)
    in_specs = [pl.BlockSpec((1, ts, D_MODEL), lambda b, s: (b, s, 0))]
    in_specs += [_const_spec(a.shape) for a in args[1:]]
    return pl.pallas_call(
        _gla_kernel,
        grid=(bsz, n_s),
        in_specs=in_specs,
        out_specs=_route_out_specs(ts, lambda b, s: b * n_s + s),
        out_shape=_route_out_shapes(t),
        scratch_shapes=[pltpu.VMEM((GLA_HEADS, GLA_DV, GLA_DK), f32),
                        pltpu.VMEM((ts, GLA_HV), f32)],
        compiler_params=pltpu.CompilerParams(
            dimension_semantics=("arbitrary", "arbitrary"), vmem_limit_bytes=VMEM_LIMIT),
        name="gla_mixer",
    )(*args)


def _n_expert_blocks(t):
    tiles = t // SORT_TILE
    worst = t * TOP_K + tiles * N_EXPERTS * (RUN_CHUNK - 1) + N_EXPERTS * (EXPERT_ROWS - 1)
    return -(-worst // EXPERT_ROWS)


def _zero_table_len(t):
    tail = _n_expert_blocks(t) * EXPERT_ROWS - t * TOP_K
    n = N_EXPERTS + -(-tail // ZERO_CHUNK) + 1
    return -(-(n + 1) // LANES) * LANES


def _cumsum(x, axis):
    x = jnp.moveaxis(x, axis, -1)
    n = x.shape[-1]
    upto = jnp.arange(n, dtype=i32)[:, None] <= jnp.arange(n, dtype=i32)[None, :]
    return jnp.moveaxis(jnp.sum(x[..., :, None] * upto.astype(i32), axis=-2), -1, axis)


def _flat_chunks(n_per, max_n):
    cum = _cumsum(n_per, -1)
    c = jnp.arange(max_n, dtype=i32)
    seg = jnp.minimum(jnp.sum((c[:, None] >= cum[..., None, :]).astype(i32), axis=-1), n_per.shape[-1] - 1)
    onehot = (seg[..., None] == jnp.arange(n_per.shape[-1], dtype=i32)).astype(i32)
    return onehot, c - _pick(onehot, cum - n_per), cum[..., -1]


def _pick(onehot, per_segment):
    return jnp.sum(onehot * per_segment[..., None, :], axis=-1)


def _plan(cnt, t):
    tiles = t // SORT_TILE
    cnt = cnt.reshape(tiles, N_EXPERTS).astype(i32)
    nch = (cnt + RUN_CHUNK - 1) // RUN_CHUNK
    run = nch * RUN_CHUNK
    counts = jnp.sum(run, axis=0)
    padded = ((counts + EXPERT_ROWS - 1) // EXPERT_ROWS) * EXPERT_ROWS
    pad_end = _cumsum(padded, 0)
    pad_start = pad_end - padded
    base = pad_start[None, :] + _cumsum(run, 0) - run
    n_pair = nch // 2
    onehot, j, n_copies = _flat_chunks(n_pair + nch % 2, MAX_CHUNKS)
    off = _cumsum(run, 1) - run
    loc = _pick(onehot, off) + j * (2 * RUN_CHUNK)
    glb = _pick(onehot, base) + j * (2 * RUN_CHUNK)
    big = (j < _pick(onehot, n_pair)).astype(i32)
    units = jnp.sum(nch, axis=1)
    fill = jnp.zeros((tiles, TABLE_LEN - TAB_UNITS - 1), i32)
    table = jnp.concatenate([loc, glb, big, n_copies[:, None], units[:, None], fill],
                            axis=1).reshape(tiles, 1, TABLE_LEN)
    off_col = off.astype(f32).reshape(tiles, N_EXPERTS, 1)

    nb = _n_expert_blocks(t)
    n_blk = padded // EXPERT_ROWS
    n_big = n_blk // BIG_BLOCKS
    n_items = n_big + n_blk % BIG_BLOCKS
    ionehot, k, n_total = _flat_chunks(n_items, _n_work_items(t))
    k_big = _pick(ionehot, n_big)
    item_big = (k < k_big).astype(i32)
    item_start = _pick(ionehot, pad_start // EXPERT_ROWS) + jnp.where(
        k < k_big, BIG_BLOCKS * k, BIG_BLOCKS * k_big + k - k_big)
    icum = _cumsum(n_items, 0)
    item_first = jnp.concatenate([icum - n_items, n_total[None], pad_end[-1:] // EXPERT_ROWS])
    items = (item_first, item_start, item_big)

    region_end = jnp.concatenate([pad_start[1:], jnp.full((1,), nb * EXPERT_ROWS, i32)])
    zlen = region_end - (pad_start + counts)
    nz = (zlen + ZERO_CHUNK - 1) // ZERO_CHUNK
    zl = _zero_table_len(t)
    zonehot, zj, nztot = _flat_chunks(nz, zl - 1)
    zstart = _pick(zonehot, region_end) - (zj + 1) * ZERO_CHUNK
    ztable = jnp.concatenate([zstart, nztot[None]]).reshape(1, 1, zl)
    return table, off_col, items, ztable


def _start_runs(tab, u, make):
    n = tab[u, 0, TAB_COPIES]

    def one(c, priority):
        loc, glb = tab[u, 0, TAB_LOC + c], tab[u, 0, TAB_GLB + c]

        @pl.when(tab[u, 0, TAB_BIG + c] != 0)
        def _():
            make(2 * RUN_CHUNK, loc, glb).start(priority=priority)

        @pl.when(tab[u, 0, TAB_BIG + c] == 0)
        def _():
            make(RUN_CHUNK, loc, glb).start(priority=priority)

    def pair(k, carry):
        one(2 * k, 0)

        @pl.when(2 * k + 1 < n)
        def _():
            one(2 * k + 1, 1)
        return carry
    lax.fori_loop(0, (n + 1) // 2, pair, 0)


def _wait_runs(units, make):
    lax.fori_loop(0, units // 2, lambda c, carry: (make(2 * RUN_CHUNK, 0, 0).wait(), carry)[1], 0)

    @pl.when(units % 2 == 1)
    def _():
        make(RUN_CHUNK, 0, 0).wait()


def _dispatch_kernel(tab_ref, ztab_ref, x_ref, meta_ref, offc_ref, ng_ref,
                     out_hbm, cmeta_ref, stage, zbuf, sems, zsem, prev_n):
    t = pl.program_id(0)
    slot = t % 2
    zl = ztab_ref.shape[2]

    def zero_copy(z):
        dst = pl.multiple_of(ztab_ref[0, 0, z], RUN_CHUNK)
        return pltpu.make_async_copy(zbuf, out_hbm.at[pl.ds(dst, ZERO_CHUNK), :], zsem)

    @pl.when(t == 0)
    def _():
        zbuf[...] = jnp.zeros_like(zbuf)
        nz = ztab_ref[0, 0, zl - 1]
        lax.fori_loop(0, nz, lambda z, c: (zero_copy(z).start(), c)[1], 0)
        lax.fori_loop(0, nz, lambda z, c: (zero_copy(z).wait(), c)[1], 0)

    x = x_ref[...]
    hn = (x * _rms(x) * ng_ref[...]).astype(bf16)
    e1 = meta_ref[0:1, :].astype(i32)
    e2 = meta_ref[1:2, :].astype(i32)
    ids = lax.broadcasted_iota(i32, (N_EXPERTS, SORT_TILE), 0)
    oh0 = ids == e1
    oh1 = ids == e2
    before = (lax.broadcasted_iota(i32, (SORT_TILE, SORT_TILE), 0)
              < lax.broadcasted_iota(i32, (SORT_TILE, SORT_TILE), 1)).astype(bf16)
    c0 = _dot(oh0.astype(bf16), before)
    c1 = _dot(oh1.astype(bf16), before)
    n0 = jnp.sum(oh0.astype(f32), axis=1, keepdims=True)
    offc = offc_ref[0]
    pos0 = jnp.sum(jnp.where(oh0, offc + c0, 0.0), axis=0, keepdims=True)
    pos1 = jnp.sum(jnp.where(oh1, offc + n0 + c1, 0.0), axis=0, keepdims=True)
    zero = jnp.zeros((1, SORT_TILE), f32)
    cmeta_ref[...] = jnp.concatenate(
        [pos0, pos1, meta_ref[2:3, :], meta_ref[3:4, :], zero, zero, zero, zero], axis=0)
    p = lax.broadcasted_iota(i32, (LOCAL_CAP, SORT_TILE), 0).astype(f32)
    perm = jnp.logical_or(p == pos0, p == pos1).astype(bf16)
    stage[slot] = _pack_rows(_dot(perm, hn))

    def copy_from(s):
        def make(rows, loc, glb):
            return pltpu.make_async_copy(stage.at[s, pl.ds(pl.multiple_of(loc, RUN_CHUNK), rows), :],
                                         out_hbm.at[pl.ds(pl.multiple_of(glb, RUN_CHUNK), rows), :], sems.at[s])
        return make

    @pl.when(t > 0)
    def _():
        _wait_runs(prev_n[0], copy_from(1 - slot))

    _start_runs(tab_ref, 0, copy_from(slot))
    units = tab_ref[0, 0, TAB_UNITS]
    prev_n[0] = units

    @pl.when(t == pl.num_programs(0) - 1)
    def _():
        _wait_runs(units, copy_from(slot))


def _dispatch(x, meta, table, off_col, ztable, norm_g):
    t = x.shape[0]
    tiles = t // SORT_TILE
    zl = ztable.shape[2]
    nb = _n_expert_blocks(t)
    return pl.pallas_call(
        _dispatch_kernel,
        grid=(tiles,),
        in_specs=[
            pl.BlockSpec((1, 1, TABLE_LEN), lambda i: (i, 0, 0), memory_space=pltpu.SMEM),
            pl.BlockSpec((1, 1, zl), lambda i: (0, 0, 0), memory_space=pltpu.SMEM),
            pl.BlockSpec((SORT_TILE, D_MODEL), lambda i: (i, 0)),
            pl.BlockSpec((META_ROWS, SORT_TILE), lambda i: (0, i)),
            pl.BlockSpec((1, N_EXPERTS, 1), lambda i: (i, 0, 0)),
            _const_spec((1, D_MODEL)),
        ],
        out_specs=[pl.BlockSpec(memory_space=pl.ANY),
                   pl.BlockSpec((META_ROWS, SORT_TILE), lambda i: (0, i))],
        out_shape=[jax.ShapeDtypeStruct((nb * EXPERT_ROWS, HALF), u32),
                   jax.ShapeDtypeStruct((META_ROWS, t), f32)],
        scratch_shapes=[pltpu.VMEM((2, LOCAL_CAP, HALF), u32),
                        pltpu.VMEM((ZERO_CHUNK, HALF), u32),
                        pltpu.SemaphoreType.DMA((2,)),
                        pltpu.SemaphoreType.DMA(()),
                        pltpu.SMEM((1,), i32)],
        compiler_params=pltpu.CompilerParams(
            dimension_semantics=("arbitrary",), vmem_limit_bytes=VMEM_LIMIT),
        name="moe_dispatch",
    )(table, ztable, x, meta, off_col, norm_g.reshape(1, D_MODEL))


X_BUFFERS = 3
Y_BUFFERS = 2
BIG_BLOCKS = 4


def _n_work_items(t):
    return _n_expert_blocks(t) // BIG_BLOCKS + (BIG_BLOCKS - 1) * N_EXPERTS + 1


def _expert_kernel(ifirst_ref, istart_ref, ibig_ref, w1_ref, w3_ref, w2_ref, xs_hbm, ys_hbm,
                   xbuf, ybuf, xsem, ysem, w1b, w3b, w2b):
    e = pl.program_id(0)
    lo = ifirst_ref[e]
    hi = ifirst_ref[e + 1]
    total = ifirst_ref[N_EXPERTS]
    nvalid = ifirst_ref[N_EXPERTS + 1]
    n_blocks = ys_hbm.shape[0] // EXPERT_ROWS

    def rows_of(big):
        return (BIG_BLOCKS if big else 1) * EXPERT_ROWS

    def hbm_rows(ref, i, big):
        return ref.at[pl.ds(pl.multiple_of(istart_ref[i] * EXPERT_ROWS, EXPERT_ROWS), rows_of(big)), :]

    def x_copy(i, big):
        slot = i % X_BUFFERS
        return pltpu.make_async_copy(hbm_rows(xs_hbm, i, big),
                                     xbuf.at[slot, pl.ds(0, rows_of(big)), :], xsem.at[slot])

    def y_copy(i, big):
        slot = i % Y_BUFFERS
        return pltpu.make_async_copy(ybuf.at[slot, pl.ds(0, rows_of(big)), :],
                                     hbm_rows(ys_hbm, i, big), ysem.at[slot])

    def by_size(i, fn):
        @pl.when(ibig_ref[i] != 0)
        def _():
            fn(True)

        @pl.when(ibig_ref[i] == 0)
        def _():
            fn(False)

    @pl.when(e == 0)
    def _():
        for k in range(X_BUFFERS - 1):
            @pl.when(k < total)
            def _():
                by_size(k, lambda big: x_copy(k, big).start())

    @pl.when(hi > lo)
    def _():
        w1b[...] = w1_ref[0, 0].astype(bf16)
        w3b[...] = w3_ref[0, 0].astype(bf16)
        w2b[...] = w2_ref[0, 0].astype(bf16)

    def item(i, carry):
        ahead = i + X_BUFFERS - 1

        @pl.when(ahead < total)
        def _():
            by_size(ahead, lambda big: x_copy(ahead, big).start())

        def run(big):
            rows = rows_of(big)
            x_copy(i, big).wait()
            hn = _unpack_rows(xbuf[i % X_BUFFERS, pl.ds(0, rows), :])
            h1 = _dot(hn, w1b[...])
            h3 = _dot(hn, w3b[...])
            hid = ((h1 / (1.0 + jnp.exp(-h1))) * h3).astype(bf16)
            y = _dot(hid, w2b[...])

            @pl.when(i >= Y_BUFFERS)
            def _():
                by_size(i - Y_BUFFERS, lambda b: y_copy(i - Y_BUFFERS, b).wait())

            ybuf[i % Y_BUFFERS, pl.ds(0, rows), :] = _pack_rows(y.astype(bf16).astype(f32))
            y_copy(i, big).start()

        by_size(i, run)
        return carry

    lax.fori_loop(lo, hi, item, 0)

    @pl.when(e == pl.num_programs(0) - 1)
    def _():
        for k in range(1, Y_BUFFERS + 1):
            @pl.when(total >= k)
            def _():
                by_size(total - k, lambda big: y_copy(total - k, big).wait())
        ybuf[0, pl.ds(0, EXPERT_ROWS), :] = jnp.zeros((EXPERT_ROWS, HALF), u32)

        def tail(g):
            dst = pl.multiple_of(g * EXPERT_ROWS, EXPERT_ROWS)
            return pltpu.make_async_copy(ybuf.at[0, pl.ds(0, EXPERT_ROWS), :],
                                         ys_hbm.at[pl.ds(dst, EXPERT_ROWS), :], ysem.at[0])
        lax.fori_loop(nvalid, n_blocks, lambda g, c: (tail(g).start(), c)[1], 0)
        lax.fori_loop(nvalid, n_blocks, lambda g, c: (tail(g).wait(), c)[1], 0)


def _experts(xs, item_first, item_start, item_big, w1, w3, w2, layer):
    w_index = lambda e, *_: (layer, e, 0, 0)
    big_rows = BIG_BLOCKS * EXPERT_ROWS
    grid_spec = pltpu.PrefetchScalarGridSpec(
        num_scalar_prefetch=3,
        grid=(N_EXPERTS,),
        in_specs=[
            pl.BlockSpec((1, 1, D_MODEL, EXPERT_FF), w_index),
            pl.BlockSpec((1, 1, D_MODEL, EXPERT_FF), w_index),
            pl.BlockSpec((1, 1, EXPERT_FF, D_MODEL), w_index),
            pl.BlockSpec(memory_space=pl.ANY),
        ],
        out_specs=pl.BlockSpec(memory_space=pl.ANY),
        scratch_shapes=[pltpu.VMEM((X_BUFFERS, big_rows, HALF), u32),
                        pltpu.VMEM((Y_BUFFERS, big_rows, HALF), u32),
                        pltpu.SemaphoreType.DMA((X_BUFFERS,)),
                        pltpu.SemaphoreType.DMA((Y_BUFFERS,)),
                        pltpu.VMEM((D_MODEL, EXPERT_FF), bf16),
                        pltpu.VMEM((D_MODEL, EXPERT_FF), bf16),
                        pltpu.VMEM((EXPERT_FF, D_MODEL), bf16)],
    )
    return pl.pallas_call(
        _expert_kernel,
        grid_spec=grid_spec,
        out_shape=jax.ShapeDtypeStruct(xs.shape, u32),
        compiler_params=pltpu.CompilerParams(
            dimension_semantics=("arbitrary",), vmem_limit_bytes=VMEM_LIMIT),
        name="moe_experts",
    )(item_first, item_start, item_big, w1, w3, w2, xs)


def _combine(x_ref, tab_ref, tab_next_ref, cmeta_ref, ys_hbm, ybuf, sems):
    i = pl.program_id(0)
    n_steps = pl.num_programs(0)
    tiles = x_ref.shape[0] // SORT_TILE
    slot = i % 2

    def copy_into(s, u):
        def make(rows, loc, glb):
            dst = pl.multiple_of(u * LOCAL_CAP + loc, RUN_CHUNK)
            return pltpu.make_async_copy(ys_hbm.at[pl.ds(pl.multiple_of(glb, RUN_CHUNK), rows), :],
                                         ybuf.at[s, pl.ds(dst, rows), :], sems.at[s])
        return make

    def fetch(tab, s):
        for u in range(tiles):
            _start_runs(tab, u, copy_into(s, u))

            def clear(c, carry):
                dst = pl.multiple_of(u * LOCAL_CAP + c * RUN_CHUNK, RUN_CHUNK)
                ybuf[s, pl.ds(dst, RUN_CHUNK), :] = jnp.zeros((RUN_CHUNK, HALF), u32)
                return carry
            lax.fori_loop(tab[u, 0, TAB_UNITS], MAX_CHUNKS, clear, 0)

    @pl.when(i == 0)
    def _():
        fetch(tab_ref, 0)

    @pl.when(i + 1 < n_steps)
    def _():
        fetch(tab_next_ref, 1 - slot)

    for u in range(tiles):
        _wait_runs(tab_ref[u, 0, TAB_UNITS], copy_into(slot, u))

    p = lax.broadcasted_iota(i32, (LOCAL_CAP, SORT_TILE), 0).astype(f32)
    parts = []
    for u in range(tiles):
        cs = slice(u * SORT_TILE, (u + 1) * SORT_TILE)
        pos0, pos1 = cmeta_ref[0:1, cs], cmeta_ref[1:2, cs]
        g = (jnp.where(p == pos0, cmeta_ref[2:3, cs], 0.0)
             + jnp.where(p == pos1, cmeta_ref[3:4, cs], 0.0))
        g_hi = g.astype(bf16)
        g_lo = (g - g_hi.astype(f32)).astype(bf16)
        y = _unpack_rows(ybuf[slot, pl.ds(u * LOCAL_CAP, LOCAL_CAP), :])
        both = _dot_tn(jnp.concatenate([g_hi, g_lo], axis=1), y)
        parts.append(x_ref[cs, :] + both[:SORT_TILE] + both[SORT_TILE:])
    return parts[0] if tiles == 1 else jnp.concatenate(parts, axis=0)


def _combine_specs(ts, n_steps):
    tiles = ts // SORT_TILE
    return [
        pl.BlockSpec((ts, D_MODEL), lambda i: (i, 0)),
        pl.BlockSpec((tiles, 1, TABLE_LEN), lambda i: (i, 0, 0), memory_space=pltpu.SMEM),
        pl.BlockSpec((tiles, 1, TABLE_LEN), lambda i: (jnp.minimum(i + 1, n_steps - 1), 0, 0),
                     memory_space=pltpu.SMEM),
        pl.BlockSpec((META_ROWS, ts), lambda i: (0, i)),
        pl.BlockSpec(memory_space=pl.ANY),
    ]


def _combine_scratch(ts):
    tiles = ts // SORT_TILE
    return [pltpu.VMEM((2, tiles * LOCAL_CAP, HALF), u32), pltpu.SemaphoreType.DMA((2,))]


def _gelu(z):
    return 0.5 * z * (1.0 + lax.erf(z * (2.0 ** -0.5)))


def _sgu_kernel(x_ref, ng_ref, wu_ref, wv_ref, lng_ref, lnb_ref,
                ws_ref, bs_ref, wo_ref, mng_ref, wrt_ref, br_ref,
                x3_ref, meta_ref, cnt_ref, vn_ref):
    ts = x_ref.shape[0]
    x2 = x_ref[...]
    h = (x2 * _rms(x2) * ng_ref[...]).astype(bf16)
    v = _gelu(_dot(h, wv_ref[...]))
    mu = jnp.mean(v, axis=-1, keepdims=True)
    vc = v - mu
    rstd = lax.rsqrt(jnp.mean(vc * vc, axis=-1, keepdims=True) + EPS)
    vn_ref[...] = (vc * rstd * lng_ref[...] + lnb_ref[...]).astype(bf16)

    pos = lax.broadcasted_iota(i32, (SGU_BLOCK, SGU_BLOCK), 0) // CHUNK
    src = lax.broadcasted_iota(i32, (SGU_BLOCK, SGU_BLOCK), 1) // CHUNK
    acc = x2
    for g in range(SGU_GROUPS):
        cs = slice(g * SGU_GC, (g + 1) * SGU_GC)
        ws = jnp.where(pos >= src, ws_ref[g], jnp.zeros((), bf16))
        u = _gelu(_dot(h, wu_ref[:, cs]))
        mixed = [_dot(ws, vn_ref[nb * SGU_BLOCK:(nb + 1) * SGU_BLOCK, cs]) + bs_ref[:, g:g + 1]
                 for nb in range(ts // SGU_BLOCK)]
        out = (u * jnp.concatenate(mixed, axis=0)).astype(bf16)
        acc = acc + _dot(out, wo_ref[cs, :])
    x3_ref[...] = acc
    _route(acc, mng_ref[...], wrt_ref[...], br_ref[...], meta_ref, cnt_ref)


def _sgu_layer(x2, norm_g, w_in, ln_g, ln_b, w_s, b_s, w_out, moe_norm_g, wrt, br, ts):
    t = x2.shape[0]
    n_steps = t // ts
    args = (norm_g.reshape(1, D_MODEL), w_in[:, :SGU_HALF].astype(bf16), w_in[:, SGU_HALF:].astype(bf16),
            ln_g.reshape(1, SGU_HALF), ln_b.reshape(1, SGU_HALF), w_s.astype(bf16), b_s.T,
            w_out.astype(bf16), moe_norm_g.reshape(1, D_MODEL), wrt, br)
    return pl.pallas_call(
        _sgu_kernel,
        grid=(n_steps,),
        in_specs=[pl.BlockSpec((ts, D_MODEL), lambda i: (i, 0))] + [_const_spec(a.shape) for a in args],
        out_specs=_route_out_specs(ts, lambda i: i),
        out_shape=_route_out_shapes(t),
        scratch_shapes=[pltpu.VMEM((ts, SGU_HALF), bf16)],
        compiler_params=pltpu.CompilerParams(
            dimension_semantics=("arbitrary",), vmem_limit_bytes=VMEM_LIMIT),
        name="sgu_mixer",
    )(x2, *args)


def _combine_kernel(x_ref, tab_ref, tab_next_ref, cmeta_ref, ys_hbm, out_ref, ybuf, sems):
    out_ref[...] = _combine(x_ref, tab_ref, tab_next_ref, cmeta_ref, ys_hbm, ybuf, sems)


def _final_kernel(x_ref, tab_ref, tab_next_ref, cmeta_ref, ys_hbm, ng_ref, out_ref, ybuf, sems):
    x = _combine(x_ref, tab_ref, tab_next_ref, cmeta_ref, ys_hbm, ybuf, sems)
    out_ref[...] = x * _rms(x) * ng_ref[...]


def _combine_layer(x, table, cmeta, ys, norm_g, ts):
    t = x.shape[0]
    n_steps = t // ts
    final = norm_g is not None
    extra = (norm_g.reshape(1, D_MODEL),) if final else ()
    return pl.pallas_call(
        _final_kernel if final else _combine_kernel,
        grid=(n_steps,),
        in_specs=_combine_specs(ts, n_steps) + [_const_spec(a.shape) for a in extra],
        out_specs=pl.BlockSpec((ts, D_MODEL), lambda i: (i, 0)),
        out_shape=jax.ShapeDtypeStruct((t, D_MODEL), f32),
        scratch_shapes=_combine_scratch(ts),
        compiler_params=pltpu.CompilerParams(
            dimension_semantics=("arbitrary",), vmem_limit_bytes=VMEM_LIMIT),
        name="final_norm" if final else "moe_combine",
    )(x, table, table, cmeta, ys, *extra)


def _moe(x, meta, cnt, norm_g, w1, w3, w2, layer):
    table, off_col, items, ztable = _plan(cnt, x.shape[0])
    sorted_x, cmeta = _dispatch(x, meta, table, off_col, ztable, norm_g)
    return _experts(sorted_x, *items, w1, w3, w2, layer), table, cmeta


def _forward(x, gla_norm, gla_w_in, gla_w_gate_up, gla_b_gate, gla_head_g, gla_w_out, sgu_norm, sgu_w_in,
             sgu_ln_g, sgu_ln_b, sgu_w_s, sgu_b_s, sgu_w_out, moe_norm, moe_w_group, moe_b_group,
             moe_w_sub, moe_b_sub, moe_w1, moe_w3, moe_w2, final_norm, *, ts_gla, ts_sgu, ts_fin):
    wrt0, br0 = _router_params(moe_w_group[0], moe_b_group[0], moe_w_sub[0], moe_b_sub[0])
    wrt1, br1 = _router_params(moe_w_group[1], moe_b_group[1], moe_w_sub[1], moe_b_sub[1])
    x1, meta0, cnt0 = _gla_layer(x, gla_norm[0], gla_w_in[0], gla_w_gate_up[0], gla_b_gate[0], gla_head_g[0],
                                 gla_w_out[0], moe_norm[0], wrt0, br0, ts_gla)
    ys0, table0, cmeta0 = _moe(x1, meta0, cnt0, moe_norm[0], moe_w1, moe_w3, moe_w2, 0)
    x2 = _combine_layer(x1, table0, cmeta0, ys0, None, ts_fin)
    x3, meta1, cnt1 = _sgu_layer(x2, sgu_norm[0], sgu_w_in[0], sgu_ln_g[0], sgu_ln_b[0],
                                 sgu_w_s[0], sgu_b_s[0], sgu_w_out[0], moe_norm[1], wrt1, br1, ts_sgu)
    ys1, table1, cmeta1 = _moe(x3, meta1, cnt1, moe_norm[1], moe_w1, moe_w3, moe_w2, 1)
    out = _combine_layer(x3, table1, cmeta1, ys1, final_norm, ts_fin)
    return out.reshape(x.shape)


def kernel(x, gla_norm, gla_w_in, gla_w_gate_up, gla_b_gate, gla_head_g, gla_w_out, sgu_norm, sgu_w_in, sgu_ln_g, sgu_ln_b, sgu_w_s, sgu_b_s, sgu_w_out, moe_norm, moe_w_group, moe_b_group, moe_w_sub, moe_b_sub, moe_w1, moe_w3, moe_w2, final_norm):
    return _forward(x, gla_norm, gla_w_in, gla_w_gate_up, gla_b_gate, gla_head_g, gla_w_out, sgu_norm,
                    sgu_w_in, sgu_ln_g, sgu_ln_b, sgu_w_s, sgu_b_s, sgu_w_out, moe_norm, moe_w_group,
                    moe_b_group, moe_w_sub, moe_b_sub, moe_w1, moe_w3, moe_w2, final_norm,
                    ts_gla=1024, ts_sgu=1024, ts_fin=1024)
```

```python
import jax
import jax.numpy as jnp
from jax import lax
from jax.experimental import pallas as pl
from jax.experimental.pallas import tpu as pltpu

D_MODEL = 1024
HALF = D_MODEL // 2
EPS = 1e-6
LANES = 128
SUBLANES = 8

CHUNK = 64
GLA_HEADS = 4
GLA_DK = 128
GLA_DV = 256
GLA_HK = GLA_HEADS * GLA_DK
GLA_HV = GLA_HEADS * GLA_DV
GLA_GATE_RANK = 16
GLA_TAU = 16.0

SGU_BLOCK = 128
SGU_GROUPS = 4
SGU_HALF = 2048
SGU_GC = SGU_HALF // SGU_GROUPS

N_GROUPS = 4
EXPERTS_PER_GROUP = 8
N_EXPERTS = N_GROUPS * EXPERTS_PER_GROUP
TOP_K = 2
EXPERT_FF = 512
ROUTE_ROWS = 64
META_ROWS = 8

SORT_TILE = 256
RUN_CHUNK = SUBLANES
LOCAL_CAP = 768
MAX_CHUNKS = LOCAL_CAP // RUN_CHUNK
EXPERT_ROWS = 256
ZERO_CHUNK = EXPERT_ROWS
TABLE_LEN = 384
TAB_LOC, TAB_GLB, TAB_BIG = 0, MAX_CHUNKS, 2 * MAX_CHUNKS
TAB_COPIES, TAB_UNITS = 3 * MAX_CHUNKS, 3 * MAX_CHUNKS + 1
assert LOCAL_CAP >= TOP_K * SORT_TILE + N_EXPERTS * (RUN_CHUNK - 1)
assert TAB_UNITS < TABLE_LEN

VMEM_LIMIT = 56 * 1024 * 1024

f32 = jnp.float32
bf16 = jnp.bfloat16
i32 = jnp.int32
u32 = jnp.uint32


def _dot(a, b):
    return jnp.dot(a, b, preferred_element_type=f32)


def _dot_tn(a, b):
    return lax.dot_general(a, b, (((0,), (0,)), ((), ())), preferred_element_type=f32)


def _dot_nt(a, b):
    return lax.dot_general(a, b, (((1,), (1,)), ((), ())), preferred_element_type=f32)


def _rms(x):
    return lax.rsqrt(jnp.mean(x * x, axis=-1, keepdims=True) + EPS)


def _pack_rows(v):
    lo = lax.bitcast_convert_type(v[:, :HALF], u32)
    hi = lax.bitcast_convert_type(v[:, HALF:], u32)
    return lax.shift_right_logical(lo, jnp.uint32(16)) | (hi & jnp.uint32(0xFFFF0000))


def _unpack_rows(w):
    lo = lax.bitcast_convert_type(lax.shift_left(w, jnp.uint32(16)), f32)
    hi = lax.bitcast_convert_type(w & jnp.uint32(0xFFFF0000), f32)
    return jnp.concatenate([lo, hi], axis=-1).astype(bf16)


def _const_spec(shape):
    return pl.BlockSpec(shape, lambda *_: (0,) * len(shape))


def _route(x1, norm_g, wr_t, br, meta_ref, cnt_ref):
    n = x1.shape[0]
    hn = (x1 * _rms(x1) * norm_g).astype(bf16)
    lt = _dot_nt(wr_t, hn) + br
    rows = lax.broadcasted_iota(i32, (SUBLANES, n), 0)
    neg = jnp.float32(-jnp.inf)
    lg = jnp.where(rows < N_GROUPS, lt[0:SUBLANES], neg)
    gmax = jnp.max(lg, axis=0, keepdims=True)
    gidx = jnp.min(jnp.where(lg == gmax, rows, SUBLANES), axis=0, keepdims=True)
    g_w = 1.0 / jnp.sum(jnp.exp(lg - gmax), axis=0, keepdims=True)
    chosen = jnp.zeros((SUBLANES, n), f32)
    for g in range(N_GROUPS):
        chosen = jnp.where(gidx == g, lt[SUBLANES * (g + 1):SUBLANES * (g + 2)], chosen)
    m1 = jnp.max(chosen, axis=0, keepdims=True)
    i1 = jnp.min(jnp.where(chosen == m1, rows, SUBLANES), axis=0, keepdims=True)
    rest = jnp.where(rows == i1, neg, chosen)
    m2 = jnp.max(rest, axis=0, keepdims=True)
    i2 = jnp.min(jnp.where(rest == m2, rows, SUBLANES), axis=0, keepdims=True)
    t = jnp.exp(m2 - m1)
    s1 = 1.0 / (1.0 + t)
    s2 = t / (1.0 + t)
    e1 = gidx * EXPERTS_PER_GROUP + i1
    e2 = gidx * EXPERTS_PER_GROUP + i2
    zero = jnp.zeros((1, n), f32)
    meta_ref[...] = jnp.concatenate(
        [e1.astype(f32), e2.astype(f32), g_w * s1, g_w * s2, zero, zero, zero, zero], axis=0)
    ids = lax.broadcasted_iota(i32, (N_EXPERTS, n), 0)
    hits = (ids == e1).astype(f32) + (ids == e2).astype(f32)
    for u in range(n // SORT_TILE):
        cnt_ref[u * N_EXPERTS:(u + 1) * N_EXPERTS, :] = jnp.sum(
            hits[:, u * SORT_TILE:(u + 1) * SORT_TILE], axis=1, keepdims=True)


def _router_params(w_group, b_group, w_sub, b_sub):
    tail = ROUTE_ROWS - SUBLANES - N_EXPERTS
    wrt = jnp.concatenate([
        w_group.T, jnp.zeros((SUBLANES - N_GROUPS, D_MODEL), f32),
        jnp.transpose(w_sub, (0, 2, 1)).reshape(N_EXPERTS, D_MODEL),
        jnp.zeros((tail, D_MODEL), f32)], axis=0)
    br = jnp.concatenate([b_group, jnp.zeros((SUBLANES - N_GROUPS,), f32), b_sub.reshape(N_EXPERTS),
                          jnp.zeros((tail,), f32)]).reshape(ROUTE_ROWS, 1)
    return wrt.astype(bf16), br


def _route_out_specs(ts, index):
    tiles = ts // SORT_TILE
    return [pl.BlockSpec((ts, D_MODEL), lambda *g: (index(*g), 0)),
            pl.BlockSpec((META_ROWS, ts), lambda *g: (0, index(*g))),
            pl.BlockSpec((tiles * N_EXPERTS, 1), lambda *g: (index(*g), 0))]


def _route_out_shapes(t):
    return [jax.ShapeDtypeStruct((t, D_MODEL), f32),
            jax.ShapeDtypeStruct((META_ROWS, t), f32),
            jax.ShapeDtypeStruct((t // SORT_TILE * N_EXPERTS, 1), f32)]


def _gla_kernel(x_ref, ng_ref, wq_ref, wk_ref, wv_ref, wr_ref, wg_ref, wgu_ref, bg_ref,
                hg_ref, wo_ref, mng_ref, wrt_ref, br_ref,
                x1_ref, meta_ref, cnt_ref, st_ref, o_ref):
    ts = x_ref.shape[1]

    @pl.when(pl.program_id(1) == 0)
    def _():
        st_ref[...] = jnp.zeros_like(st_ref)

    x = x_ref[0]
    h = (x * _rms(x) * ng_ref[...]).astype(bf16)
    q = _dot(h, wq_ref[...]) * (GLA_DK ** -0.5)
    k = _dot(h, wk_ref[...])
    v = _dot(h, wv_ref[...]).astype(bf16)
    glr = _dot(h, wg_ref[...]).astype(bf16)
    gp = _dot(glr, wgu_ref[...]) + bg_ref[...]
    log_a = (jnp.minimum(gp, 0.0) - jnp.log(1.0 + jnp.exp(-jnp.abs(gp)))) * (1.0 / GLA_TAU)

    row = lax.broadcasted_iota(i32, (CHUNK, GLA_HK), 0)
    for c in range(ts // CHUNK):
        rs = slice(c * CHUNK, (c + 1) * CHUNK)
        b = log_a[rs]
        sh = 1
        while sh < CHUNK:
            b = b + jnp.where(row >= sh, pltpu.roll(b, sh, axis=0), 0.0)
            sh *= 2
        b_end = b[CHUNK - 1:CHUNK]
        kdec = (k[rs] * jnp.exp(b_end - b)).astype(bf16)
        decay = jnp.exp(b_end)
        qc = q[rs].astype(bf16)
        vc = v[rs]
        for hd in range(GLA_HEADS):
            ks = slice(hd * GLA_DK, (hd + 1) * GLA_DK)
            vs = slice(hd * GLA_DV, (hd + 1) * GLA_DV)
            st = st_ref[hd] * decay[:, ks] + _dot_tn(vc[:, vs], kdec[:, ks])
            st_ref[hd] = st
            o_ref[rs, vs] = _dot_nt(qc[:, ks], st.astype(bf16))

    r = _dot(h, wr_ref[...])
    gated = []
    for hd in range(GLA_HEADS):
        vs = slice(hd * GLA_DV, (hd + 1) * GLA_DV)
        oh = o_ref[:, vs]
        rh = r[:, vs]
        gated.append(oh * _rms(oh) * hg_ref[:, vs] * (rh / (1.0 + jnp.exp(-rh))))
    y = _dot(jnp.concatenate(gated, axis=-1).astype(bf16), wo_ref[...])
    x1 = x + y
    x1_ref[...] = x1
    _route(x1, mng_ref[...], wrt_ref[...], br_ref[...], meta_ref, cnt_ref)


def _gla_layer(x, norm_g, w_in, w_gate_up, b_gate, head_g, w_out, moe_norm_g, wrt, br, ts):
    bsz, seq, _ = x.shape
    t = bsz * seq
    wq = w_in[:, 0:GLA_HK].astype(bf16)
    wk = w_in[:, GLA_HK:2 * GLA_HK].astype(bf16)
    wv = w_in[:, 2 * GLA_HK:2 * GLA_HK + GLA_HV].astype(bf16)
    wr = w_in[:, 2 * GLA_HK + GLA_HV:2 * GLA_HK + 2 * GLA_HV].astype(bf16)
    wg = jnp.pad(w_in[:, 2 * GLA_HK + 2 * GLA_HV:], ((0, 0), (0, LANES - GLA_GATE_RANK))).astype(bf16)
    wgu = jnp.pad(w_gate_up, ((0, LANES - GLA_GATE_RANK), (0, 0))).astype(bf16)
    n_s = seq // ts
    args = (x, norm_g.reshape(1, D_MODEL), wq, wk, wv, wr, wg, wgu, b_gate.reshape(1, GLA_HK),
            head_g.reshape(1, GLA_HV), w_out.astype(bf16), moe_norm_g.reshape(1, D_MODEL), wrt, br)
    in_specs = [pl.BlockSpec((1, ts, D_MODEL), lambda b, s: (b, s, 0))]
    in_specs += [_const_spec(a.shape) for a in args[1:]]
    return pl.pallas_call(
        _gla_kernel,
        grid=(bsz, n_s),
        in_specs=in_specs,
        out_specs=_route_out_specs(ts, lambda b, s: b * n_s + s),
        out_shape=_route_out_shapes(t),
        scratch_shapes=[pltpu.VMEM((GLA_HEADS, GLA_DV, GLA_DK), f32),
                        pltpu.VMEM((ts, GLA_HV), f32)],
        compiler_params=pltpu.CompilerParams(
            dimension_semantics=("arbitrary", "arbitrary"), vmem_limit_bytes=VMEM_LIMIT),
        name="gla_mixer",
    )(*args)


def _n_expert_blocks(t):
    tiles = t // SORT_TILE
    worst = t * TOP_K + tiles * N_EXPERTS * (RUN_CHUNK - 1) + N_EXPERTS * (EXPERT_ROWS - 1)
    return -(-worst // EXPERT_ROWS)


def _zero_table_len(t):
    tail = _n_expert_blocks(t) * EXPERT_ROWS - t * TOP_K
    n = N_EXPERTS + -(-tail // ZERO_CHUNK) + 1
    return -(-(n + 1) // LANES) * LANES


def _cumsum(x, axis):
    x = jnp.moveaxis(x, axis, -1)
    n = x.shape[-1]
    upto = jnp.arange(n, dtype=i32)[:, None] <= jnp.arange(n, dtype=i32)[None, :]
    return jnp.moveaxis(jnp.sum(x[..., :, None] * upto.astype(i32), axis=-2), -1, axis)


def _flat_chunks(n_per, max_n):
    cum = _cumsum(n_per, -1)
    c = jnp.arange(max_n, dtype=i32)
    seg = jnp.minimum(jnp.sum((c[:, None] >= cum[..., None, :]).astype(i32), axis=-1), n_per.shape[-1] - 1)
    onehot = (seg[..., None] == jnp.arange(n_per.shape[-1], dtype=i32)).astype(i32)
    return onehot, c - _pick(onehot, cum - n_per), cum[..., -1]


def _pick(onehot, per_segment):
    return jnp.sum(onehot * per_segment[..., None, :], axis=-1)


def _plan(cnt, t):
    tiles = t // SORT_TILE
    cnt = cnt.reshape(tiles, N_EXPERTS).astype(i32)
    nch = (cnt + RUN_CHUNK - 1) // RUN_CHUNK
    run = nch * RUN_CHUNK
    counts = jnp.sum(run, axis=0)
    padded = ((counts + EXPERT_ROWS - 1) // EXPERT_ROWS) * EXPERT_ROWS
    pad_end = _cumsum(padded, 0)
    pad_start = pad_end - padded
    base = pad_start[None, :] + _cumsum(run, 0) - run
    n_pair = nch // 2
    onehot, j, n_copies = _flat_chunks(n_pair + nch % 2, MAX_CHUNKS)
    off = _cumsum(run, 1) - run
    loc = _pick(onehot, off) + j * (2 * RUN_CHUNK)
    glb = _pick(onehot, base) + j * (2 * RUN_CHUNK)
    big = (j < _pick(onehot, n_pair)).astype(i32)
    units = jnp.sum(nch, axis=1)
    fill = jnp.zeros((tiles, TABLE_LEN - TAB_UNITS - 1), i32)
    table = jnp.concatenate([loc, glb, big, n_copies[:, None], units[:, None], fill],
                            axis=1).reshape(tiles, 1, TABLE_LEN)
    off_col = off.astype(f32).reshape(tiles, N_EXPERTS, 1)

    nb = _n_expert_blocks(t)
    n_blk = padded // EXPERT_ROWS
    n_big = n_blk // BIG_BLOCKS
    n_items = n_big + n_blk % BIG_BLOCKS
    ionehot, k, n_total = _flat_chunks(n_items, _n_work_items(t))
    k_big = _pick(ionehot, n_big)
    item_big = (k < k_big).astype(i32)
    item_start = _pick(ionehot, pad_start // EXPERT_ROWS) + jnp.where(
        k < k_big, BIG_BLOCKS * k, BIG_BLOCKS * k_big + k - k_big)
    icum = _cumsum(n_items, 0)
    item_first = jnp.concatenate([icum - n_items, n_total[None], pad_end[-1:] // EXPERT_ROWS])
    items = (item_first, item_start, item_big)

    region_end = jnp.concatenate([pad_start[1:], jnp.full((1,), nb * EXPERT_ROWS, i32)])
    zlen = region_end - (pad_start + counts)
    nz = (zlen + ZERO_CHUNK - 1) // ZERO_CHUNK
    zl = _zero_table_len(t)
    zonehot, zj, nztot = _flat_chunks(nz, zl - 1)
    zstart = _pick(zonehot, region_end) - (zj + 1) * ZERO_CHUNK
    ztable = jnp.concatenate([zstart, nztot[None]]).reshape(1, 1, zl)
    return table, off_col, items, ztable


def _start_runs(tab, u, make):
    n = tab[u, 0, TAB_COPIES]

    def one(c, priority):
        loc, glb = tab[u, 0, TAB_LOC + c], tab[u, 0, TAB_GLB + c]

        @pl.when(tab[u, 0, TAB_BIG + c] != 0)
        def _():
            make(2 * RUN_CHUNK, loc, glb).start(priority=priority)

        @pl.when(tab[u, 0, TAB_BIG + c] == 0)
        def _():
            make(RUN_CHUNK, loc, glb).start(priority=priority)

    def pair(k, carry):
        one(2 * k, 0)

        @pl.when(2 * k + 1 < n)
        def _():
            one(2 * k + 1, 1)
        return carry
    lax.fori_loop(0, (n + 1) // 2, pair, 0)


def _wait_runs(units, make):
    lax.fori_loop(0, units // 2, lambda c, carry: (make(2 * RUN_CHUNK, 0, 0).wait(), carry)[1], 0)

    @pl.when(units % 2 == 1)
    def _():
        make(RUN_CHUNK, 0, 0).wait()


def _dispatch_kernel(tab_ref, ztab_ref, x_ref, meta_ref, offc_ref, ng_ref,
                     out_hbm, cmeta_ref, stage, zbuf, sems, zsem, prev_n):
    t = pl.program_id(0)
    slot = t % 2
    zl = ztab_ref.shape[2]

    def zero_copy(z):
        dst = pl.multiple_of(ztab_ref[0, 0, z], RUN_CHUNK)
        return pltpu.make_async_copy(zbuf, out_hbm.at[pl.ds(dst, ZERO_CHUNK), :], zsem)

    @pl.when(t == 0)
    def _():
        zbuf[...] = jnp.zeros_like(zbuf)
        nz = ztab_ref[0, 0, zl - 1]
        lax.fori_loop(0, nz, lambda z, c: (zero_copy(z).start(), c)[1], 0)
        lax.fori_loop(0, nz, lambda z, c: (zero_copy(z).wait(), c)[1], 0)

    tiles = x_ref.shape[0] // SORT_TILE
    ids = lax.broadcasted_iota(i32, (N_EXPERTS, SORT_TILE), 0)
    before = (lax.broadcasted_iota(i32, (SORT_TILE, SORT_TILE), 0)
              < lax.broadcasted_iota(i32, (SORT_TILE, SORT_TILE), 1)).astype(bf16)
    p = lax.broadcasted_iota(i32, (LOCAL_CAP, SORT_TILE), 0).astype(f32)
    zero = jnp.zeros((1, SORT_TILE), f32)
    for u in range(tiles):
        cs = slice(u * SORT_TILE, (u + 1) * SORT_TILE)
        x = x_ref[cs, :]
        hn = (x * _rms(x) * ng_ref[...]).astype(bf16)
        oh0 = ids == meta_ref[0:1, cs].astype(i32)
        oh1 = ids == meta_ref[1:2, cs].astype(i32)
        c0 = _dot(oh0.astype(bf16), before)
        c1 = _dot(oh1.astype(bf16), before)
        n0 = jnp.sum(oh0.astype(f32), axis=1, keepdims=True)
        offc = offc_ref[u]
        pos0 = jnp.sum(jnp.where(oh0, offc + c0, 0.0), axis=0, keepdims=True)
        pos1 = jnp.sum(jnp.where(oh1, offc + n0 + c1, 0.0), axis=0, keepdims=True)
        cmeta_ref[:, cs] = jnp.concatenate(
            [pos0, pos1, meta_ref[2:3, cs], meta_ref[3:4, cs], zero, zero, zero, zero], axis=0)
        perm = jnp.logical_or(p == pos0, p == pos1).astype(bf16)
        stage[slot, pl.ds(u * LOCAL_CAP, LOCAL_CAP), :] = _pack_rows(_dot(perm, hn))

    def copy_from(s, u):
        def make(rows, loc, glb):
            src = pl.multiple_of(u * LOCAL_CAP + loc, RUN_CHUNK)
            return pltpu.make_async_copy(stage.at[s, pl.ds(src, rows), :],
                                         out_hbm.at[pl.ds(pl.multiple_of(glb, RUN_CHUNK), rows), :], sems.at[s])
        return make

    @pl.when(t > 0)
    def _():
        _wait_runs(prev_n[0], copy_from(1 - slot, 0))

    units = 0
    for u in range(tiles):
        _start_runs(tab_ref, u, copy_from(slot, u))
        units = units + tab_ref[u, 0, TAB_UNITS]
    prev_n[0] = units

    @pl.when(t == pl.num_programs(0) - 1)
    def _():
        _wait_runs(units, copy_from(slot, 0))


def _dispatch(x, meta, table, off_col, ztable, norm_g, ts):
    t = x.shape[0]
    tiles = ts // SORT_TILE
    zl = ztable.shape[2]
    nb = _n_expert_blocks(t)
    return pl.pallas_call(
        _dispatch_kernel,
        grid=(t // ts,),
        in_specs=[
            pl.BlockSpec((tiles, 1, TABLE_LEN), lambda i: (i, 0, 0), memory_space=pltpu.SMEM),
            pl.BlockSpec((1, 1, zl), lambda i: (0, 0, 0), memory_space=pltpu.SMEM),
            pl.BlockSpec((ts, D_MODEL), lambda i: (i, 0)),
            pl.BlockSpec((META_ROWS, ts), lambda i: (0, i)),
            pl.BlockSpec((tiles, N_EXPERTS, 1), lambda i: (i, 0, 0)),
            _const_spec((1, D_MODEL)),
        ],
        out_specs=[pl.BlockSpec(memory_space=pl.ANY),
                   pl.BlockSpec((META_ROWS, ts), lambda i: (0, i))],
        out_shape=[jax.ShapeDtypeStruct((nb * EXPERT_ROWS, HALF), u32),
                   jax.ShapeDtypeStruct((META_ROWS, t), f32)],
        scratch_shapes=[pltpu.VMEM((2, tiles * LOCAL_CAP, HALF), u32),
                        pltpu.VMEM((ZERO_CHUNK, HALF), u32),
                        pltpu.SemaphoreType.DMA((2,)),
                        pltpu.SemaphoreType.DMA(()),
                        pltpu.SMEM((1,), i32)],
        compiler_params=pltpu.CompilerParams(
            dimension_semantics=("arbitrary",), vmem_limit_bytes=VMEM_LIMIT),
        name="moe_dispatch",
    )(table, ztable, x, meta, off_col, norm_g.reshape(1, D_MODEL))


X_BUFFERS = 3
Y_BUFFERS = 2
BIG_BLOCKS = 4


def _n_work_items(t):
    return _n_expert_blocks(t) // BIG_BLOCKS + (BIG_BLOCKS - 1) * N_EXPERTS + 1


def _expert_kernel(ifirst_ref, istart_ref, ibig_ref, w1_ref, w3_ref, w2_ref, xs_hbm, ys_hbm,
                   xbuf, ybuf, xsem, ysem, w1b, w3b, w2b):
    e = pl.program_id(0)
    lo = ifirst_ref[e]
    hi = ifirst_ref[e + 1]
    total = ifirst_ref[N_EXPERTS]
    nvalid = ifirst_ref[N_EXPERTS + 1]
    n_blocks = ys_hbm.shape[0] // EXPERT_ROWS

    def rows_of(big):
        return (BIG_BLOCKS if big else 1) * EXPERT_ROWS

    def hbm_rows(ref, i, big):
        return ref.at[pl.ds(pl.multiple_of(istart_ref[i] * EXPERT_ROWS, EXPERT_ROWS), rows_of(big)), :]

    def x_copy(i, big):
        slot = i % X_BUFFERS
        return pltpu.make_async_copy(hbm_rows(xs_hbm, i, big),
                                     xbuf.at[slot, pl.ds(0, rows_of(big)), :], xsem.at[slot])

    def y_copy(i, big):
        slot = i % Y_BUFFERS
        return pltpu.make_async_copy(ybuf.at[slot, pl.ds(0, rows_of(big)), :],
                                     hbm_rows(ys_hbm, i, big), ysem.at[slot])

    def by_size(i, fn):
        @pl.when(ibig_ref[i] != 0)
        def _():
            fn(True)

        @pl.when(ibig_ref[i] == 0)
        def _():
            fn(False)

    @pl.when(e == 0)
    def _():
        for k in range(X_BUFFERS - 1):
            @pl.when(k < total)
            def _():
                by_size(k, lambda big: x_copy(k, big).start())

    @pl.when(hi > lo)
    def _():
        w1b[...] = w1_ref[0, 0].astype(bf16)
        w3b[...] = w3_ref[0, 0].astype(bf16)
        w2b[...] = w2_ref[0, 0].astype(bf16)

    def item(i, carry):
        ahead = i + X_BUFFERS - 1

        @pl.when(ahead < total)
        def _():
            by_size(ahead, lambda big: x_copy(ahead, big).start())

        def run(big):
            rows = rows_of(big)
            x_copy(i, big).wait()
            hn = _unpack_rows(xbuf[i % X_BUFFERS, pl.ds(0, rows), :])
            h1 = _dot(hn, w1b[...])
            h3 = _dot(hn, w3b[...])
            hid = ((h1 / (1.0 + jnp.exp(-h1))) * h3).astype(bf16)
            y = _dot(hid, w2b[...])

            @pl.when(i >= Y_BUFFERS)
            def _():
                by_size(i - Y_BUFFERS, lambda b: y_copy(i - Y_BUFFERS, b).wait())

            ybuf[i % Y_BUFFERS, pl.ds(0, rows), :] = _pack_rows(y.astype(bf16).astype(f32))
            y_copy(i, big).start()

        by_size(i, run)
        return carry

    lax.fori_loop(lo, hi, item, 0)

    @pl.when(e == pl.num_programs(0) - 1)
    def _():
        for k in range(1, Y_BUFFERS + 1):
            @pl.when(total >= k)
            def _():
                by_size(total - k, lambda big: y_copy(total - k, big).wait())
        ybuf[0, pl.ds(0, EXPERT_ROWS), :] = jnp.zeros((EXPERT_ROWS, HALF), u32)

        def tail(g):
            dst = pl.multiple_of(g * EXPERT_ROWS, EXPERT_ROWS)
            return pltpu.make_async_copy(ybuf.at[0, pl.ds(0, EXPERT_ROWS), :],
                                         ys_hbm.at[pl.ds(dst, EXPERT_ROWS), :], ysem.at[0])
        lax.fori_loop(nvalid, n_blocks, lambda g, c: (tail(g).start(), c)[1], 0)
        lax.fori_loop(nvalid, n_blocks, lambda g, c: (tail(g).wait(), c)[1], 0)


def _experts(xs, item_first, item_start, item_big, w1, w3, w2, layer):
    w_index = lambda e, *_: (layer, e, 0, 0)
    big_rows = BIG_BLOCKS * EXPERT_ROWS
    grid_spec = pltpu.PrefetchScalarGridSpec(
        num_scalar_prefetch=3,
        grid=(N_EXPERTS,),
        in_specs=[
            pl.BlockSpec((1, 1, D_MODEL, EXPERT_FF), w_index),
            pl.BlockSpec((1, 1, D_MODEL, EXPERT_FF), w_index),
            pl.BlockSpec((1, 1, EXPERT_FF, D_MODEL), w_index),
            pl.BlockSpec(memory_space=pl.ANY),
        ],
        out_specs=pl.BlockSpec(memory_space=pl.ANY),
        scratch_shapes=[pltpu.VMEM((X_BUFFERS, big_rows, HALF), u32),
                        pltpu.VMEM((Y_BUFFERS, big_rows, HALF), u32),
                        pltpu.SemaphoreType.DMA((X_BUFFERS,)),
                        pltpu.SemaphoreType.DMA((Y_BUFFERS,)),
                        pltpu.VMEM((D_MODEL, EXPERT_FF), bf16),
                        pltpu.VMEM((D_MODEL, EXPERT_FF), bf16),
                        pltpu.VMEM((EXPERT_FF, D_MODEL), bf16)],
    )
    return pl.pallas_call(
        _expert_kernel,
        grid_spec=grid_spec,
        out_shape=jax.ShapeDtypeStruct(xs.shape, u32),
        compiler_params=pltpu.CompilerParams(
            dimension_semantics=("arbitrary",), vmem_limit_bytes=VMEM_LIMIT),
        name="moe_experts",
    )(item_first, item_start, item_big, w1, w3, w2, xs)


def _combine(x_ref, tab_ref, tab_next_ref, cmeta_ref, ys_hbm, ybuf, sems):
    i = pl.program_id(0)
    n_steps = pl.num_programs(0)
    tiles = x_ref.shape[0] // SORT_TILE
    slot = i % 2

    def copy_into(s, u):
        def make(rows, loc, glb):
            dst = pl.multiple_of(u * LOCAL_CAP + loc, RUN_CHUNK)
            return pltpu.make_async_copy(ys_hbm.at[pl.ds(pl.multiple_of(glb, RUN_CHUNK), rows), :],
                                         ybuf.at[s, pl.ds(dst, rows), :], sems.at[s])
        return make

    def fetch(tab, s):
        for u in range(tiles):
            _start_runs(tab, u, copy_into(s, u))

            def clear(c, carry):
                dst = pl.multiple_of(u * LOCAL_CAP + c * RUN_CHUNK, RUN_CHUNK)
                ybuf[s, pl.ds(dst, RUN_CHUNK), :] = jnp.zeros((RUN_CHUNK, HALF), u32)
                return carry
            lax.fori_loop(tab[u, 0, TAB_UNITS], MAX_CHUNKS, clear, 0)

    @pl.when(i == 0)
    def _():
        fetch(tab_ref, 0)

    @pl.when(i + 1 < n_steps)
    def _():
        fetch(tab_next_ref, 1 - slot)

    for u in range(tiles):
        _wait_runs(tab_ref[u, 0, TAB_UNITS], copy_into(slot, u))

    p = lax.broadcasted_iota(i32, (LOCAL_CAP, SORT_TILE), 0).astype(f32)
    parts = []
    for u in range(tiles):
        cs = slice(u * SORT_TILE, (u + 1) * SORT_TILE)
        pos0, pos1 = cmeta_ref[0:1, cs], cmeta_ref[1:2, cs]
        g = (jnp.where(p == pos0, cmeta_ref[2:3, cs], 0.0)
             + jnp.where(p == pos1, cmeta_ref[3:4, cs], 0.0))
        g_hi = g.astype(bf16)
        g_lo = (g - g_hi.astype(f32)).astype(bf16)
        y = _unpack_rows(ybuf[slot, pl.ds(u * LOCAL_CAP, LOCAL_CAP), :])
        both = _dot_tn(jnp.concatenate([g_hi, g_lo], axis=1), y)
        parts.append(x_ref[cs, :] + both[:SORT_TILE] + both[SORT_TILE:])
    return parts[0] if tiles == 1 else jnp.concatenate(parts, axis=0)


def _combine_specs(ts, n_steps):
    tiles = ts // SORT_TILE
    return [
        pl.BlockSpec((ts, D_MODEL), lambda i: (i, 0)),
        pl.BlockSpec((tiles, 1, TABLE_LEN), lambda i: (i, 0, 0), memory_space=pltpu.SMEM),
        pl.BlockSpec((tiles, 1, TABLE_LEN), lambda i: (jnp.minimum(i + 1, n_steps - 1), 0, 0),
                     memory_space=pltpu.SMEM),
        pl.BlockSpec((META_ROWS, ts), lambda i: (0, i)),
        pl.BlockSpec(memory_space=pl.ANY),
    ]


def _combine_scratch(ts):
    tiles = ts // SORT_TILE
    return [pltpu.VMEM((2, tiles * LOCAL_CAP, HALF), u32), pltpu.SemaphoreType.DMA((2,))]


def _gelu(z):
    return 0.5 * z * (1.0 + lax.erf(z * (2.0 ** -0.5)))


def _sgu_kernel(x_ref, ng_ref, wu_ref, wv_ref, lng_ref, lnb_ref,
                ws_ref, bs_ref, wo_ref, mng_ref, wrt_ref, br_ref,
                x3_ref, meta_ref, cnt_ref, vn_ref):
    ts = x_ref.shape[0]
    x2 = x_ref[...]
    h = (x2 * _rms(x2) * ng_ref[...]).astype(bf16)
    v = _gelu(_dot(h, wv_ref[...]))
    mu = jnp.mean(v, axis=-1, keepdims=True)
    vc = v - mu
    rstd = lax.rsqrt(jnp.mean(vc * vc, axis=-1, keepdims=True) + EPS)
    vn_ref[...] = (vc * rstd * lng_ref[...] + lnb_ref[...]).astype(bf16)

    pos = lax.broadcasted_iota(i32, (SGU_BLOCK, SGU_BLOCK), 0) // CHUNK
    src = lax.broadcasted_iota(i32, (SGU_BLOCK, SGU_BLOCK), 1) // CHUNK
    acc = x2
    for g in range(SGU_GROUPS):
        cs = slice(g * SGU_GC, (g + 1) * SGU_GC)
        ws = jnp.where(pos >= src, ws_ref[g], jnp.zeros((), bf16))
        u = _gelu(_dot(h, wu_ref[:, cs]))
        mixed = [_dot(ws, vn_ref[nb * SGU_BLOCK:(nb + 1) * SGU_BLOCK, cs]) + bs_ref[:, g:g + 1]
                 for nb in range(ts // SGU_BLOCK)]
        out = (u * jnp.concatenate(mixed, axis=0)).astype(bf16)
        acc = acc + _dot(out, wo_ref[cs, :])
    x3_ref[...] = acc
    _route(acc, mng_ref[...], wrt_ref[...], br_ref[...], meta_ref, cnt_ref)


def _sgu_layer(x2, norm_g, w_in, ln_g, ln_b, w_s, b_s, w_out, moe_norm_g, wrt, br, ts):
    t = x2.shape[0]
    n_steps = t // ts
    args = (norm_g.reshape(1, D_MODEL), w_in[:, :SGU_HALF].astype(bf16), w_in[:, SGU_HALF:].astype(bf16),
            ln_g.reshape(1, SGU_HALF), ln_b.reshape(1, SGU_HALF), w_s.astype(bf16), b_s.T,
            w_out.astype(bf16), moe_norm_g.reshape(1, D_MODEL), wrt, br)
    return pl.pallas_call(
        _sgu_kernel,
        grid=(n_steps,),
        in_specs=[pl.BlockSpec((ts, D_MODEL), lambda i: (i, 0))] + [_const_spec(a.shape) for a in args],
        out_specs=_route_out_specs(ts, lambda i: i),
        out_shape=_route_out_shapes(t),
        scratch_shapes=[pltpu.VMEM((ts, SGU_HALF), bf16)],
        compiler_params=pltpu.CompilerParams(
            dimension_semantics=("arbitrary",), vmem_limit_bytes=VMEM_LIMIT),
        name="sgu_mixer",
    )(x2, *args)


def _combine_kernel(x_ref, tab_ref, tab_next_ref, cmeta_ref, ys_hbm, out_ref, ybuf, sems):
    out_ref[...] = _combine(x_ref, tab_ref, tab_next_ref, cmeta_ref, ys_hbm, ybuf, sems)


def _final_kernel(x_ref, tab_ref, tab_next_ref, cmeta_ref, ys_hbm, ng_ref, out_ref, ybuf, sems):
    x = _combine(x_ref, tab_ref, tab_next_ref, cmeta_ref, ys_hbm, ybuf, sems)
    out_ref[...] = x * _rms(x) * ng_ref[...]


def _combine_layer(x, table, cmeta, ys, norm_g, ts):
    t = x.shape[0]
    n_steps = t // ts
    final = norm_g is not None
    extra = (norm_g.reshape(1, D_MODEL),) if final else ()
    return pl.pallas_call(
        _final_kernel if final else _combine_kernel,
        grid=(n_steps,),
        in_specs=_combine_specs(ts, n_steps) + [_const_spec(a.shape) for a in extra],
        out_specs=pl.BlockSpec((ts, D_MODEL), lambda i: (i, 0)),
        out_shape=jax.ShapeDtypeStruct((t, D_MODEL), f32),
        scratch_shapes=_combine_scratch(ts),
        compiler_params=pltpu.CompilerParams(
            dimension_semantics=("arbitrary",), vmem_limit_bytes=VMEM_LIMIT),
        name="final_norm" if final else "moe_combine",
    )(x, table, table, cmeta, ys, *extra)


def _moe(x, meta, cnt, norm_g, w1, w3, w2, layer, ts):
    table, off_col, items, ztable = _plan(cnt, x.shape[0])
    sorted_x, cmeta = _dispatch(x, meta, table, off_col, ztable, norm_g, ts)
    return _experts(sorted_x, *items, w1, w3, w2, layer), table, cmeta


def _forward(x, gla_norm, gla_w_in, gla_w_gate_up, gla_b_gate, gla_head_g, gla_w_out, sgu_norm, sgu_w_in,
             sgu_ln_g, sgu_ln_b, sgu_w_s, sgu_b_s, sgu_w_out, moe_norm, moe_w_group, moe_b_group,
             moe_w_sub, moe_b_sub, moe_w1, moe_w3, moe_w2, final_norm, *, ts_gla, ts_sgu, ts_fin):
    wrt0, br0 = _router_params(moe_w_group[0], moe_b_group[0], moe_w_sub[0], moe_b_sub[0])
    wrt1, br1 = _router_params(moe_w_group[1], moe_b_group[1], moe_w_sub[1], moe_b_sub[1])
    x1, meta0, cnt0 = _gla_layer(x, gla_norm[0], gla_w_in[0], gla_w_gate_up[0], gla_b_gate[0], gla_head_g[0],
                                 gla_w_out[0], moe_norm[0], wrt0, br0, ts_gla)
    ys0, table0, cmeta0 = _moe(x1, meta0, cnt0, moe_norm[0], moe_w1, moe_w3, moe_w2, 0, ts_fin)
    x2 = _combine_layer(x1, table0, cmeta0, ys0, None, ts_fin)
    x3, meta1, cnt1 = _sgu_layer(x2, sgu_norm[0], sgu_w_in[0], sgu_ln_g[0], sgu_ln_b[0],
                                 sgu_w_s[0], sgu_b_s[0], sgu_w_out[0], moe_norm[1], wrt1, br1, ts_sgu)
    ys1, table1, cmeta1 = _moe(x3, meta1, cnt1, moe_norm[1], moe_w1, moe_w3, moe_w2, 1, ts_fin)
    out = _combine_layer(x3, table1, cmeta1, ys1, final_norm, ts_fin)
    return out.reshape(x.shape)


def kernel(x, gla_norm, gla_w_in, gla_w_gate_up, gla_b_gate, gla_head_g, gla_w_out, sgu_norm, sgu_w_in, sgu_ln_g, sgu_ln_b, sgu_w_s, sgu_b_s, sgu_w_out, moe_norm, moe_w_group, moe_b_group, moe_w_sub, moe_b_sub, moe_w1, moe_w3, moe_w2, final_norm):
    return _forward(x, gla_norm, gla_w_in, gla_w_gate_up, gla_b_gate, gla_head_g, gla_w_out, sgu_norm,
                    sgu_w_in, sgu_ln_g, sgu_ln_b, sgu_w_s, sgu_b_s, sgu_w_out, moe_norm, moe_w_group,
                    moe_b_group, moe_w_sub, moe_b_sub, moe_w1, moe_w3, moe_w2, final_norm,
                    ts_gla=1024, ts_sgu=1024, ts_fin=1024)
```

```python
import jax
import jax.numpy as jnp
from jax import lax
from jax.experimental import pallas as pl
from jax.experimental.pallas import tpu as pltpu

D_MODEL = 1024
HALF = D_MODEL // 2
EPS = 1e-6
LANES = 128
SUBLANES = 8

CHUNK = 64
GLA_HEADS = 4
GLA_DK = 128
GLA_DV = 256
GLA_HK = GLA_HEADS * GLA_DK
GLA_HV = GLA_HEADS * GLA_DV
GLA_GATE_RANK = 16
GLA_TAU = 16.0

SGU_BLOCK = 128
SGU_GROUPS = 4
SGU_HALF = 2048
SGU_GC = SGU_HALF // SGU_GROUPS

N_GROUPS = 4
EXPERTS_PER_GROUP = 8
N_EXPERTS = N_GROUPS * EXPERTS_PER_GROUP
TOP_K = 2
EXPERT_FF = 512
ROUTE_ROWS = 64
META_ROWS = 8

SORT_TILE = 256
RUN_CHUNK = SUBLANES
LOCAL_CAP = 768
MAX_CHUNKS = LOCAL_CAP // RUN_CHUNK
EXPERT_ROWS = 256
ZERO_CHUNK = EXPERT_ROWS
MAX_PAIRS = MAX_CHUNKS // 2
TABLE_LEN = 256
TAB_PAIR_LOC, TAB_PAIR_GLB = 0, MAX_PAIRS
TAB_ODD_LOC, TAB_ODD_GLB = 2 * MAX_PAIRS, 2 * MAX_PAIRS + N_EXPERTS
TAB_PAIRS = 2 * MAX_PAIRS + 2 * N_EXPERTS
TAB_ODDS, TAB_UNITS = TAB_PAIRS + 1, TAB_PAIRS + 2
assert LOCAL_CAP >= TOP_K * SORT_TILE + N_EXPERTS * (RUN_CHUNK - 1)
assert TAB_UNITS < TABLE_LEN

VMEM_LIMIT = 56 * 1024 * 1024

f32 = jnp.float32
bf16 = jnp.bfloat16
i32 = jnp.int32
u32 = jnp.uint32


def _dot(a, b):
    return jnp.dot(a, b, preferred_element_type=f32)


def _dot_tn(a, b):
    return lax.dot_general(a, b, (((0,), (0,)), ((), ())), preferred_element_type=f32)


def _dot_nt(a, b):
    return lax.dot_general(a, b, (((1,), (1,)), ((), ())), preferred_element_type=f32)


def _rms(x):
    return lax.rsqrt(jnp.mean(x * x, axis=-1, keepdims=True) + EPS)


def _pack_rows(v):
    lo = lax.bitcast_convert_type(v[:, :HALF], u32)
    hi = lax.bitcast_convert_type(v[:, HALF:], u32)
    return lax.shift_right_logical(lo, jnp.uint32(16)) | (hi & jnp.uint32(0xFFFF0000))


def _unpack_rows(w):
    lo = lax.bitcast_convert_type(lax.shift_left(w, jnp.uint32(16)), f32)
    hi = lax.bitcast_convert_type(w & jnp.uint32(0xFFFF0000), f32)
    return jnp.concatenate([lo, hi], axis=-1).astype(bf16)


def _const_spec(shape):
    return pl.BlockSpec(shape, lambda *_: (0,) * len(shape))


def _route(x1, norm_g, wr_t, br, meta_ref, cnt_ref):
    n = x1.shape[0]
    hn = (x1 * _rms(x1) * norm_g).astype(bf16)
    lt = _dot_nt(wr_t, hn) + br
    rows = lax.broadcasted_iota(i32, (SUBLANES, n), 0)
    neg = jnp.float32(-jnp.inf)
    lg = jnp.where(rows < N_GROUPS, lt[0:SUBLANES], neg)
    gmax = jnp.max(lg, axis=0, keepdims=True)
    gidx = jnp.min(jnp.where(lg == gmax, rows, SUBLANES), axis=0, keepdims=True)
    g_w = 1.0 / jnp.sum(jnp.exp(lg - gmax), axis=0, keepdims=True)
    chosen = jnp.zeros((SUBLANES, n), f32)
    for g in range(N_GROUPS):
        chosen = jnp.where(gidx == g, lt[SUBLANES * (g + 1):SUBLANES * (g + 2)], chosen)
    m1 = jnp.max(chosen, axis=0, keepdims=True)
    i1 = jnp.min(jnp.where(chosen == m1, rows, SUBLANES), axis=0, keepdims=True)
    rest = jnp.where(rows == i1, neg, chosen)
    m2 = jnp.max(rest, axis=0, keepdims=True)
    i2 = jnp.min(jnp.where(rest == m2, rows, SUBLANES), axis=0, keepdims=True)
    t = jnp.exp(m2 - m1)
    s1 = 1.0 / (1.0 + t)
    s2 = t / (1.0 + t)
    e1 = gidx * EXPERTS_PER_GROUP + i1
    e2 = gidx * EXPERTS_PER_GROUP + i2
    zero = jnp.zeros((1, n), f32)
    meta_ref[...] = jnp.concatenate(
        [e1.astype(f32), e2.astype(f32), g_w * s1, g_w * s2, zero, zero, zero, zero], axis=0)
    ids = lax.broadcasted_iota(i32, (N_EXPERTS, n), 0)
    hits = (ids == e1).astype(f32) + (ids == e2).astype(f32)
    for u in range(n // SORT_TILE):
        cnt_ref[u * N_EXPERTS:(u + 1) * N_EXPERTS, :] = jnp.sum(
            hits[:, u * SORT_TILE:(u + 1) * SORT_TILE], axis=1, keepdims=True)


def _router_params(w_group, b_group, w_sub, b_sub):
    tail = ROUTE_ROWS - SUBLANES - N_EXPERTS
    wrt = jnp.concatenate([
        w_group.T, jnp.zeros((SUBLANES - N_GROUPS, D_MODEL), f32),
        jnp.transpose(w_sub, (0, 2, 1)).reshape(N_EXPERTS, D_MODEL),
        jnp.zeros((tail, D_MODEL), f32)], axis=0)
    br = jnp.concatenate([b_group, jnp.zeros((SUBLANES - N_GROUPS,), f32), b_sub.reshape(N_EXPERTS),
                          jnp.zeros((tail,), f32)]).reshape(ROUTE_ROWS, 1)
    return wrt.astype(bf16), br


def _route_out_specs(ts, index):
    tiles = ts // SORT_TILE
    return [pl.BlockSpec((ts, D_MODEL), lambda *g: (index(*g), 0)),
            pl.BlockSpec((META_ROWS, ts), lambda *g: (0, index(*g))),
            pl.BlockSpec((tiles * N_EXPERTS, 1), lambda *g: (index(*g), 0))]


def _route_out_shapes(t):
    return [jax.ShapeDtypeStruct((t, D_MODEL), f32),
            jax.ShapeDtypeStruct((META_ROWS, t), f32),
            jax.ShapeDtypeStruct((t // SORT_TILE * N_EXPERTS, 1), f32)]


def _gla_kernel(x_ref, ng_ref, wq_ref, wk_ref, wv_ref, wr_ref, wg_ref, wgu_ref, bg_ref,
                hg_ref, wo_ref, mng_ref, wrt_ref, br_ref,
                x1_ref, meta_ref, cnt_ref, st_ref, o_ref):
    ts = x_ref.shape[1]

    @pl.when(pl.program_id(1) == 0)
    def _():
        st_ref[...] = jnp.zeros_like(st_ref)

    x = x_ref[0]
    h = (x * _rms(x) * ng_ref[...]).astype(bf16)
    q = _dot(h, wq_ref[...]) * (GLA_DK ** -0.5)
    k = _dot(h, wk_ref[...])
    v = _dot(h, wv_ref[...]).astype(bf16)
    glr = _dot(h, wg_ref[...]).astype(bf16)
    gp = _dot(glr, wgu_ref[...]) + bg_ref[...]
    log_a = (jnp.minimum(gp, 0.0) - jnp.log(1.0 + jnp.exp(-jnp.abs(gp)))) * (1.0 / GLA_TAU)

    row = lax.broadcasted_iota(i32, (CHUNK, GLA_HK), 0)
    for c in range(ts // CHUNK):
        rs = slice(c * CHUNK, (c + 1) * CHUNK)
        b = log_a[rs]
        sh = 1
        while sh < CHUNK:
            b = b + jnp.where(row >= sh, pltpu.roll(b, sh, axis=0), 0.0)
            sh *= 2
        b_end = b[CHUNK - 1:CHUNK]
        kdec = (k[rs] * jnp.exp(b_end - b)).astype(bf16)
        decay = jnp.exp(b_end)
        qc = q[rs].astype(bf16)
        vc = v[rs]
        for hd in range(GLA_HEADS):
            ks = slice(hd * GLA_DK, (hd + 1) * GLA_DK)
            vs = slice(hd * GLA_DV, (hd + 1) * GLA_DV)
            st = st_ref[hd] * decay[:, ks] + _dot_tn(vc[:, vs], kdec[:, ks])
            st_ref[hd] = st
            o_ref[rs, vs] = _dot_nt(qc[:, ks], st.astype(bf16))

    r = _dot(h, wr_ref[...])
    gated = []
    for hd in range(GLA_HEADS):
        vs = slice(hd * GLA_DV, (hd + 1) * GLA_DV)
        oh = o_ref[:, vs]
        rh = r[:, vs]
        gated.append(oh * _rms(oh) * hg_ref[:, vs] * (rh / (1.0 + jnp.exp(-rh))))
    y = _dot(jnp.concatenate(gated, axis=-1).astype(bf16), wo_ref[...])
    x1 = x + y
    x1_ref[...] = x1
    _route(x1, mng_ref[...], wrt_ref[...], br_ref[...], meta_ref, cnt_ref)


def _gla_layer(x, norm_g, w_in, w_gate_up, b_gate, head_g, w_out, moe_norm_g, wrt, br, ts):
    bsz, seq, _ = x.shape
    t = bsz * seq
    wq = w_in[:, 0:GLA_HK].astype(bf16)
    wk = w_in[:, GLA_HK:2 * GLA_HK].astype(bf16)
    wv = w_in[:, 2 * GLA_HK:2 * GLA_HK + GLA_HV].astype(bf16)
    wr = w_in[:, 2 * GLA_HK + GLA_HV:2 * GLA_HK + 2 * GLA_HV].astype(bf16)
    wg = jnp.pad(w_in[:, 2 * GLA_HK + 2 * GLA_HV:], ((0, 0), (0, LANES - GLA_GATE_RANK))).astype(bf16)
    wgu = jnp.pad(w_gate_up, ((0, LANES - GLA_GATE_RANK), (0, 0))).astype(bf16)
    n_s = seq // ts
    args = (x, norm_g.reshape(1, D_MODEL), wq, wk, wv, wr, wg, wgu, b_gate.reshape(1, GLA_HK),
            head_g.reshape(1, GLA_HV), w_out.astype(bf16), moe_norm_g.reshape(1, D_MODEL), wrt, br)
    in_specs = [pl.BlockSpec((1, ts, D_MODEL), lambda b, s: (b, s, 0))]
    in_specs += [_const_spec(a.shape) for a in args[1:]]
    return pl.pallas_call(
        _gla_kernel,
        grid=(bsz, n_s),
        in_specs=in_specs,
        out_specs=_route_out_specs(ts, lambda b, s: b * n_s + s),
        out_shape=_route_out_shapes(t),
        scratch_shapes=[pltpu.VMEM((GLA_HEADS, GLA_DV, GLA_DK), f32),
                        pltpu.VMEM((ts, GLA_HV), f32)],
        compiler_params=pltpu.CompilerParams(
            dimension_semantics=("arbitrary", "arbitrary"), vmem_limit_bytes=VMEM_LIMIT),
        name="gla_mixer",
    )(*args)


def _n_expert_blocks(t):
    tiles = t // SORT_TILE
    worst = t * TOP_K + tiles * N_EXPERTS * (RUN_CHUNK - 1) + N_EXPERTS * (EXPERT_ROWS - 1)
    return -(-worst // EXPERT_ROWS)


def _zero_table_len(t):
    tail = _n_expert_blocks(t) * EXPERT_ROWS - t * TOP_K
    n = N_EXPERTS + -(-tail // ZERO_CHUNK) + 1
    return -(-(n + 1) // LANES) * LANES


def _cumsum(x, axis):
    x = jnp.moveaxis(x, axis, -1)
    n = x.shape[-1]
    upto = jnp.arange(n, dtype=i32)[:, None] <= jnp.arange(n, dtype=i32)[None, :]
    return jnp.moveaxis(jnp.sum(x[..., :, None] * upto.astype(i32), axis=-2), -1, axis)


def _flat_chunks(n_per, max_n):
    cum = _cumsum(n_per, -1)
    c = jnp.arange(max_n, dtype=i32)
    seg = jnp.minimum(jnp.sum((c[:, None] >= cum[..., None, :]).astype(i32), axis=-1), n_per.shape[-1] - 1)
    onehot = (seg[..., None] == jnp.arange(n_per.shape[-1], dtype=i32)).astype(i32)
    return onehot, c - _pick(onehot, cum - n_per), cum[..., -1]


def _pick(onehot, per_segment):
    return jnp.sum(onehot * per_segment[..., None, :], axis=-1)


def _plan(cnt, t):
    tiles = t // SORT_TILE
    cnt = cnt.reshape(tiles, N_EXPERTS).astype(i32)
    nch = (cnt + RUN_CHUNK - 1) // RUN_CHUNK
    run = nch * RUN_CHUNK
    counts = jnp.sum(run, axis=0)
    padded = ((counts + EXPERT_ROWS - 1) // EXPERT_ROWS) * EXPERT_ROWS
    pad_end = _cumsum(padded, 0)
    pad_start = pad_end - padded
    base = pad_start[None, :] + _cumsum(run, 0) - run
    n_pair = nch // 2
    off = _cumsum(run, 1) - run
    onehot, j, n_pairs = _flat_chunks(n_pair, MAX_PAIRS)
    pair_loc = _pick(onehot, off) + j * (2 * RUN_CHUNK)
    pair_glb = _pick(onehot, base) + j * (2 * RUN_CHUNK)
    onehot, _, n_odds = _flat_chunks(nch % 2, N_EXPERTS)
    odd_loc = _pick(onehot, off + n_pair * (2 * RUN_CHUNK))
    odd_glb = _pick(onehot, base + n_pair * (2 * RUN_CHUNK))
    units = jnp.sum(nch, axis=1)
    fill = jnp.zeros((tiles, TABLE_LEN - TAB_UNITS - 1), i32)
    table = jnp.concatenate([pair_loc, pair_glb, odd_loc, odd_glb, n_pairs[:, None], n_odds[:, None],
                             units[:, None], fill], axis=1).reshape(tiles, 1, TABLE_LEN)
    off_col = off.astype(f32).reshape(tiles, N_EXPERTS, 1)

    nb = _n_expert_blocks(t)
    n_blk = padded // EXPERT_ROWS
    n_big = n_blk // BIG_BLOCKS
    n_items = n_big + n_blk % BIG_BLOCKS
    ionehot, k, n_total = _flat_chunks(n_items, _n_work_items(t))
    k_big = _pick(ionehot, n_big)
    item_big = (k < k_big).astype(i32)
    item_start = _pick(ionehot, pad_start // EXPERT_ROWS) + jnp.where(
        k < k_big, BIG_BLOCKS * k, BIG_BLOCKS * k_big + k - k_big)
    icum = _cumsum(n_items, 0)
    item_first = jnp.concatenate([icum - n_items, n_total[None], pad_end[-1:] // EXPERT_ROWS])
    items = (item_first, item_start, item_big)

    region_end = jnp.concatenate([pad_start[1:], jnp.full((1,), nb * EXPERT_ROWS, i32)])
    zlen = region_end - (pad_start + counts)
    nz = (zlen + ZERO_CHUNK - 1) // ZERO_CHUNK
    zl = _zero_table_len(t)
    zonehot, zj, nztot = _flat_chunks(nz, zl - 1)
    zstart = _pick(zonehot, region_end) - (zj + 1) * ZERO_CHUNK
    ztable = jnp.concatenate([zstart, nztot[None]]).reshape(1, 1, zl)
    return table, off_col, items, ztable


def _start_runs(tab, u, make):
    for rows, n, loc0, glb0 in ((2 * RUN_CHUNK, tab[u, 0, TAB_PAIRS], TAB_PAIR_LOC, TAB_PAIR_GLB),
                                (RUN_CHUNK, tab[u, 0, TAB_ODDS], TAB_ODD_LOC, TAB_ODD_GLB)):
        for priority in (0, 1):
            def body(k, carry):
                c = 2 * k + priority
                make(rows, tab[u, 0, loc0 + c], tab[u, 0, glb0 + c]).start(priority=priority)
                return carry
            lax.fori_loop(0, (n + 1 - priority) // 2, body, 0)


def _wait_runs(units, make):
    left = units
    for per_wait in (16, 2, 1):
        lax.fori_loop(0, left // per_wait,
                      lambda c, carry: (make(per_wait * RUN_CHUNK, 0, 0).wait(), carry)[1], 0)
        left = left % per_wait


def _dispatch_kernel(tab_ref, ztab_ref, x_ref, meta_ref, offc_ref, ng_ref,
                     out_hbm, cmeta_ref, stage, zbuf, sems, zsem, prev_n):
    t = pl.program_id(0)
    slot = t % 2
    zl = ztab_ref.shape[2]

    def zero_copy(z):
        dst = pl.multiple_of(ztab_ref[0, 0, z], RUN_CHUNK)
        return pltpu.make_async_copy(zbuf, out_hbm.at[pl.ds(dst, ZERO_CHUNK), :], zsem)

    @pl.when(t == 0)
    def _():
        zbuf[...] = jnp.zeros_like(zbuf)
        nz = ztab_ref[0, 0, zl - 1]
        lax.fori_loop(0, nz, lambda z, c: (zero_copy(z).start(), c)[1], 0)
        lax.fori_loop(0, nz, lambda z, c: (zero_copy(z).wait(), c)[1], 0)

    tiles = x_ref.shape[0] // SORT_TILE
    ids = lax.broadcasted_iota(i32, (N_EXPERTS, SORT_TILE), 0)
    before = (lax.broadcasted_iota(i32, (SORT_TILE, SORT_TILE), 0)
              < lax.broadcasted_iota(i32, (SORT_TILE, SORT_TILE), 1)).astype(bf16)
    p = lax.broadcasted_iota(i32, (LOCAL_CAP, SORT_TILE), 0).astype(f32)
    zero = jnp.zeros((1, SORT_TILE), f32)
    for u in range(tiles):
        cs = slice(u * SORT_TILE, (u + 1) * SORT_TILE)
        x = x_ref[cs, :]
        hn = (x * _rms(x) * ng_ref[...]).astype(bf16)
        oh0 = ids == meta_ref[0:1, cs].astype(i32)
        oh1 = ids == meta_ref[1:2, cs].astype(i32)
        c0 = _dot(oh0.astype(bf16), before)
        c1 = _dot(oh1.astype(bf16), before)
        n0 = jnp.sum(oh0.astype(f32), axis=1, keepdims=True)
        offc = offc_ref[u]
        pos0 = jnp.sum(jnp.where(oh0, offc + c0, 0.0), axis=0, keepdims=True)
        pos1 = jnp.sum(jnp.where(oh1, offc + n0 + c1, 0.0), axis=0, keepdims=True)
        cmeta_ref[:, cs] = jnp.concatenate(
            [pos0, pos1, meta_ref[2:3, cs], meta_ref[3:4, cs], zero, zero, zero, zero], axis=0)
        perm = jnp.logical_or(p == pos0, p == pos1).astype(bf16)
        stage[slot, pl.ds(u * LOCAL_CAP, LOCAL_CAP), :] = _pack_rows(_dot(perm, hn))

    def copy_from(s, u):
        def make(rows, loc, glb):
            src = pl.multiple_of(u * LOCAL_CAP + loc, RUN_CHUNK)
            return pltpu.make_async_copy(stage.at[s, pl.ds(src, rows), :],
                                         out_hbm.at[pl.ds(pl.multiple_of(glb, RUN_CHUNK), rows), :], sems.at[s])
        return make

    @pl.when(t > 0)
    def _():
        _wait_runs(prev_n[0], copy_from(1 - slot, 0))

    units = 0
    for u in range(tiles):
        _start_runs(tab_ref, u, copy_from(slot, u))
        units = units + tab_ref[u, 0, TAB_UNITS]
    prev_n[0] = units

    @pl.when(t == pl.num_programs(0) - 1)
    def _():
        _wait_runs(units, copy_from(slot, 0))


def _dispatch(x, meta, table, off_col, ztable, norm_g, ts):
    t = x.shape[0]
    tiles = ts // SORT_TILE
    zl = ztable.shape[2]
    nb = _n_expert_blocks(t)
    return pl.pallas_call(
        _dispatch_kernel,
        grid=(t // ts,),
        in_specs=[
            pl.BlockSpec((tiles, 1, TABLE_LEN), lambda i: (i, 0, 0), memory_space=pltpu.SMEM),
            pl.BlockSpec((1, 1, zl), lambda i: (0, 0, 0), memory_space=pltpu.SMEM),
            pl.BlockSpec((ts, D_MODEL), lambda i: (i, 0)),
            pl.BlockSpec((META_ROWS, ts), lambda i: (0, i)),
            pl.BlockSpec((tiles, N_EXPERTS, 1), lambda i: (i, 0, 0)),
            _const_spec((1, D_MODEL)),
        ],
        out_specs=[pl.BlockSpec(memory_space=pl.ANY),
                   pl.BlockSpec((META_ROWS, ts), lambda i: (0, i))],
        out_shape=[jax.ShapeDtypeStruct((nb * EXPERT_ROWS, HALF), u32),
                   jax.ShapeDtypeStruct((META_ROWS, t), f32)],
        scratch_shapes=[pltpu.VMEM((2, tiles * LOCAL_CAP, HALF), u32),
                        pltpu.VMEM((ZERO_CHUNK, HALF), u32),
                        pltpu.SemaphoreType.DMA((2,)),
                        pltpu.SemaphoreType.DMA(()),
                        pltpu.SMEM((1,), i32)],
        compiler_params=pltpu.CompilerParams(
            dimension_semantics=("arbitrary",), vmem_limit_bytes=VMEM_LIMIT),
        name="moe_dispatch",
    )(table, ztable, x, meta, off_col, norm_g.reshape(1, D_MODEL))


X_BUFFERS = 3
Y_BUFFERS = 2
BIG_BLOCKS = 4


def _n_work_items(t):
    return _n_expert_blocks(t) // BIG_BLOCKS + (BIG_BLOCKS - 1) * N_EXPERTS + 1


def _expert_kernel(ifirst_ref, istart_ref, ibig_ref, w1_ref, w3_ref, w2_ref, xs_hbm, ys_hbm,
                   xbuf, ybuf, xsem, ysem, w1b, w3b, w2b):
    e = pl.program_id(0)
    lo = ifirst_ref[e]
    hi = ifirst_ref[e + 1]
    total = ifirst_ref[N_EXPERTS]
    nvalid = ifirst_ref[N_EXPERTS + 1]
    n_blocks = ys_hbm.shape[0] // EXPERT_ROWS

    def rows_of(big):
        return (BIG_BLOCKS if big else 1) * EXPERT_ROWS

    def hbm_rows(ref, i, big):
        return ref.at[pl.ds(pl.multiple_of(istart_ref[i] * EXPERT_ROWS, EXPERT_ROWS), rows_of(big)), :]

    def x_copy(i, big):
        slot = i % X_BUFFERS
        return pltpu.make_async_copy(hbm_rows(xs_hbm, i, big),
                                     xbuf.at[slot, pl.ds(0, rows_of(big)), :], xsem.at[slot])

    def y_copy(i, big):
        slot = i % Y_BUFFERS
        return pltpu.make_async_copy(ybuf.at[slot, pl.ds(0, rows_of(big)), :],
                                     hbm_rows(ys_hbm, i, big), ysem.at[slot])

    def by_size(i, fn):
        @pl.when(ibig_ref[i] != 0)
        def _():
            fn(True)

        @pl.when(ibig_ref[i] == 0)
        def _():
            fn(False)

    @pl.when(e == 0)
    def _():
        for k in range(X_BUFFERS - 1):
            @pl.when(k < total)
            def _():
                by_size(k, lambda big: x_copy(k, big).start())

    @pl.when(hi > lo)
    def _():
        w1b[...] = w1_ref[0, 0].astype(bf16)
        w3b[...] = w3_ref[0, 0].astype(bf16)
        w2b[...] = w2_ref[0, 0].astype(bf16)

    def item(i, carry):
        ahead = i + X_BUFFERS - 1

        @pl.when(ahead < total)
        def _():
            by_size(ahead, lambda big: x_copy(ahead, big).start())

        def run(big):
            rows = rows_of(big)
            x_copy(i, big).wait()
            hn = _unpack_rows(xbuf[i % X_BUFFERS, pl.ds(0, rows), :])
            h1 = _dot(hn, w1b[...])
            h3 = _dot(hn, w3b[...])
            hid = ((h1 / (1.0 + jnp.exp(-h1))) * h3).astype(bf16)
            y = _dot(hid, w2b[...])

            @pl.when(i >= Y_BUFFERS)
            def _():
                by_size(i - Y_BUFFERS, lambda b: y_copy(i - Y_BUFFERS, b).wait())

            ybuf[i % Y_BUFFERS, pl.ds(0, rows), :] = _pack_rows(y.astype(bf16).astype(f32))
            y_copy(i, big).start()

        by_size(i, run)
        return carry

    lax.fori_loop(lo, hi, item, 0)

    @pl.when(e == pl.num_programs(0) - 1)
    def _():
        for k in range(1, Y_BUFFERS + 1):
            @pl.when(total >= k)
            def _():
                by_size(total - k, lambda big: y_copy(total - k, big).wait())
        ybuf[0, pl.ds(0, EXPERT_ROWS), :] = jnp.zeros((EXPERT_ROWS, HALF), u32)

        def tail(g):
            dst = pl.multiple_of(g * EXPERT_ROWS, EXPERT_ROWS)
            return pltpu.make_async_copy(ybuf.at[0, pl.ds(0, EXPERT_ROWS), :],
                                         ys_hbm.at[pl.ds(dst, EXPERT_ROWS), :], ysem.at[0])
        lax.fori_loop(nvalid, n_blocks, lambda g, c: (tail(g).start(), c)[1], 0)
        lax.fori_loop(nvalid, n_blocks, lambda g, c: (tail(g).wait(), c)[1], 0)


def _experts(xs, item_first, item_start, item_big, w1, w3, w2, layer):
    w_index = lambda e, *_: (layer, e, 0, 0)
    big_rows = BIG_BLOCKS * EXPERT_ROWS
    grid_spec = pltpu.PrefetchScalarGridSpec(
        num_scalar_prefetch=3,
        grid=(N_EXPERTS,),
        in_specs=[
            pl.BlockSpec((1, 1, D_MODEL, EXPERT_FF), w_index),
            pl.BlockSpec((1, 1, D_MODEL, EXPERT_FF), w_index),
            pl.BlockSpec((1, 1, EXPERT_FF, D_MODEL), w_index),
            pl.BlockSpec(memory_space=pl.ANY),
        ],
        out_specs=pl.BlockSpec(memory_space=pl.ANY),
        scratch_shapes=[pltpu.VMEM((X_BUFFERS, big_rows, HALF), u32),
                        pltpu.VMEM((Y_BUFFERS, big_rows, HALF), u32),
                        pltpu.SemaphoreType.DMA((X_BUFFERS,)),
                        pltpu.SemaphoreType.DMA((Y_BUFFERS,)),
                        pltpu.VMEM((D_MODEL, EXPERT_FF), bf16),
                        pltpu.VMEM((D_MODEL, EXPERT_FF), bf16),
                        pltpu.VMEM((EXPERT_FF, D_MODEL), bf16)],
    )
    return pl.pallas_call(
        _expert_kernel,
        grid_spec=grid_spec,
        out_shape=jax.ShapeDtypeStruct(xs.shape, u32),
        compiler_params=pltpu.CompilerParams(
            dimension_semantics=("arbitrary",), vmem_limit_bytes=VMEM_LIMIT),
        name="moe_experts",
    )(item_first, item_start, item_big, w1, w3, w2, xs)


def _combine(x_ref, tab_ref, tab_next_ref, cmeta_ref, ys_hbm, ybuf, sems):
    i = pl.program_id(0)
    n_steps = pl.num_programs(0)
    tiles = x_ref.shape[0] // SORT_TILE
    slot = i % 2

    def copy_into(s, u):
        def make(rows, loc, glb):
            dst = pl.multiple_of(u * LOCAL_CAP + loc, RUN_CHUNK)
            return pltpu.make_async_copy(ys_hbm.at[pl.ds(pl.multiple_of(glb, RUN_CHUNK), rows), :],
                                         ybuf.at[s, pl.ds(dst, rows), :], sems.at[s])
        return make

    def fetch(tab, s):
        for u in range(tiles):
            _start_runs(tab, u, copy_into(s, u))

            def clear(c, carry):
                dst = pl.multiple_of(u * LOCAL_CAP + c * RUN_CHUNK, RUN_CHUNK)
                ybuf[s, pl.ds(dst, RUN_CHUNK), :] = jnp.zeros((RUN_CHUNK, HALF), u32)
                return carry
            lax.fori_loop(tab[u, 0, TAB_UNITS], MAX_CHUNKS, clear, 0)

    @pl.when(i == 0)
    def _():
        fetch(tab_ref, 0)

    @pl.when(i + 1 < n_steps)
    def _():
        fetch(tab_next_ref, 1 - slot)

    for u in range(tiles):
        _wait_runs(tab_ref[u, 0, TAB_UNITS], copy_into(slot, u))

    p = lax.broadcasted_iota(i32, (LOCAL_CAP, SORT_TILE), 0).astype(f32)
    parts = []
    for u in range(tiles):
        cs = slice(u * SORT_TILE, (u + 1) * SORT_TILE)
        pos0, pos1 = cmeta_ref[0:1, cs], cmeta_ref[1:2, cs]
        g = (jnp.where(p == pos0, cmeta_ref[2:3, cs], 0.0)
             + jnp.where(p == pos1, cmeta_ref[3:4, cs], 0.0))
        g_hi = g.astype(bf16)
        g_lo = (g - g_hi.astype(f32)).astype(bf16)
        y = _unpack_rows(ybuf[slot, pl.ds(u * LOCAL_CAP, LOCAL_CAP), :])
        both = _dot_tn(jnp.concatenate([g_hi, g_lo], axis=1), y)
        parts.append(x_ref[cs, :] + both[:SORT_TILE] + both[SORT_TILE:])
    return parts[0] if tiles == 1 else jnp.concatenate(parts, axis=0)


def _combine_specs(ts, n_steps):
    tiles = ts // SORT_TILE
    return [
        pl.BlockSpec((ts, D_MODEL), lambda i: (i, 0)),
        pl.BlockSpec((tiles, 1, TABLE_LEN), lambda i: (i, 0, 0), memory_space=pltpu.SMEM),
        pl.BlockSpec((tiles, 1, TABLE_LEN), lambda i: (jnp.minimum(i + 1, n_steps - 1), 0, 0),
                     memory_space=pltpu.SMEM),
        pl.BlockSpec((META_ROWS, ts), lambda i: (0, i)),
        pl.BlockSpec(memory_space=pl.ANY),
    ]


def _combine_scratch(ts):
    tiles = ts // SORT_TILE
    return [pltpu.VMEM((2, tiles * LOCAL_CAP, HALF), u32), pltpu.SemaphoreType.DMA((2,))]


def _gelu(z):
    return 0.5 * z * (1.0 + lax.erf(z * (2.0 ** -0.5)))


def _sgu_kernel(x_ref, ng_ref, wu_ref, wv_ref, lng_ref, lnb_ref,
                ws_ref, bs_ref, wo_ref, mng_ref, wrt_ref, br_ref,
                x3_ref, meta_ref, cnt_ref, vn_ref):
    ts = x_ref.shape[0]
    x2 = x_ref[...]
    h = (x2 * _rms(x2) * ng_ref[...]).astype(bf16)
    v = _gelu(_dot(h, wv_ref[...]))
    mu = jnp.mean(v, axis=-1, keepdims=True)
    vc = v - mu
    rstd = lax.rsqrt(jnp.mean(vc * vc, axis=-1, keepdims=True) + EPS)
    vn_ref[...] = (vc * rstd * lng_ref[...] + lnb_ref[...]).astype(bf16)

    pos = lax.broadcasted_iota(i32, (SGU_BLOCK, SGU_BLOCK), 0) // CHUNK
    src = lax.broadcasted_iota(i32, (SGU_BLOCK, SGU_BLOCK), 1) // CHUNK
    acc = x2
    for g in range(SGU_GROUPS):
        cs = slice(g * SGU_GC, (g + 1) * SGU_GC)
        ws = jnp.where(pos >= src, ws_ref[g], jnp.zeros((), bf16))
        u = _gelu(_dot(h, wu_ref[:, cs]))
        mixed = [_dot(ws, vn_ref[nb * SGU_BLOCK:(nb + 1) * SGU_BLOCK, cs]) + bs_ref[:, g:g + 1]
                 for nb in range(ts // SGU_BLOCK)]
        out = (u * jnp.concatenate(mixed, axis=0)).astype(bf16)
        acc = acc + _dot(out, wo_ref[cs, :])
    x3_ref[...] = acc
    _route(acc, mng_ref[...], wrt_ref[...], br_ref[...], meta_ref, cnt_ref)


def _sgu_layer(x2, norm_g, w_in, ln_g, ln_b, w_s, b_s, w_out, moe_norm_g, wrt, br, ts):
    t = x2.shape[0]
    n_steps = t // ts
    args = (norm_g.reshape(1, D_MODEL), w_in[:, :SGU_HALF].astype(bf16), w_in[:, SGU_HALF:].astype(bf16),
            ln_g.reshape(1, SGU_HALF), ln_b.reshape(1, SGU_HALF), w_s.astype(bf16), b_s.T,
            w_out.astype(bf16), moe_norm_g.reshape(1, D_MODEL), wrt, br)
    return pl.pallas_call(
        _sgu_kernel,
        grid=(n_steps,),
        in_specs=[pl.BlockSpec((ts, D_MODEL), lambda i: (i, 0))] + [_const_spec(a.shape) for a in args],
        out_specs=_route_out_specs(ts, lambda i: i),
        out_shape=_route_out_shapes(t),
        scratch_shapes=[pltpu.VMEM((ts, SGU_HALF), bf16)],
        compiler_params=pltpu.CompilerParams(
            dimension_semantics=("arbitrary",), vmem_limit_bytes=VMEM_LIMIT),
        name="sgu_mixer",
    )(x2, *args)


def _combine_kernel(x_ref, tab_ref, tab_next_ref, cmeta_ref, ys_hbm, out_ref, ybuf, sems):
    out_ref[...] = _combine(x_ref, tab_ref, tab_next_ref, cmeta_ref, ys_hbm, ybuf, sems)


def _final_kernel(x_ref, tab_ref, tab_next_ref, cmeta_ref, ys_hbm, ng_ref, out_ref, ybuf, sems):
    x = _combine(x_ref, tab_ref, tab_next_ref, cmeta_ref, ys_hbm, ybuf, sems)
    out_ref[...] = x * _rms(x) * ng_ref[...]


def _combine_layer(x, table, cmeta, ys, norm_g, ts):
    t = x.shape[0]
    n_steps = t // ts
    final = norm_g is not None
    extra = (norm_g.reshape(1, D_MODEL),) if final else ()
    return pl.pallas_call(
        _final_kernel if final else _combine_kernel,
        grid=(n_steps,),
        in_specs=_combine_specs(ts, n_steps) + [_const_spec(a.shape) for a in extra],
        out_specs=pl.BlockSpec((ts, D_MODEL), lambda i: (i, 0)),
        out_shape=jax.ShapeDtypeStruct((t, D_MODEL), f32),
        scratch_shapes=_combine_scratch(ts),
        compiler_params=pltpu.CompilerParams(
            dimension_semantics=("arbitrary",), vmem_limit_bytes=VMEM_LIMIT),
        name="final_norm" if final else "moe_combine",
    )(x, table, table, cmeta, ys, *extra)


def _moe(x, meta, cnt, norm_g, w1, w3, w2, layer, ts):
    table, off_col, items, ztable = _plan(cnt, x.shape[0])
    sorted_x, cmeta = _dispatch(x, meta, table, off_col, ztable, norm_g, ts)
    return _experts(sorted_x, *items, w1, w3, w2, layer), table, cmeta


def _forward(x, gla_norm, gla_w_in, gla_w_gate_up, gla_b_gate, gla_head_g, gla_w_out, sgu_norm, sgu_w_in,
             sgu_ln_g, sgu_ln_b, sgu_w_s, sgu_b_s, sgu_w_out, moe_norm, moe_w_group, moe_b_group,
             moe_w_sub, moe_b_sub, moe_w1, moe_w3, moe_w2, final_norm, *, ts_gla, ts_sgu, ts_fin):
    wrt0, br0 = _router_params(moe_w_group[0], moe_b_group[0], moe_w_sub[0], moe_b_sub[0])
    wrt1, br1 = _router_params(moe_w_group[1], moe_b_group[1], moe_w_sub[1], moe_b_sub[1])
    x1, meta0, cnt0 = _gla_layer(x, gla_norm[0], gla_w_in[0], gla_w_gate_up[0], gla_b_gate[0], gla_head_g[0],
                                 gla_w_out[0], moe_norm[0], wrt0, br0, ts_gla)
    ys0, table0, cmeta0 = _moe(x1, meta0, cnt0, moe_norm[0], moe_w1, moe_w3, moe_w2, 0, ts_fin)
    x2 = _combine_layer(x1, table0, cmeta0, ys0, None, ts_fin)
    x3, meta1, cnt1 = _sgu_layer(x2, sgu_norm[0], sgu_w_in[0], sgu_ln_g[0], sgu_ln_b[0],
                                 sgu_w_s[0], sgu_b_s[0], sgu_w_out[0], moe_norm[1], wrt1, br1, ts_sgu)
    ys1, table1, cmeta1 = _moe(x3, meta1, cnt1, moe_norm[1], moe_w1, moe_w3, moe_w2, 1, ts_fin)
    out = _combine_layer(x3, table1, cmeta1, ys1, final_norm, ts_fin)
    return out.reshape(x.shape)


def kernel(x, gla_norm, gla_w_in, gla_w_gate_up, gla_b_gate, gla_head_g, gla_w_out, sgu_norm, sgu_w_in, sgu_ln_g, sgu_ln_b, sgu_w_s, sgu_b_s, sgu_w_out, moe_norm, moe_w_group, moe_b_group, moe_w_sub, moe_b_sub, moe_w1, moe_w3, moe_w2, final_norm):
    return _forward(x, gla_norm, gla_w_in, gla_w_gate_up, gla_b_gate, gla_head_g, gla_w_out, sgu_norm,
                    sgu_w_in, sgu_ln_g, sgu_ln_b, sgu_w_s, sgu_b_s, sgu_w_out, moe_norm, moe_w_group,
                    moe_b_group, moe_w_sub, moe_b_sub, moe_w1, moe_w3, moe_w2, final_norm,
                    ts_gla=1024, ts_sgu=1024, ts_fin=1024)
```

```python
import jax
import jax.numpy as jnp
from jax import lax
from jax.experimental import pallas as pl
from jax.experimental.pallas import tpu as pltpu

D_MODEL = 1024
HALF = D_MODEL // 2
EPS = 1e-6
LANES = 128
SUBLANES = 8

CHUNK = 64
GLA_HEADS = 4
GLA_DK = 128
GLA_DV = 256
GLA_HK = GLA_HEADS * GLA_DK
GLA_HV = GLA_HEADS * GLA_DV
GLA_GATE_RANK = 16
GLA_TAU = 16.0

SGU_BLOCK = 128
SGU_GROUPS = 4
SGU_HALF = 2048
SGU_GC = SGU_HALF // SGU_GROUPS

N_GROUPS = 4
EXPERTS_PER_GROUP = 8
N_EXPERTS = N_GROUPS * EXPERTS_PER_GROUP
TOP_K = 2
EXPERT_FF = 512
ROUTE_ROWS = 64
META_ROWS = 8

SORT_TILE = 256
RUN_CHUNK = SUBLANES
LOCAL_CAP = 768
MAX_CHUNKS = LOCAL_CAP // RUN_CHUNK
EXPERT_ROWS = 256
ZERO_CHUNK = EXPERT_ROWS
MAX_PAIRS = MAX_CHUNKS // 2
TABLE_LEN = 256
TAB_PAIR_LOC, TAB_PAIR_GLB = 0, MAX_PAIRS
TAB_ODD_LOC, TAB_ODD_GLB = 2 * MAX_PAIRS, 2 * MAX_PAIRS + N_EXPERTS
TAB_PAIRS = 2 * MAX_PAIRS + 2 * N_EXPERTS
TAB_ODDS, TAB_UNITS = TAB_PAIRS + 1, TAB_PAIRS + 2
assert LOCAL_CAP >= TOP_K * SORT_TILE + N_EXPERTS * (RUN_CHUNK - 1)
assert TAB_UNITS < TABLE_LEN

VMEM_LIMIT = 56 * 1024 * 1024

f32 = jnp.float32
bf16 = jnp.bfloat16
i32 = jnp.int32
u32 = jnp.uint32


def _dot(a, b):
    return jnp.dot(a, b, preferred_element_type=f32)


def _dot_tn(a, b):
    return lax.dot_general(a, b, (((0,), (0,)), ((), ())), preferred_element_type=f32)


def _dot_nt(a, b):
    return lax.dot_general(a, b, (((1,), (1,)), ((), ())), preferred_element_type=f32)


def _rms(x):
    return lax.rsqrt(jnp.mean(x * x, axis=-1, keepdims=True) + EPS)


def _pack_rows(v):
    lo = lax.bitcast_convert_type(v[:, :HALF], u32)
    hi = lax.bitcast_convert_type(v[:, HALF:], u32)
    return lax.shift_right_logical(lo, jnp.uint32(16)) | (hi & jnp.uint32(0xFFFF0000))


def _unpack_rows(w):
    lo = lax.bitcast_convert_type(lax.shift_left(w, jnp.uint32(16)), f32)
    hi = lax.bitcast_convert_type(w & jnp.uint32(0xFFFF0000), f32)
    return jnp.concatenate([lo, hi], axis=-1).astype(bf16)


def _const_spec(shape):
    return pl.BlockSpec(shape, lambda *_: (0,) * len(shape))


def _route(x1, norm_g, wr_t, br, meta_ref, cnt_ref):
    n = x1.shape[0]
    hn = (x1 * _rms(x1) * norm_g).astype(bf16)
    lt = _dot_nt(wr_t, hn) + br
    rows = lax.broadcasted_iota(i32, (SUBLANES, n), 0)
    neg = jnp.float32(-jnp.inf)
    lg = jnp.where(rows < N_GROUPS, lt[0:SUBLANES], neg)
    gmax = jnp.max(lg, axis=0, keepdims=True)
    gidx = jnp.min(jnp.where(lg == gmax, rows, SUBLANES), axis=0, keepdims=True)
    g_w = 1.0 / jnp.sum(jnp.exp(lg - gmax), axis=0, keepdims=True)
    chosen = jnp.zeros((SUBLANES, n), f32)
    for g in range(N_GROUPS):
        chosen = jnp.where(gidx == g, lt[SUBLANES * (g + 1):SUBLANES * (g + 2)], chosen)
    m1 = jnp.max(chosen, axis=0, keepdims=True)
    i1 = jnp.min(jnp.where(chosen == m1, rows, SUBLANES), axis=0, keepdims=True)
    rest = jnp.where(rows == i1, neg, chosen)
    m2 = jnp.max(rest, axis=0, keepdims=True)
    i2 = jnp.min(jnp.where(rest == m2, rows, SUBLANES), axis=0, keepdims=True)
    t = jnp.exp(m2 - m1)
    s1 = 1.0 / (1.0 + t)
    s2 = t / (1.0 + t)
    e1 = gidx * EXPERTS_PER_GROUP + i1
    e2 = gidx * EXPERTS_PER_GROUP + i2
    zero = jnp.zeros((1, n), f32)
    meta_ref[...] = jnp.concatenate(
        [e1.astype(f32), e2.astype(f32), g_w * s1, g_w * s2, zero, zero, zero, zero], axis=0)
    ids = lax.broadcasted_iota(i32, (N_EXPERTS, n), 0)
    hits = (ids == e1).astype(f32) + (ids == e2).astype(f32)
    for u in range(n // SORT_TILE):
        cnt_ref[u * N_EXPERTS:(u + 1) * N_EXPERTS, :] = jnp.sum(
            hits[:, u * SORT_TILE:(u + 1) * SORT_TILE], axis=1, keepdims=True)


def _router_params(w_group, b_group, w_sub, b_sub):
    tail = ROUTE_ROWS - SUBLANES - N_EXPERTS
    wrt = jnp.concatenate([
        w_group.T, jnp.zeros((SUBLANES - N_GROUPS, D_MODEL), f32),
        jnp.transpose(w_sub, (0, 2, 1)).reshape(N_EXPERTS, D_MODEL),
        jnp.zeros((tail, D_MODEL), f32)], axis=0)
    br = jnp.concatenate([b_group, jnp.zeros((SUBLANES - N_GROUPS,), f32), b_sub.reshape(N_EXPERTS),
                          jnp.zeros((tail,), f32)]).reshape(ROUTE_ROWS, 1)
    return wrt.astype(bf16), br


def _route_out_specs(ts, index):
    tiles = ts // SORT_TILE
    return [pl.BlockSpec((ts, D_MODEL), lambda *g: (index(*g), 0)),
            pl.BlockSpec((META_ROWS, ts), lambda *g: (0, index(*g))),
            pl.BlockSpec((tiles * N_EXPERTS, 1), lambda *g: (index(*g), 0))]


def _route_out_shapes(t):
    return [jax.ShapeDtypeStruct((t, D_MODEL), f32),
            jax.ShapeDtypeStruct((META_ROWS, t), f32),
            jax.ShapeDtypeStruct((t // SORT_TILE * N_EXPERTS, 1), f32)]


def _gla_kernel(x_ref, ng_ref, wq_ref, wk_ref, wv_ref, wr_ref, wg_ref, wgu_ref, bg_ref,
                hg_ref, wo_ref, mng_ref, wrt_ref, br_ref,
                x1_ref, meta_ref, cnt_ref, st_ref, o_ref):
    ts = x_ref.shape[1]

    @pl.when(pl.program_id(1) == 0)
    def _():
        st_ref[...] = jnp.zeros_like(st_ref)

    x = x_ref[0]
    h = (x * _rms(x) * ng_ref[...]).astype(bf16)
    q = _dot(h, wq_ref[...]) * (GLA_DK ** -0.5)
    k = _dot(h, wk_ref[...])
    v = _dot(h, wv_ref[...]).astype(bf16)
    glr = _dot(h, wg_ref[...]).astype(bf16)
    gp = _dot(glr, wgu_ref[...]) + bg_ref[...]
    log_a = (jnp.minimum(gp, 0.0) - jnp.log(1.0 + jnp.exp(-jnp.abs(gp)))) * (1.0 / GLA_TAU)

    row = lax.broadcasted_iota(i32, (CHUNK, GLA_HK), 0)
    for c in range(ts // CHUNK):
        rs = slice(c * CHUNK, (c + 1) * CHUNK)
        b = log_a[rs]
        sh = 1
        while sh < CHUNK:
            b = b + jnp.where(row >= sh, pltpu.roll(b, sh, axis=0), 0.0)
            sh *= 2
        b_end = b[CHUNK - 1:CHUNK]
        kdec = (k[rs] * jnp.exp(b_end - b)).astype(bf16)
        decay = jnp.exp(b_end)
        qc = q[rs].astype(bf16)
        vc = v[rs]
        for hd in range(GLA_HEADS):
            ks = slice(hd * GLA_DK, (hd + 1) * GLA_DK)
            vs = slice(hd * GLA_DV, (hd + 1) * GLA_DV)
            st = st_ref[hd] * decay[:, ks] + _dot_tn(vc[:, vs], kdec[:, ks])
            st_ref[hd] = st
            o_ref[rs, vs] = _dot_nt(qc[:, ks], st.astype(bf16))

    r = _dot(h, wr_ref[...])
    gated = []
    for hd in range(GLA_HEADS):
        vs = slice(hd * GLA_DV, (hd + 1) * GLA_DV)
        oh = o_ref[:, vs]
        rh = r[:, vs]
        gated.append(oh * _rms(oh) * hg_ref[:, vs] * (rh / (1.0 + jnp.exp(-rh))))
    y = _dot(jnp.concatenate(gated, axis=-1).astype(bf16), wo_ref[...])
    x1 = x + y
    x1_ref[...] = x1
    _route(x1, mng_ref[...], wrt_ref[...], br_ref[...], meta_ref, cnt_ref)


def _gla_layer(x, norm_g, w_in, w_gate_up, b_gate, head_g, w_out, moe_norm_g, wrt, br, ts):
    bsz, seq, _ = x.shape
    t = bsz * seq
    wq = w_in[:, 0:GLA_HK].astype(bf16)
    wk = w_in[:, GLA_HK:2 * GLA_HK].astype(bf16)
    wv = w_in[:, 2 * GLA_HK:2 * GLA_HK + GLA_HV].astype(bf16)
    wr = w_in[:, 2 * GLA_HK + GLA_HV:2 * GLA_HK + 2 * GLA_HV].astype(bf16)
    wg = jnp.pad(w_in[:, 2 * GLA_HK + 2 * GLA_HV:], ((0, 0), (0, LANES - GLA_GATE_RANK))).astype(bf16)
    wgu = jnp.pad(w_gate_up, ((0, LANES - GLA_GATE_RANK), (0, 0))).astype(bf16)
    n_s = seq // ts
    args = (x, norm_g.reshape(1, D_MODEL), wq, wk, wv, wr, wg, wgu, b_gate.reshape(1, GLA_HK),
            head_g.reshape(1, GLA_HV), w_out.astype(bf16), moe_norm_g.reshape(1, D_MODEL), wrt, br)
    in_specs = [pl.BlockSpec((1, ts, D_MODEL), lambda b, s: (b, s, 0))]
    in_specs += [_const_spec(a.shape) for a in args[1:]]
    return pl.pallas_call(
        _gla_kernel,
        grid=(bsz, n_s),
        in_specs=in_specs,
        out_specs=_route_out_specs(ts, lambda b, s: b * n_s + s),
        out_shape=_route_out_shapes(t),
        scratch_shapes=[pltpu.VMEM((GLA_HEADS, GLA_DV, GLA_DK), f32),
                        pltpu.VMEM((ts, GLA_HV), f32)],
        compiler_params=pltpu.CompilerParams(
            dimension_semantics=("arbitrary", "arbitrary"), vmem_limit_bytes=VMEM_LIMIT),
        name="gla_mixer",
    )(*args)


def _n_expert_blocks(t):
    tiles = t // SORT_TILE
    worst = t * TOP_K + tiles * N_EXPERTS * (RUN_CHUNK - 1) + N_EXPERTS * (EXPERT_ROWS - 1)
    return -(-worst // EXPERT_ROWS)


def _zero_table_len(t):
    tail = _n_expert_blocks(t) * EXPERT_ROWS - t * TOP_K
    n = N_EXPERTS + -(-tail // ZERO_CHUNK) + 1
    return -(-(n + 1) // LANES) * LANES


def _cumsum(x, axis):
    x = jnp.moveaxis(x, axis, -1)
    n = x.shape[-1]
    upto = jnp.arange(n, dtype=i32)[:, None] <= jnp.arange(n, dtype=i32)[None, :]
    return jnp.moveaxis(jnp.sum(x[..., :, None] * upto.astype(i32), axis=-2), -1, axis)


def _flat_chunks(n_per, max_n):
    cum = _cumsum(n_per, -1)
    c = jnp.arange(max_n, dtype=i32)
    seg = jnp.minimum(jnp.sum((c[:, None] >= cum[..., None, :]).astype(i32), axis=-1), n_per.shape[-1] - 1)
    onehot = (seg[..., None] == jnp.arange(n_per.shape[-1], dtype=i32)).astype(i32)
    return onehot, c - _pick(onehot, cum - n_per), cum[..., -1]


def _pick(onehot, per_segment):
    return jnp.sum(onehot * per_segment[..., None, :], axis=-1)


def _plan(cnt, t):
    tiles = t // SORT_TILE
    cnt = cnt.reshape(tiles, N_EXPERTS).astype(i32)
    nch = (cnt + RUN_CHUNK - 1) // RUN_CHUNK
    run = nch * RUN_CHUNK
    counts = jnp.sum(run, axis=0)
    padded = ((counts + EXPERT_ROWS - 1) // EXPERT_ROWS) * EXPERT_ROWS
    pad_end = _cumsum(padded, 0)
    pad_start = pad_end - padded
    base = pad_start[None, :] + _cumsum(run, 0) - run
    n_pair = nch // 2
    off = _cumsum(run, 1) - run
    onehot, j, n_pairs = _flat_chunks(n_pair, MAX_PAIRS)
    pair_loc = _pick(onehot, off) + j * (2 * RUN_CHUNK)
    pair_glb = _pick(onehot, base) + j * (2 * RUN_CHUNK)
    onehot, _, n_odds = _flat_chunks(nch % 2, N_EXPERTS)
    odd_loc = _pick(onehot, off + n_pair * (2 * RUN_CHUNK))
    odd_glb = _pick(onehot, base + n_pair * (2 * RUN_CHUNK))
    units = jnp.sum(nch, axis=1)
    fill = jnp.zeros((tiles, TABLE_LEN - TAB_UNITS - 1), i32)
    table = jnp.concatenate([pair_loc, pair_glb, odd_loc, odd_glb, n_pairs[:, None], n_odds[:, None],
                             units[:, None], fill], axis=1).reshape(tiles, 1, TABLE_LEN)
    off_col = off.astype(f32).reshape(tiles, N_EXPERTS, 1)

    nb = _n_expert_blocks(t)
    n_blk = padded // EXPERT_ROWS
    n_big = n_blk // BIG_BLOCKS
    n_items = n_big + n_blk % BIG_BLOCKS
    ionehot, k, n_total = _flat_chunks(n_items, _n_work_items(t))
    k_big = _pick(ionehot, n_big)
    item_big = (k < k_big).astype(i32)
    item_start = _pick(ionehot, pad_start // EXPERT_ROWS) + jnp.where(
        k < k_big, BIG_BLOCKS * k, BIG_BLOCKS * k_big + k - k_big)
    icum = _cumsum(n_items, 0)
    item_first = jnp.concatenate([icum - n_items, n_total[None], pad_end[-1:] // EXPERT_ROWS])
    items = (item_first, item_start, item_big)

    region_end = jnp.concatenate([pad_start[1:], jnp.full((1,), nb * EXPERT_ROWS, i32)])
    zlen = region_end - (pad_start + counts)
    nz = (zlen + ZERO_CHUNK - 1) // ZERO_CHUNK
    zl = _zero_table_len(t)
    zonehot, zj, nztot = _flat_chunks(nz, zl - 1)
    zstart = _pick(zonehot, region_end) - (zj + 1) * ZERO_CHUNK
    ztable = jnp.concatenate([zstart, nztot[None]]).reshape(1, 1, zl)
    return table, off_col, items, ztable


ISSUE_UNROLL = 4


def _start_runs(tab, u, make):
    for rows, n, loc0, glb0 in ((2 * RUN_CHUNK, tab[u, 0, TAB_PAIRS], TAB_PAIR_LOC, TAB_PAIR_GLB),
                                (RUN_CHUNK, tab[u, 0, TAB_ODDS], TAB_ODD_LOC, TAB_ODD_GLB)):
        def start(c, priority):
            make(rows, tab[u, 0, loc0 + c], tab[u, 0, glb0 + c]).start(priority=priority)

        def four(k, carry):
            for q in range(ISSUE_UNROLL):
                start(ISSUE_UNROLL * k + q, q % 2)
            return carry
        lax.fori_loop(0, n // ISSUE_UNROLL, four, 0)
        lax.fori_loop(n - n % ISSUE_UNROLL, n, lambda c, carry: (start(c, 0), carry)[1], 0)


def _wait_runs(units, make):
    left = units
    for per_wait in (16, 2, 1):
        lax.fori_loop(0, left // per_wait,
                      lambda c, carry: (make(per_wait * RUN_CHUNK, 0, 0).wait(), carry)[1], 0)
        left = left % per_wait


def _dispatch_kernel(tab_ref, ztab_ref, x_ref, meta_ref, offc_ref, ng_ref,
                     out_hbm, cmeta_ref, stage, zbuf, sems, zsem, prev_n):
    t = pl.program_id(0)
    slot = t % 2
    zl = ztab_ref.shape[2]

    def zero_copy(z):
        dst = pl.multiple_of(ztab_ref[0, 0, z], RUN_CHUNK)
        return pltpu.make_async_copy(zbuf, out_hbm.at[pl.ds(dst, ZERO_CHUNK), :], zsem)

    @pl.when(t == 0)
    def _():
        zbuf[...] = jnp.zeros_like(zbuf)
        nz = ztab_ref[0, 0, zl - 1]
        lax.fori_loop(0, nz, lambda z, c: (zero_copy(z).start(), c)[1], 0)
        lax.fori_loop(0, nz, lambda z, c: (zero_copy(z).wait(), c)[1], 0)

    tiles = x_ref.shape[0] // SORT_TILE
    ids = lax.broadcasted_iota(i32, (N_EXPERTS, SORT_TILE), 0)
    before = (lax.broadcasted_iota(i32, (SORT_TILE, SORT_TILE), 0)
              < lax.broadcasted_iota(i32, (SORT_TILE, SORT_TILE), 1)).astype(bf16)
    p = lax.broadcasted_iota(i32, (LOCAL_CAP, SORT_TILE), 0).astype(f32)
    zero = jnp.zeros((1, SORT_TILE), f32)
    for u in range(tiles):
        cs = slice(u * SORT_TILE, (u + 1) * SORT_TILE)
        x = x_ref[cs, :]
        hn = (x * _rms(x) * ng_ref[...]).astype(bf16)
        oh0 = ids == meta_ref[0:1, cs].astype(i32)
        oh1 = ids == meta_ref[1:2, cs].astype(i32)
        c0 = _dot(oh0.astype(bf16), before)
        c1 = _dot(oh1.astype(bf16), before)
        n0 = jnp.sum(oh0.astype(f32), axis=1, keepdims=True)
        offc = offc_ref[u]
        pos0 = jnp.sum(jnp.where(oh0, offc + c0, 0.0), axis=0, keepdims=True)
        pos1 = jnp.sum(jnp.where(oh1, offc + n0 + c1, 0.0), axis=0, keepdims=True)
        cmeta_ref[:, cs] = jnp.concatenate(
            [pos0, pos1, meta_ref[2:3, cs], meta_ref[3:4, cs], zero, zero, zero, zero], axis=0)
        perm = jnp.logical_or(p == pos0, p == pos1).astype(bf16)
        stage[slot, pl.ds(u * LOCAL_CAP, LOCAL_CAP), :] = _pack_rows(_dot(perm, hn))

    def copy_from(s, u):
        def make(rows, loc, glb):
            src = pl.multiple_of(u * LOCAL_CAP + loc, RUN_CHUNK)
            return pltpu.make_async_copy(stage.at[s, pl.ds(src, rows), :],
                                         out_hbm.at[pl.ds(pl.multiple_of(glb, RUN_CHUNK), rows), :], sems.at[s])
        return make

    @pl.when(t > 0)
    def _():
        _wait_runs(prev_n[0], copy_from(1 - slot, 0))

    units = 0
    for u in range(tiles):
        _start_runs(tab_ref, u, copy_from(slot, u))
        units = units + tab_ref[u, 0, TAB_UNITS]
    prev_n[0] = units

    @pl.when(t == pl.num_programs(0) - 1)
    def _():
        _wait_runs(units, copy_from(slot, 0))


def _dispatch(x, meta, table, off_col, ztable, norm_g, ts):
    t = x.shape[0]
    tiles = ts // SORT_TILE
    zl = ztable.shape[2]
    nb = _n_expert_blocks(t)
    return pl.pallas_call(
        _dispatch_kernel,
        grid=(t // ts,),
        in_specs=[
            pl.BlockSpec((tiles, 1, TABLE_LEN), lambda i: (i, 0, 0), memory_space=pltpu.SMEM),
            pl.BlockSpec((1, 1, zl), lambda i: (0, 0, 0), memory_space=pltpu.SMEM),
            pl.BlockSpec((ts, D_MODEL), lambda i: (i, 0)),
            pl.BlockSpec((META_ROWS, ts), lambda i: (0, i)),
            pl.BlockSpec((tiles, N_EXPERTS, 1), lambda i: (i, 0, 0)),
            _const_spec((1, D_MODEL)),
        ],
        out_specs=[pl.BlockSpec(memory_space=pl.ANY),
                   pl.BlockSpec((META_ROWS, ts), lambda i: (0, i))],
        out_shape=[jax.ShapeDtypeStruct((nb * EXPERT_ROWS, HALF), u32),
                   jax.ShapeDtypeStruct((META_ROWS, t), f32)],
        scratch_shapes=[pltpu.VMEM((2, tiles * LOCAL_CAP, HALF), u32),
                        pltpu.VMEM((ZERO_CHUNK, HALF), u32),
                        pltpu.SemaphoreType.DMA((2,)),
                        pltpu.SemaphoreType.DMA(()),
                        pltpu.SMEM((1,), i32)],
        compiler_params=pltpu.CompilerParams(
            dimension_semantics=("arbitrary",), vmem_limit_bytes=VMEM_LIMIT),
        name="moe_dispatch",
    )(table, ztable, x, meta, off_col, norm_g.reshape(1, D_MODEL))


X_BUFFERS = 3
Y_BUFFERS = 2
BIG_BLOCKS = 4


def _n_work_items(t):
    return _n_expert_blocks(t) // BIG_BLOCKS + (BIG_BLOCKS - 1) * N_EXPERTS + 1


def _expert_kernel(ifirst_ref, istart_ref, ibig_ref, w1_ref, w3_ref, w2_ref, xs_hbm, ys_hbm,
                   xbuf, ybuf, xsem, ysem, w1b, w3b, w2b):
    e = pl.program_id(0)
    lo = ifirst_ref[e]
    hi = ifirst_ref[e + 1]
    total = ifirst_ref[N_EXPERTS]
    nvalid = ifirst_ref[N_EXPERTS + 1]
    n_blocks = ys_hbm.shape[0] // EXPERT_ROWS

    def rows_of(big):
        return (BIG_BLOCKS if big else 1) * EXPERT_ROWS

    def hbm_rows(ref, i, big):
        return ref.at[pl.ds(pl.multiple_of(istart_ref[i] * EXPERT_ROWS, EXPERT_ROWS), rows_of(big)), :]

    def x_copy(i, big):
        slot = i % X_BUFFERS
        return pltpu.make_async_copy(hbm_rows(xs_hbm, i, big),
                                     xbuf.at[slot, pl.ds(0, rows_of(big)), :], xsem.at[slot])

    def y_copy(i, big):
        slot = i % Y_BUFFERS
        return pltpu.make_async_copy(ybuf.at[slot, pl.ds(0, rows_of(big)), :],
                                     hbm_rows(ys_hbm, i, big), ysem.at[slot])

    def by_size(i, fn):
        @pl.when(ibig_ref[i] != 0)
        def _():
            fn(True)

        @pl.when(ibig_ref[i] == 0)
        def _():
            fn(False)

    @pl.when(e == 0)
    def _():
        for k in range(X_BUFFERS - 1):
            @pl.when(k < total)
            def _():
                by_size(k, lambda big: x_copy(k, big).start())

    @pl.when(hi > lo)
    def _():
        w1b[...] = w1_ref[0, 0].astype(bf16)
        w3b[...] = w3_ref[0, 0].astype(bf16)
        w2b[...] = w2_ref[0, 0].astype(bf16)

    def item(i, carry):
        ahead = i + X_BUFFERS - 1

        @pl.when(ahead < total)
        def _():
            by_size(ahead, lambda big: x_copy(ahead, big).start())

        def run(big):
            rows = rows_of(big)
            x_copy(i, big).wait()
            hn = _unpack_rows(xbuf[i % X_BUFFERS, pl.ds(0, rows), :])
            h1 = _dot(hn, w1b[...])
            h3 = _dot(hn, w3b[...])
            hid = ((h1 / (1.0 + jnp.exp(-h1))) * h3).astype(bf16)
            y = _dot(hid, w2b[...])

            @pl.when(i >= Y_BUFFERS)
            def _():
                by_size(i - Y_BUFFERS, lambda b: y_copy(i - Y_BUFFERS, b).wait())

            ybuf[i % Y_BUFFERS, pl.ds(0, rows), :] = _pack_rows(y.astype(bf16).astype(f32))
            y_copy(i, big).start()

        by_size(i, run)
        return carry

    lax.fori_loop(lo, hi, item, 0)

    @pl.when(e == pl.num_programs(0) - 1)
    def _():
        for k in range(1, Y_BUFFERS + 1):
            @pl.when(total >= k)
            def _():
                by_size(total - k, lambda big: y_copy(total - k, big).wait())
        ybuf[0, pl.ds(0, EXPERT_ROWS), :] = jnp.zeros((EXPERT_ROWS, HALF), u32)

        def tail(g):
            dst = pl.multiple_of(g * EXPERT_ROWS, EXPERT_ROWS)
            return pltpu.make_async_copy(ybuf.at[0, pl.ds(0, EXPERT_ROWS), :],
                                         ys_hbm.at[pl.ds(dst, EXPERT_ROWS), :], ysem.at[0])
        lax.fori_loop(nvalid, n_blocks, lambda g, c: (tail(g).start(), c)[1], 0)
        lax.fori_loop(nvalid, n_blocks, lambda g, c: (tail(g).wait(), c)[1], 0)


def _experts(xs, item_first, item_start, item_big, w1, w3, w2, layer):
    w_index = lambda e, *_: (layer, e, 0, 0)
    big_rows = BIG_BLOCKS * EXPERT_ROWS
    grid_spec = pltpu.PrefetchScalarGridSpec(
        num_scalar_prefetch=3,
        grid=(N_EXPERTS,),
        in_specs=[
            pl.BlockSpec((1, 1, D_MODEL, EXPERT_FF), w_index),
            pl.BlockSpec((1, 1, D_MODEL, EXPERT_FF), w_index),
            pl.BlockSpec((1, 1, EXPERT_FF, D_MODEL), w_index),
            pl.BlockSpec(memory_space=pl.ANY),
        ],
        out_specs=pl.BlockSpec(memory_space=pl.ANY),
        scratch_shapes=[pltpu.VMEM((X_BUFFERS, big_rows, HALF), u32),
                        pltpu.VMEM((Y_BUFFERS, big_rows, HALF), u32),
                        pltpu.SemaphoreType.DMA((X_BUFFERS,)),
                        pltpu.SemaphoreType.DMA((Y_BUFFERS,)),
                        pltpu.VMEM((D_MODEL, EXPERT_FF), bf16),
                        pltpu.VMEM((D_MODEL, EXPERT_FF), bf16),
                        pltpu.VMEM((EXPERT_FF, D_MODEL), bf16)],
    )
    return pl.pallas_call(
        _expert_kernel,
        grid_spec=grid_spec,
        out_shape=jax.ShapeDtypeStruct(xs.shape, u32),
        compiler_params=pltpu.CompilerParams(
            dimension_semantics=("arbitrary",), vmem_limit_bytes=VMEM_LIMIT),
        name="moe_experts",
    )(item_first, item_start, item_big, w1, w3, w2, xs)


def _combine(x_ref, tab_ref, tab_next_ref, cmeta_ref, ys_hbm, ybuf, sems):
    i = pl.program_id(0)
    n_steps = pl.num_programs(0)
    tiles = x_ref.shape[0] // SORT_TILE
    slot = i % 2

    def copy_into(s, u):
        def make(rows, loc, glb):
            dst = pl.multiple_of(u * LOCAL_CAP + loc, RUN_CHUNK)
            return pltpu.make_async_copy(ys_hbm.at[pl.ds(pl.multiple_of(glb, RUN_CHUNK), rows), :],
                                         ybuf.at[s, pl.ds(dst, rows), :], sems.at[s])
        return make

    def fetch(tab, s):
        for u in range(tiles):
            _start_runs(tab, u, copy_into(s, u))

            def clear(c, carry):
                dst = pl.multiple_of(u * LOCAL_CAP + c * RUN_CHUNK, RUN_CHUNK)
                ybuf[s, pl.ds(dst, RUN_CHUNK), :] = jnp.zeros((RUN_CHUNK, HALF), u32)
                return carry
            lax.fori_loop(tab[u, 0, TAB_UNITS], MAX_CHUNKS, clear, 0)

    @pl.when(i == 0)
    def _():
        fetch(tab_ref, 0)

    @pl.when(i + 1 < n_steps)
    def _():
        fetch(tab_next_ref, 1 - slot)

    for u in range(tiles):
        _wait_runs(tab_ref[u, 0, TAB_UNITS], copy_into(slot, u))

    p = lax.broadcasted_iota(i32, (LOCAL_CAP, SORT_TILE), 0).astype(f32)
    parts = []
    for u in range(tiles):
        cs = slice(u * SORT_TILE, (u + 1) * SORT_TILE)
        pos0, pos1 = cmeta_ref[0:1, cs], cmeta_ref[1:2, cs]
        g = (jnp.where(p == pos0, cmeta_ref[2:3, cs], 0.0)
             + jnp.where(p == pos1, cmeta_ref[3:4, cs], 0.0))
        g_hi = g.astype(bf16)
        g_lo = (g - g_hi.astype(f32)).astype(bf16)
        y = _unpack_rows(ybuf[slot, pl.ds(u * LOCAL_CAP, LOCAL_CAP), :])
        both = _dot_tn(jnp.concatenate([g_hi, g_lo], axis=1), y)
        parts.append(x_ref[cs, :] + both[:SORT_TILE] + both[SORT_TILE:])
    return parts[0] if tiles == 1 else jnp.concatenate(parts, axis=0)


def _combine_specs(ts, n_steps):
    tiles = ts // SORT_TILE
    return [
        pl.BlockSpec((ts, D_MODEL), lambda i: (i, 0)),
        pl.BlockSpec((tiles, 1, TABLE_LEN), lambda i: (i, 0, 0), memory_space=pltpu.SMEM),
        pl.BlockSpec((tiles, 1, TABLE_LEN), lambda i: (jnp.minimum(i + 1, n_steps - 1), 0, 0),
                     memory_space=pltpu.SMEM),
        pl.BlockSpec((META_ROWS, ts), lambda i: (0, i)),
        pl.BlockSpec(memory_space=pl.ANY),
    ]


def _combine_scratch(ts):
    tiles = ts // SORT_TILE
    return [pltpu.VMEM((2, tiles * LOCAL_CAP, HALF), u32), pltpu.SemaphoreType.DMA((2,))]


def _gelu(z):
    return 0.5 * z * (1.0 + lax.erf(z * (2.0 ** -0.5)))


def _sgu_kernel(x_ref, ng_ref, wu_ref, wv_ref, lng_ref, lnb_ref,
                ws_ref, bs_ref, wo_ref, mng_ref, wrt_ref, br_ref,
                x3_ref, meta_ref, cnt_ref, vn_ref):
    ts = x_ref.shape[0]
    x2 = x_ref[...]
    h = (x2 * _rms(x2) * ng_ref[...]).astype(bf16)
    v = _gelu(_dot(h, wv_ref[...]))
    mu = jnp.mean(v, axis=-1, keepdims=True)
    vc = v - mu
    rstd = lax.rsqrt(jnp.mean(vc * vc, axis=-1, keepdims=True) + EPS)
    vn_ref[...] = (vc * rstd * lng_ref[...] + lnb_ref[...]).astype(bf16)

    pos = lax.broadcasted_iota(i32, (SGU_BLOCK, SGU_BLOCK), 0) // CHUNK
    src = lax.broadcasted_iota(i32, (SGU_BLOCK, SGU_BLOCK), 1) // CHUNK
    acc = x2
    for g in range(SGU_GROUPS):
        cs = slice(g * SGU_GC, (g + 1) * SGU_GC)
        ws = jnp.where(pos >= src, ws_ref[g], jnp.zeros((), bf16))
        u = _gelu(_dot(h, wu_ref[:, cs]))
        mixed = [_dot(ws, vn_ref[nb * SGU_BLOCK:(nb + 1) * SGU_BLOCK, cs]) + bs_ref[:, g:g + 1]
                 for nb in range(ts // SGU_BLOCK)]
        out = (u * jnp.concatenate(mixed, axis=0)).astype(bf16)
        acc = acc + _dot(out, wo_ref[cs, :])
    x3_ref[...] = acc
    _route(acc, mng_ref[...], wrt_ref[...], br_ref[...], meta_ref, cnt_ref)


def _sgu_layer(x2, norm_g, w_in, ln_g, ln_b, w_s, b_s, w_out, moe_norm_g, wrt, br, ts):
    t = x2.shape[0]
    n_steps = t // ts
    args = (norm_g.reshape(1, D_MODEL), w_in[:, :SGU_HALF].astype(bf16), w_in[:, SGU_HALF:].astype(bf16),
            ln_g.reshape(1, SGU_HALF), ln_b.reshape(1, SGU_HALF), w_s.astype(bf16), b_s.T,
            w_out.astype(bf16), moe_norm_g.reshape(1, D_MODEL), wrt, br)
    return pl.pallas_call(
        _sgu_kernel,
        grid=(n_steps,),
        in_specs=[pl.BlockSpec((ts, D_MODEL), lambda i: (i, 0))] + [_const_spec(a.shape) for a in args],
        out_specs=_route_out_specs(ts, lambda i: i),
        out_shape=_route_out_shapes(t),
        scratch_shapes=[pltpu.VMEM((ts, SGU_HALF), bf16)],
        compiler_params=pltpu.CompilerParams(
            dimension_semantics=("arbitrary",), vmem_limit_bytes=VMEM_LIMIT),
        name="sgu_mixer",
    )(x2, *args)


def _combine_kernel(x_ref, tab_ref, tab_next_ref, cmeta_ref, ys_hbm, out_ref, ybuf, sems):
    out_ref[...] = _combine(x_ref, tab_ref, tab_next_ref, cmeta_ref, ys_hbm, ybuf, sems)


def _final_kernel(x_ref, tab_ref, tab_next_ref, cmeta_ref, ys_hbm, ng_ref, out_ref, ybuf, sems):
    x = _combine(x_ref, tab_ref, tab_next_ref, cmeta_ref, ys_hbm, ybuf, sems)
    out_ref[...] = x * _rms(x) * ng_ref[...]


def _combine_layer(x, table, cmeta, ys, norm_g, ts):
    t = x.shape[0]
    n_steps = t // ts
    final = norm_g is not None
    extra = (norm_g.reshape(1, D_MODEL),) if final else ()
    return pl.pallas_call(
        _final_kernel if final else _combine_kernel,
        grid=(n_steps,),
        in_specs=_combine_specs(ts, n_steps) + [_const_spec(a.shape) for a in extra],
        out_specs=pl.BlockSpec((ts, D_MODEL), lambda i: (i, 0)),
        out_shape=jax.ShapeDtypeStruct((t, D_MODEL), f32),
        scratch_shapes=_combine_scratch(ts),
        compiler_params=pltpu.CompilerParams(
            dimension_semantics=("arbitrary",), vmem_limit_bytes=VMEM_LIMIT),
        name="final_norm" if final else "moe_combine",
    )(x, table, table, cmeta, ys, *extra)


def _moe(x, meta, cnt, norm_g, w1, w3, w2, layer, ts):
    table, off_col, items, ztable = _plan(cnt, x.shape[0])
    sorted_x, cmeta = _dispatch(x, meta, table, off_col, ztable, norm_g, ts)
    return _experts(sorted_x, *items, w1, w3, w2, layer), table, cmeta


def _forward(x, gla_norm, gla_w_in, gla_w_gate_up, gla_b_gate, gla_head_g, gla_w_out, sgu_norm, sgu_w_in,
             sgu_ln_g, sgu_ln_b, sgu_w_s, sgu_b_s, sgu_w_out, moe_norm, moe_w_group, moe_b_group,
             moe_w_sub, moe_b_sub, moe_w1, moe_w3, moe_w2, final_norm, *, ts_gla, ts_sgu, ts_fin):
    wrt0, br0 = _router_params(moe_w_group[0], moe_b_group[0], moe_w_sub[0], moe_b_sub[0])
    wrt1, br1 = _router_params(moe_w_group[1], moe_b_group[1], moe_w_sub[1], moe_b_sub[1])
    x1, meta0, cnt0 = _gla_layer(x, gla_norm[0], gla_w_in[0], gla_w_gate_up[0], gla_b_gate[0], gla_head_g[0],
                                 gla_w_out[0], moe_norm[0], wrt0, br0, ts_gla)
    ys0, table0, cmeta0 = _moe(x1, meta0, cnt0, moe_norm[0], moe_w1, moe_w3, moe_w2, 0, ts_fin)
    x2 = _combine_layer(x1, table0, cmeta0, ys0, None, ts_fin)
    x3, meta1, cnt1 = _sgu_layer(x2, sgu_norm[0], sgu_w_in[0], sgu_ln_g[0], sgu_ln_b[0],
                                 sgu_w_s[0], sgu_b_s[0], sgu_w_out[0], moe_norm[1], wrt1, br1, ts_sgu)
    ys1, table1, cmeta1 = _moe(x3, meta1, cnt1, moe_norm[1], moe_w1, moe_w3, moe_w2, 1, ts_fin)
    out = _combine_layer(x3, table1, cmeta1, ys1, final_norm, ts_fin)
    return out.reshape(x.shape)


def kernel(x, gla_norm, gla_w_in, gla_w_gate_up, gla_b_gate, gla_head_g, gla_w_out, sgu_norm, sgu_w_in, sgu_ln_g, sgu_ln_b, sgu_w_s, sgu_b_s, sgu_w_out, moe_norm, moe_w_group, moe_b_group, moe_w_sub, moe_b_sub, moe_w1, moe_w3, moe_w2, final_norm):
    return _forward(x, gla_norm, gla_w_in, gla_w_gate_up, gla_b_gate, gla_head_g, gla_w_out, sgu_norm,
                    sgu_w_in, sgu_ln_g, sgu_ln_b, sgu_w_s, sgu_b_s, sgu_w_out, moe_norm, moe_w_group,
                    moe_b_group, moe_w_sub, moe_b_sub, moe_w1, moe_w3, moe_w2, final_norm,
                    ts_gla=1024, ts_sgu=1024, ts_fin=1024)
```

```python
import jax
import jax.numpy as jnp
from jax import lax
from jax.experimental import pallas as pl
from jax.experimental.pallas import tpu as pltpu

D_MODEL = 1024
HALF = D_MODEL // 2
EPS = 1e-6
LANES = 128
SUBLANES = 8

CHUNK = 64
GLA_HEADS = 4
GLA_DK = 128
GLA_DV = 256
GLA_HK = GLA_HEADS * GLA_DK
GLA_HV = GLA_HEADS * GLA_DV
GLA_GATE_RANK = 16
GLA_TAU = 16.0

SGU_BLOCK = 128
SGU_GROUPS = 4
SGU_HALF = 2048
SGU_GC = SGU_HALF // SGU_GROUPS

N_GROUPS = 4
EXPERTS_PER_GROUP = 8
N_EXPERTS = N_GROUPS * EXPERTS_PER_GROUP
TOP_K = 2
EXPERT_FF = 512
ROUTE_ROWS = 64
META_ROWS = 8

SORT_TILE = 256
RUN_CHUNK = SUBLANES
LOCAL_CAP = 768
MAX_CHUNKS = LOCAL_CAP // RUN_CHUNK
EXPERT_ROWS = 256
ZERO_CHUNK = EXPERT_ROWS
MAX_PAIRS = MAX_CHUNKS // 2
TABLE_LEN = 256
TAB_PAIR_LOC, TAB_PAIR_GLB = 0, MAX_PAIRS
TAB_ODD_LOC, TAB_ODD_GLB = 2 * MAX_PAIRS, 2 * MAX_PAIRS + N_EXPERTS
TAB_PAIRS = 2 * MAX_PAIRS + 2 * N_EXPERTS
TAB_ODDS, TAB_UNITS = TAB_PAIRS + 1, TAB_PAIRS + 2
assert LOCAL_CAP >= TOP_K * SORT_TILE + N_EXPERTS * (RUN_CHUNK - 1)
assert TAB_UNITS < TABLE_LEN

VMEM_LIMIT = 56 * 1024 * 1024

f32 = jnp.float32
bf16 = jnp.bfloat16
i32 = jnp.int32
u32 = jnp.uint32


def _dot(a, b):
    return jnp.dot(a, b, preferred_element_type=f32)


def _dot_tn(a, b):
    return lax.dot_general(a, b, (((0,), (0,)), ((), ())), preferred_element_type=f32)


def _dot_nt(a, b):
    return lax.dot_general(a, b, (((1,), (1,)), ((), ())), preferred_element_type=f32)


def _rms(x):
    return lax.rsqrt(jnp.mean(x * x, axis=-1, keepdims=True) + EPS)


def _pack_rows(v):
    lo = lax.bitcast_convert_type(v[:, :HALF], u32)
    hi = lax.bitcast_convert_type(v[:, HALF:], u32)
    return lax.shift_right_logical(lo, jnp.uint32(16)) | (hi & jnp.uint32(0xFFFF0000))


def _unpack_rows(w):
    lo = lax.bitcast_convert_type(lax.shift_left(w, jnp.uint32(16)), f32)
    hi = lax.bitcast_convert_type(w & jnp.uint32(0xFFFF0000), f32)
    return jnp.concatenate([lo, hi], axis=-1).astype(bf16)


def _const_spec(shape):
    return pl.BlockSpec(shape, lambda *_: (0,) * len(shape))


def _route(x1, norm_g, wr_t, br, meta_ref, cnt_ref):
    n = x1.shape[0]
    hn = (x1 * _rms(x1) * norm_g).astype(bf16)
    lt = _dot_nt(wr_t, hn) + br
    rows = lax.broadcasted_iota(i32, (SUBLANES, n), 0)
    neg = jnp.float32(-jnp.inf)
    lg = jnp.where(rows < N_GROUPS, lt[0:SUBLANES], neg)
    gmax = jnp.max(lg, axis=0, keepdims=True)
    gidx = jnp.min(jnp.where(lg == gmax, rows, SUBLANES), axis=0, keepdims=True)
    g_w = 1.0 / jnp.sum(jnp.exp(lg - gmax), axis=0, keepdims=True)
    chosen = jnp.zeros((SUBLANES, n), f32)
    for g in range(N_GROUPS):
        chosen = jnp.where(gidx == g, lt[SUBLANES * (g + 1):SUBLANES * (g + 2)], chosen)
    m1 = jnp.max(chosen, axis=0, keepdims=True)
    i1 = jnp.min(jnp.where(chosen == m1, rows, SUBLANES), axis=0, keepdims=True)
    rest = jnp.where(rows == i1, neg, chosen)
    m2 = jnp.max(rest, axis=0, keepdims=True)
    i2 = jnp.min(jnp.where(rest == m2, rows, SUBLANES), axis=0, keepdims=True)
    t = jnp.exp(m2 - m1)
    s1 = 1.0 / (1.0 + t)
    s2 = t / (1.0 + t)
    e1 = gidx * EXPERTS_PER_GROUP + i1
    e2 = gidx * EXPERTS_PER_GROUP + i2
    zero = jnp.zeros((1, n), f32)
    meta_ref[...] = jnp.concatenate(
        [e1.astype(f32), e2.astype(f32), g_w * s1, g_w * s2, zero, zero, zero, zero], axis=0)
    ids = lax.broadcasted_iota(i32, (N_EXPERTS, n), 0)
    hits = (ids == e1).astype(f32) + (ids == e2).astype(f32)
    for u in range(n // SORT_TILE):
        cnt_ref[u * N_EXPERTS:(u + 1) * N_EXPERTS, :] = jnp.sum(
            hits[:, u * SORT_TILE:(u + 1) * SORT_TILE], axis=1, keepdims=True)


def _router_params(w_group, b_group, w_sub, b_sub):
    tail = ROUTE_ROWS - SUBLANES - N_EXPERTS
    wrt = jnp.concatenate([
        w_group.T, jnp.zeros((SUBLANES - N_GROUPS, D_MODEL), f32),
        jnp.transpose(w_sub, (0, 2, 1)).reshape(N_EXPERTS, D_MODEL),
        jnp.zeros((tail, D_MODEL), f32)], axis=0)
    br = jnp.concatenate([b_group, jnp.zeros((SUBLANES - N_GROUPS,), f32), b_sub.reshape(N_EXPERTS),
                          jnp.zeros((tail,), f32)]).reshape(ROUTE_ROWS, 1)
    return wrt.astype(bf16), br


def _route_out_specs(ts, index):
    tiles = ts // SORT_TILE
    return [pl.BlockSpec((ts, D_MODEL), lambda *g: (index(*g), 0)),
            pl.BlockSpec((META_ROWS, ts), lambda *g: (0, index(*g))),
            pl.BlockSpec((tiles * N_EXPERTS, 1), lambda *g: (index(*g), 0))]


def _route_out_shapes(t):
    return [jax.ShapeDtypeStruct((t, D_MODEL), f32),
            jax.ShapeDtypeStruct((META_ROWS, t), f32),
            jax.ShapeDtypeStruct((t // SORT_TILE * N_EXPERTS, 1), f32)]


def _gla_kernel(x_ref, ng_ref, wq_ref, wk_ref, wv_ref, wr_ref, wg_ref, wgu_ref, bg_ref,
                hg_ref, wo_ref, mng_ref, wrt_ref, br_ref,
                x1_ref, meta_ref, cnt_ref, st_ref, o_ref):
    ts = x_ref.shape[1]

    @pl.when(pl.program_id(1) == 0)
    def _():
        st_ref[...] = jnp.zeros_like(st_ref)

    x = x_ref[0]
    h = (x * _rms(x) * ng_ref[...]).astype(bf16)
    q = _dot(h, wq_ref[...]) * (GLA_DK ** -0.5)
    k = _dot(h, wk_ref[...])
    v = _dot(h, wv_ref[...]).astype(bf16)
    glr = _dot(h, wg_ref[...]).astype(bf16)
    r = _dot(h, wr_ref[...])
    gp = _dot(glr, wgu_ref[...]) + bg_ref[...]
    log_a = (jnp.minimum(gp, 0.0) - jnp.log(1.0 + jnp.exp(-jnp.abs(gp)))) * (1.0 / GLA_TAU)

    row = lax.broadcasted_iota(i32, (CHUNK, GLA_HK), 0)
    for c in range(ts // CHUNK):
        rs = slice(c * CHUNK, (c + 1) * CHUNK)
        b = log_a[rs]
        sh = 1
        while sh < CHUNK:
            b = b + jnp.where(row >= sh, pltpu.roll(b, sh, axis=0), 0.0)
            sh *= 2
        b_end = b[CHUNK - 1:CHUNK]
        kdec = (k[rs] * jnp.exp(b_end - b)).astype(bf16)
        decay = jnp.exp(b_end)
        qc = q[rs].astype(bf16)
        vc = v[rs]
        for hd in range(GLA_HEADS):
            ks = slice(hd * GLA_DK, (hd + 1) * GLA_DK)
            vs = slice(hd * GLA_DV, (hd + 1) * GLA_DV)
            st = st_ref[hd] * decay[:, ks] + _dot_tn(vc[:, vs], kdec[:, ks])
            st_ref[hd] = st
            o_ref[rs, vs] = _dot_nt(qc[:, ks], st.astype(bf16))

    gated = []
    for hd in range(GLA_HEADS):
        vs = slice(hd * GLA_DV, (hd + 1) * GLA_DV)
        oh = o_ref[:, vs]
        rh = r[:, vs]
        gated.append(oh * _rms(oh) * hg_ref[:, vs] * (rh / (1.0 + jnp.exp(-rh))))
    y = _dot(jnp.concatenate(gated, axis=-1).astype(bf16), wo_ref[...])
    x1 = x + y
    x1_ref[...] = x1
    _route(x1, mng_ref[...], wrt_ref[...], br_ref[...], meta_ref, cnt_ref)


def _gla_layer(x, norm_g, w_in, w_gate_up, b_gate, head_g, w_out, moe_norm_g, wrt, br, ts):
    bsz, seq, _ = x.shape
    t = bsz * seq
    wq = w_in[:, 0:GLA_HK].astype(bf16)
    wk = w_in[:, GLA_HK:2 * GLA_HK].astype(bf16)
    wv = w_in[:, 2 * GLA_HK:2 * GLA_HK + GLA_HV].astype(bf16)
    wr = w_in[:, 2 * GLA_HK + GLA_HV:2 * GLA_HK + 2 * GLA_HV].astype(bf16)
    wg = jnp.pad(w_in[:, 2 * GLA_HK + 2 * GLA_HV:], ((0, 0), (0, LANES - GLA_GATE_RANK))).astype(bf16)
    wgu = jnp.pad(w_gate_up, ((0, LANES - GLA_GATE_RANK), (0, 0))).astype(bf16)
    n_s = seq // ts
    args = (x, norm_g.reshape(1, D_MODEL), wq, wk, wv, wr, wg, wgu, b_gate.reshape(1, GLA_HK),
            head_g.reshape(1, GLA_HV), w_out.astype(bf16), moe_norm_g.reshape(1, D_MODEL), wrt, br)
    in_specs = [pl.BlockSpec((1, ts, D_MODEL), lambda b, s: (b, s, 0))]
    in_specs += [_const_spec(a.shape) for a in args[1:]]
    return pl.pallas_call(
        _gla_kernel,
        grid=(bsz, n_s),
        in_specs=in_specs,
        out_specs=_route_out_specs(ts, lambda b, s: b * n_s + s),
        out_shape=_route_out_shapes(t),
        scratch_shapes=[pltpu.VMEM((GLA_HEADS, GLA_DV, GLA_DK), f32),
                        pltpu.VMEM((ts, GLA_HV), f32)],
        compiler_params=pltpu.CompilerParams(
            dimension_semantics=("arbitrary", "arbitrary"), vmem_limit_bytes=VMEM_LIMIT),
        name="gla_mixer",
    )(*args)


def _n_expert_blocks(t):
    tiles = t // SORT_TILE
    worst = t * TOP_K + tiles * N_EXPERTS * (RUN_CHUNK - 1) + N_EXPERTS * (EXPERT_ROWS - 1)
    return -(-worst // EXPERT_ROWS)


def _zero_table_len(t):
    tail = _n_expert_blocks(t) * EXPERT_ROWS - t * TOP_K
    n = N_EXPERTS + -(-tail // ZERO_CHUNK) + 1
    return -(-(n + 1) // LANES) * LANES


def _cumsum(x, axis):
    x = jnp.moveaxis(x, axis, -1)
    n = x.shape[-1]
    upto = jnp.arange(n, dtype=i32)[:, None] <= jnp.arange(n, dtype=i32)[None, :]
    return jnp.moveaxis(jnp.sum(x[..., :, None] * upto.astype(i32), axis=-2), -1, axis)


def _flat_chunks(n_per, max_n):
    cum = _cumsum(n_per, -1)
    c = jnp.arange(max_n, dtype=i32)
    seg = jnp.minimum(jnp.sum((c[:, None] >= cum[..., None, :]).astype(i32), axis=-1), n_per.shape[-1] - 1)
    onehot = (seg[..., None] == jnp.arange(n_per.shape[-1], dtype=i32)).astype(i32)
    return onehot, c - _pick(onehot, cum - n_per), cum[..., -1]


def _pick(onehot, per_segment):
    return jnp.sum(onehot * per_segment[..., None, :], axis=-1)


def _plan(cnt, t):
    tiles = t // SORT_TILE
    cnt = cnt.reshape(tiles, N_EXPERTS).astype(i32)
    nch = (cnt + RUN_CHUNK - 1) // RUN_CHUNK
    run = nch * RUN_CHUNK
    counts = jnp.sum(run, axis=0)
    padded = ((counts + EXPERT_ROWS - 1) // EXPERT_ROWS) * EXPERT_ROWS
    pad_end = _cumsum(padded, 0)
    pad_start = pad_end - padded
    base = pad_start[None, :] + _cumsum(run, 0) - run
    n_pair = nch // 2
    off = _cumsum(run, 1) - run
    onehot, j, n_pairs = _flat_chunks(n_pair, MAX_PAIRS)
    pair_loc = _pick(onehot, off) + j * (2 * RUN_CHUNK)
    pair_glb = _pick(onehot, base) + j * (2 * RUN_CHUNK)
    onehot, _, n_odds = _flat_chunks(nch % 2, N_EXPERTS)
    odd_loc = _pick(onehot, off + n_pair * (2 * RUN_CHUNK))
    odd_glb = _pick(onehot, base + n_pair * (2 * RUN_CHUNK))
    units = jnp.sum(nch, axis=1)
    fill = jnp.zeros((tiles, TABLE_LEN - TAB_UNITS - 1), i32)
    table = jnp.concatenate([pair_loc, pair_glb, odd_loc, odd_glb, n_pairs[:, None], n_odds[:, None],
                             units[:, None], fill], axis=1).reshape(tiles, 1, TABLE_LEN)
    off_col = off.astype(f32).reshape(tiles, N_EXPERTS, 1)

    nb = _n_expert_blocks(t)
    n_blk = padded // EXPERT_ROWS
    n_big = n_blk // BIG_BLOCKS
    n_items = n_big + n_blk % BIG_BLOCKS
    ionehot, k, n_total = _flat_chunks(n_items, _n_work_items(t))
    k_big = _pick(ionehot, n_big)
    item_big = (k < k_big).astype(i32)
    item_start = _pick(ionehot, pad_start // EXPERT_ROWS) + jnp.where(
        k < k_big, BIG_BLOCKS * k, BIG_BLOCKS * k_big + k - k_big)
    icum = _cumsum(n_items, 0)
    item_first = jnp.concatenate([icum - n_items, n_total[None], pad_end[-1:] // EXPERT_ROWS])
    items = (item_first, item_start, item_big)

    region_end = jnp.concatenate([pad_start[1:], jnp.full((1,), nb * EXPERT_ROWS, i32)])
    zlen = region_end - (pad_start + counts)
    nz = (zlen + ZERO_CHUNK - 1) // ZERO_CHUNK
    zl = _zero_table_len(t)
    zonehot, zj, nztot = _flat_chunks(nz, zl - 1)
    zstart = _pick(zonehot, region_end) - (zj + 1) * ZERO_CHUNK
    ztable = jnp.concatenate([zstart, nztot[None]]).reshape(1, 1, zl)
    return table, off_col, items, ztable


ISSUE_UNROLL = 4


def _start_runs(tab, u, make):
    for rows, n, loc0, glb0 in ((2 * RUN_CHUNK, tab[u, 0, TAB_PAIRS], TAB_PAIR_LOC, TAB_PAIR_GLB),
                                (RUN_CHUNK, tab[u, 0, TAB_ODDS], TAB_ODD_LOC, TAB_ODD_GLB)):
        def start(c, priority):
            make(rows, tab[u, 0, loc0 + c], tab[u, 0, glb0 + c]).start(priority=priority)

        def four(k, carry):
            for q in range(ISSUE_UNROLL):
                start(ISSUE_UNROLL * k + q, q % 2)
            return carry
        lax.fori_loop(0, n // ISSUE_UNROLL, four, 0)
        lax.fori_loop(n - n % ISSUE_UNROLL, n, lambda c, carry: (start(c, 0), carry)[1], 0)


def _wait_runs(units, make):
    left = units
    for per_wait in (16, 2, 1):
        lax.fori_loop(0, left // per_wait,
                      lambda c, carry: (make(per_wait * RUN_CHUNK, 0, 0).wait(), carry)[1], 0)
        left = left % per_wait


def _dispatch_kernel(tab_ref, ztab_ref, x_ref, meta_ref, offc_ref, ng_ref,
                     out_hbm, cmeta_ref, stage, zbuf, sems, zsem, prev_n):
    t = pl.program_id(0)
    slot = t % 2
    zl = ztab_ref.shape[2]

    def zero_copy(z):
        dst = pl.multiple_of(ztab_ref[0, 0, z], RUN_CHUNK)
        return pltpu.make_async_copy(zbuf, out_hbm.at[pl.ds(dst, ZERO_CHUNK), :], zsem)

    @pl.when(t == 0)
    def _():
        zbuf[...] = jnp.zeros_like(zbuf)
        nz = ztab_ref[0, 0, zl - 1]
        lax.fori_loop(0, nz, lambda z, c: (zero_copy(z).start(), c)[1], 0)
        lax.fori_loop(0, nz, lambda z, c: (zero_copy(z).wait(), c)[1], 0)

    tiles = x_ref.shape[0] // SORT_TILE
    ids = lax.broadcasted_iota(i32, (N_EXPERTS, SORT_TILE), 0)
    before = (lax.broadcasted_iota(i32, (SORT_TILE, SORT_TILE), 0)
              < lax.broadcasted_iota(i32, (SORT_TILE, SORT_TILE), 1)).astype(bf16)
    p = lax.broadcasted_iota(i32, (LOCAL_CAP, SORT_TILE), 0).astype(f32)
    zero = jnp.zeros((1, SORT_TILE), f32)
    for u in range(tiles):
        cs = slice(u * SORT_TILE, (u + 1) * SORT_TILE)
        x = x_ref[cs, :]
        hn = (x * _rms(x) * ng_ref[...]).astype(bf16)
        oh0 = ids == meta_ref[0:1, cs].astype(i32)
        oh1 = ids == meta_ref[1:2, cs].astype(i32)
        c0 = _dot(oh0.astype(bf16), before)
        c1 = _dot(oh1.astype(bf16), before)
        n0 = jnp.sum(oh0.astype(f32), axis=1, keepdims=True)
        offc = offc_ref[u]
        pos0 = jnp.sum(jnp.where(oh0, offc + c0, 0.0), axis=0, keepdims=True)
        pos1 = jnp.sum(jnp.where(oh1, offc + n0 + c1, 0.0), axis=0, keepdims=True)
        cmeta_ref[:, cs] = jnp.concatenate(
            [pos0, pos1, meta_ref[2:3, cs], meta_ref[3:4, cs], zero, zero, zero, zero], axis=0)
        perm = jnp.logical_or(p == pos0, p == pos1).astype(bf16)
        stage[slot, pl.ds(u * LOCAL_CAP, LOCAL_CAP), :] = _pack_rows(_dot(perm, hn))

    def copy_from(s, u):
        def make(rows, loc, glb):
            src = pl.multiple_of(u * LOCAL_CAP + loc, RUN_CHUNK)
            return pltpu.make_async_copy(stage.at[s, pl.ds(src, rows), :],
                                         out_hbm.at[pl.ds(pl.multiple_of(glb, RUN_CHUNK), rows), :], sems.at[s])
        return make

    @pl.when(t > 0)
    def _():
        _wait_runs(prev_n[0], copy_from(1 - slot, 0))

    units = 0
    for u in range(tiles):
        _start_runs(tab_ref, u, copy_from(slot, u))
        units = units + tab_ref[u, 0, TAB_UNITS]
    prev_n[0] = units

    @pl.when(t == pl.num_programs(0) - 1)
    def _():
        _wait_runs(units, copy_from(slot, 0))


def _dispatch(x, meta, table, off_col, ztable, norm_g, ts):
    t = x.shape[0]
    tiles = ts // SORT_TILE
    zl = ztable.shape[2]
    nb = _n_expert_blocks(t)
    return pl.pallas_call(
        _dispatch_kernel,
        grid=(t // ts,),
        in_specs=[
            pl.BlockSpec((tiles, 1, TABLE_LEN), lambda i: (i, 0, 0), memory_space=pltpu.SMEM),
            pl.BlockSpec((1, 1, zl), lambda i: (0, 0, 0), memory_space=pltpu.SMEM),
            pl.BlockSpec((ts, D_MODEL), lambda i: (i, 0)),
            pl.BlockSpec((META_ROWS, ts), lambda i: (0, i)),
            pl.BlockSpec((tiles, N_EXPERTS, 1), lambda i: (i, 0, 0)),
            _const_spec((1, D_MODEL)),
        ],
        out_specs=[pl.BlockSpec(memory_space=pl.ANY),
                   pl.BlockSpec((META_ROWS, ts), lambda i: (0, i))],
        out_shape=[jax.ShapeDtypeStruct((nb * EXPERT_ROWS, HALF), u32),
                   jax.ShapeDtypeStruct((META_ROWS, t), f32)],
        scratch_shapes=[pltpu.VMEM((2, tiles * LOCAL_CAP, HALF), u32),
                        pltpu.VMEM((ZERO_CHUNK, HALF), u32),
                        pltpu.SemaphoreType.DMA((2,)),
                        pltpu.SemaphoreType.DMA(()),
                        pltpu.SMEM((1,), i32)],
        compiler_params=pltpu.CompilerParams(
            dimension_semantics=("arbitrary",), vmem_limit_bytes=VMEM_LIMIT),
        name="moe_dispatch",
    )(table, ztable, x, meta, off_col, norm_g.reshape(1, D_MODEL))


X_BUFFERS = 3
Y_BUFFERS = 2
BIG_BLOCKS = 4


def _n_work_items(t):
    return _n_expert_blocks(t) // BIG_BLOCKS + (BIG_BLOCKS - 1) * N_EXPERTS + 1


def _expert_kernel(ifirst_ref, istart_ref, ibig_ref, w1_ref, w3_ref, w2_ref, xs_hbm, ys_hbm,
                   xbuf, ybuf, xsem, ysem, w1b, w3b, w2b):
    e = pl.program_id(0)
    lo = ifirst_ref[e]
    hi = ifirst_ref[e + 1]
    total = ifirst_ref[N_EXPERTS]
    nvalid = ifirst_ref[N_EXPERTS + 1]
    n_blocks = ys_hbm.shape[0] // EXPERT_ROWS

    def rows_of(big):
        return (BIG_BLOCKS if big else 1) * EXPERT_ROWS

    def hbm_rows(ref, i, big):
        return ref.at[pl.ds(pl.multiple_of(istart_ref[i] * EXPERT_ROWS, EXPERT_ROWS), rows_of(big)), :]

    def x_copy(i, big):
        slot = i % X_BUFFERS
        return pltpu.make_async_copy(hbm_rows(xs_hbm, i, big),
                                     xbuf.at[slot, pl.ds(0, rows_of(big)), :], xsem.at[slot])

    def y_copy(i, big):
        slot = i % Y_BUFFERS
        return pltpu.make_async_copy(ybuf.at[slot, pl.ds(0, rows_of(big)), :],
                                     hbm_rows(ys_hbm, i, big), ysem.at[slot])

    def by_size(i, fn):
        @pl.when(ibig_ref[i] != 0)
        def _():
            fn(True)

        @pl.when(ibig_ref[i] == 0)
        def _():
            fn(False)

    @pl.when(e == 0)
    def _():
        for k in range(X_BUFFERS - 1):
            @pl.when(k < total)
            def _():
                by_size(k, lambda big: x_copy(k, big).start())

    @pl.when(hi > lo)
    def _():
        w1b[...] = w1_ref[0, 0].astype(bf16)
        w3b[...] = w3_ref[0, 0].astype(bf16)
        w2b[...] = w2_ref[0, 0].astype(bf16)

    def item(i, carry):
        ahead = i + X_BUFFERS - 1

        @pl.when(ahead < total)
        def _():
            by_size(ahead, lambda big: x_copy(ahead, big).start())

        def run(big):
            rows = rows_of(big)
            x_copy(i, big).wait()
            hn = _unpack_rows(xbuf[i % X_BUFFERS, pl.ds(0, rows), :])
            h1 = _dot(hn, w1b[...])
            h3 = _dot(hn, w3b[...])
            hid = ((h1 / (1.0 + jnp.exp(-h1))) * h3).astype(bf16)
            y = _dot(hid, w2b[...])

            @pl.when(i >= Y_BUFFERS)
            def _():
                by_size(i - Y_BUFFERS, lambda b: y_copy(i - Y_BUFFERS, b).wait())

            ybuf[i % Y_BUFFERS, pl.ds(0, rows), :] = _pack_rows(y.astype(bf16).astype(f32))
            y_copy(i, big).start()

        by_size(i, run)
        return carry

    lax.fori_loop(lo, hi, item, 0)

    @pl.when(e == pl.num_programs(0) - 1)
    def _():
        for k in range(1, Y_BUFFERS + 1):
            @pl.when(total >= k)
            def _():
                by_size(total - k, lambda big: y_copy(total - k, big).wait())
        ybuf[0, pl.ds(0, EXPERT_ROWS), :] = jnp.zeros((EXPERT_ROWS, HALF), u32)

        def tail(g):
            dst = pl.multiple_of(g * EXPERT_ROWS, EXPERT_ROWS)
            return pltpu.make_async_copy(ybuf.at[0, pl.ds(0, EXPERT_ROWS), :],
                                         ys_hbm.at[pl.ds(dst, EXPERT_ROWS), :], ysem.at[0])
        lax.fori_loop(nvalid, n_blocks, lambda g, c: (tail(g).start(), c)[1], 0)
        lax.fori_loop(nvalid, n_blocks, lambda g, c: (tail(g).wait(), c)[1], 0)


def _experts(xs, item_first, item_start, item_big, w1, w3, w2, layer):
    w_index = lambda e, *_: (layer, e, 0, 0)
    big_rows = BIG_BLOCKS * EXPERT_ROWS
    grid_spec = pltpu.PrefetchScalarGridSpec(
        num_scalar_prefetch=3,
        grid=(N_EXPERTS,),
        in_specs=[
            pl.BlockSpec((1, 1, D_MODEL, EXPERT_FF), w_index),
            pl.BlockSpec((1, 1, D_MODEL, EXPERT_FF), w_index),
            pl.BlockSpec((1, 1, EXPERT_FF, D_MODEL), w_index),
            pl.BlockSpec(memory_space=pl.ANY),
        ],
        out_specs=pl.BlockSpec(memory_space=pl.ANY),
        scratch_shapes=[pltpu.VMEM((X_BUFFERS, big_rows, HALF), u32),
                        pltpu.VMEM((Y_BUFFERS, big_rows, HALF), u32),
                        pltpu.SemaphoreType.DMA((X_BUFFERS,)),
                        pltpu.SemaphoreType.DMA((Y_BUFFERS,)),
                        pltpu.VMEM((D_MODEL, EXPERT_FF), bf16),
                        pltpu.VMEM((D_MODEL, EXPERT_FF), bf16),
                        pltpu.VMEM((EXPERT_FF, D_MODEL), bf16)],
    )
    return pl.pallas_call(
        _expert_kernel,
        grid_spec=grid_spec,
        out_shape=jax.ShapeDtypeStruct(xs.shape, u32),
        compiler_params=pltpu.CompilerParams(
            dimension_semantics=("arbitrary",), vmem_limit_bytes=VMEM_LIMIT),
        name="moe_experts",
    )(item_first, item_start, item_big, w1, w3, w2, xs)


def _combine(x_ref, tab_ref, tab_next_ref, cmeta_ref, ys_hbm, ybuf, sems):
    i = pl.program_id(0)
    n_steps = pl.num_programs(0)
    tiles = x_ref.shape[0] // SORT_TILE
    slot = i % 2

    def copy_into(s, u):
        def make(rows, loc, glb):
            dst = pl.multiple_of(u * LOCAL_CAP + loc, RUN_CHUNK)
            return pltpu.make_async_copy(ys_hbm.at[pl.ds(pl.multiple_of(glb, RUN_CHUNK), rows), :],
                                         ybuf.at[s, pl.ds(dst, rows), :], sems.at[s])
        return make

    def fetch(tab, s):
        for u in range(tiles):
            _start_runs(tab, u, copy_into(s, u))

            def clear(c, carry):
                dst = pl.multiple_of(u * LOCAL_CAP + c * RUN_CHUNK, RUN_CHUNK)
                ybuf[s, pl.ds(dst, RUN_CHUNK), :] = jnp.zeros((RUN_CHUNK, HALF), u32)
                return carry
            lax.fori_loop(tab[u, 0, TAB_UNITS], MAX_CHUNKS, clear, 0)

    @pl.when(i == 0)
    def _():
        fetch(tab_ref, 0)

    @pl.when(i + 1 < n_steps)
    def _():
        fetch(tab_next_ref, 1 - slot)

    for u in range(tiles):
        _wait_runs(tab_ref[u, 0, TAB_UNITS], copy_into(slot, u))

    p = lax.broadcasted_iota(i32, (LOCAL_CAP, SORT_TILE), 0).astype(f32)
    parts = []
    for u in range(tiles):
        cs = slice(u * SORT_TILE, (u + 1) * SORT_TILE)
        pos0, pos1 = cmeta_ref[0:1, cs], cmeta_ref[1:2, cs]
        g = (jnp.where(p == pos0, cmeta_ref[2:3, cs], 0.0)
             + jnp.where(p == pos1, cmeta_ref[3:4, cs], 0.0))
        g_hi = g.astype(bf16)
        g_lo = (g - g_hi.astype(f32)).astype(bf16)
        y = _unpack_rows(ybuf[slot, pl.ds(u * LOCAL_CAP, LOCAL_CAP), :])
        both = _dot_tn(jnp.concatenate([g_hi, g_lo], axis=1), y)
        parts.append(x_ref[cs, :] + both[:SORT_TILE] + both[SORT_TILE:])
    return parts[0] if tiles == 1 else jnp.concatenate(parts, axis=0)


def _combine_specs(ts, n_steps):
    tiles = ts // SORT_TILE
    return [
        pl.BlockSpec((ts, D_MODEL), lambda i: (i, 0)),
        pl.BlockSpec((tiles, 1, TABLE_LEN), lambda i: (i, 0, 0), memory_space=pltpu.SMEM),
        pl.BlockSpec((tiles, 1, TABLE_LEN), lambda i: (jnp.minimum(i + 1, n_steps - 1), 0, 0),
                     memory_space=pltpu.SMEM),
        pl.BlockSpec((META_ROWS, ts), lambda i: (0, i)),
        pl.BlockSpec(memory_space=pl.ANY),
    ]


def _combine_scratch(ts):
    tiles = ts // SORT_TILE
    return [pltpu.VMEM((2, tiles * LOCAL_CAP, HALF), u32), pltpu.SemaphoreType.DMA((2,))]


def _gelu(z):
    return 0.5 * z * (1.0 + lax.erf(z * (2.0 ** -0.5)))


def _sgu_kernel(x_ref, ng_ref, wu_ref, wv_ref, lng_ref, lnb_ref,
                ws_ref, bs_ref, wo_ref, mng_ref, wrt_ref, br_ref,
                x3_ref, meta_ref, cnt_ref, vn_ref):
    ts = x_ref.shape[0]
    x2 = x_ref[...]
    h = (x2 * _rms(x2) * ng_ref[...]).astype(bf16)
    v = _gelu(_dot(h, wv_ref[...]))
    mu = jnp.mean(v, axis=-1, keepdims=True)
    vc = v - mu
    rstd = lax.rsqrt(jnp.mean(vc * vc, axis=-1, keepdims=True) + EPS)
    vn_ref[...] = (vc * rstd * lng_ref[...] + lnb_ref[...]).astype(bf16)

    pos = lax.broadcasted_iota(i32, (SGU_BLOCK, SGU_BLOCK), 0) // CHUNK
    src = lax.broadcasted_iota(i32, (SGU_BLOCK, SGU_BLOCK), 1) // CHUNK
    acc = x2
    for g in range(SGU_GROUPS):
        cs = slice(g * SGU_GC, (g + 1) * SGU_GC)
        ws = jnp.where(pos >= src, ws_ref[g], jnp.zeros((), bf16))
        u = _gelu(_dot(h, wu_ref[:, cs]))
        mixed = [_dot(ws, vn_ref[nb * SGU_BLOCK:(nb + 1) * SGU_BLOCK, cs]) + bs_ref[:, g:g + 1]
                 for nb in range(ts // SGU_BLOCK)]
        out = (u * jnp.concatenate(mixed, axis=0)).astype(bf16)
        acc = acc + _dot(out, wo_ref[cs, :])
    x3_ref[...] = acc
    _route(acc, mng_ref[...], wrt_ref[...], br_ref[...], meta_ref, cnt_ref)


def _sgu_layer(x2, norm_g, w_in, ln_g, ln_b, w_s, b_s, w_out, moe_norm_g, wrt, br, ts):
    t = x2.shape[0]
    n_steps = t // ts
    args = (norm_g.reshape(1, D_MODEL), w_in[:, :SGU_HALF].astype(bf16), w_in[:, SGU_HALF:].astype(bf16),
            ln_g.reshape(1, SGU_HALF), ln_b.reshape(1, SGU_HALF), w_s.astype(bf16), b_s.T,
            w_out.astype(bf16), moe_norm_g.reshape(1, D_MODEL), wrt, br)
    return pl.pallas_call(
        _sgu_kernel,
        grid=(n_steps,),
        in_specs=[pl.BlockSpec((ts, D_MODEL), lambda i: (i, 0))] + [_const_spec(a.shape) for a in args],
        out_specs=_route_out_specs(ts, lambda i: i),
        out_shape=_route_out_shapes(t),
        scratch_shapes=[pltpu.VMEM((ts, SGU_HALF), bf16)],
        compiler_params=pltpu.CompilerParams(
            dimension_semantics=("arbitrary",), vmem_limit_bytes=VMEM_LIMIT),
        name="sgu_mixer",
    )(x2, *args)


def _combine_kernel(x_ref, tab_ref, tab_next_ref, cmeta_ref, ys_hbm, out_ref, ybuf, sems):
    out_ref[...] = _combine(x_ref, tab_ref, tab_next_ref, cmeta_ref, ys_hbm, ybuf, sems)


def _final_kernel(x_ref, tab_ref, tab_next_ref, cmeta_ref, ys_hbm, ng_ref, out_ref, ybuf, sems):
    x = _combine(x_ref, tab_ref, tab_next_ref, cmeta_ref, ys_hbm, ybuf, sems)
    out_ref[...] = x * _rms(x) * ng_ref[...]


def _combine_layer(x, table, cmeta, ys, norm_g, ts):
    t = x.shape[0]
    n_steps = t // ts
    final = norm_g is not None
    extra = (norm_g.reshape(1, D_MODEL),) if final else ()
    return pl.pallas_call(
        _final_kernel if final else _combine_kernel,
        grid=(n_steps,),
        in_specs=_combine_specs(ts, n_steps) + [_const_spec(a.shape) for a in extra],
        out_specs=pl.BlockSpec((ts, D_MODEL), lambda i: (i, 0)),
        out_shape=jax.ShapeDtypeStruct((t, D_MODEL), f32),
        scratch_shapes=_combine_scratch(ts),
        compiler_params=pltpu.CompilerParams(
            dimension_semantics=("arbitrary",), vmem_limit_bytes=VMEM_LIMIT),
        name="final_norm" if final else "moe_combine",
    )(x, table, table, cmeta, ys, *extra)


def _moe(x, meta, cnt, norm_g, w1, w3, w2, layer, ts):
    table, off_col, items, ztable = _plan(cnt, x.shape[0])
    sorted_x, cmeta = _dispatch(x, meta, table, off_col, ztable, norm_g, ts)
    return _experts(sorted_x, *items, w1, w3, w2, layer), table, cmeta


def _forward(x, gla_norm, gla_w_in, gla_w_gate_up, gla_b_gate, gla_head_g, gla_w_out, sgu_norm, sgu_w_in,
             sgu_ln_g, sgu_ln_b, sgu_w_s, sgu_b_s, sgu_w_out, moe_norm, moe_w_group, moe_b_group,
             moe_w_sub, moe_b_sub, moe_w1, moe_w3, moe_w2, final_norm, *, ts_gla, ts_sgu, ts_fin):
    wrt0, br0 = _router_params(moe_w_group[0], moe_b_group[0], moe_w_sub[0], moe_b_sub[0])
    wrt1, br1 = _router_params(moe_w_group[1], moe_b_group[1], moe_w_sub[1], moe_b_sub[1])
    x1, meta0, cnt0 = _gla_layer(x, gla_norm[0], gla_w_in[0], gla_w_gate_up[0], gla_b_gate[0], gla_head_g[0],
                                 gla_w_out[0], moe_norm[0], wrt0, br0, ts_gla)
    ys0, table0, cmeta0 = _moe(x1, meta0, cnt0, moe_norm[0], moe_w1, moe_w3, moe_w2, 0, ts_fin)
    x2 = _combine_layer(x1, table0, cmeta0, ys0, None, ts_fin)
    x3, meta1, cnt1 = _sgu_layer(x2, sgu_norm[0], sgu_w_in[0], sgu_ln_g[0], sgu_ln_b[0],
                                 sgu_w_s[0], sgu_b_s[0], sgu_w_out[0], moe_norm[1], wrt1, br1, ts_sgu)
    ys1, table1, cmeta1 = _moe(x3, meta1, cnt1, moe_norm[1], moe_w1, moe_w3, moe_w2, 1, ts_fin)
    out = _combine_layer(x3, table1, cmeta1, ys1, final_norm, ts_fin)
    return out.reshape(x.shape)


def kernel(x, gla_norm, gla_w_in, gla_w_gate_up, gla_b_gate, gla_head_g, gla_w_out, sgu_norm, sgu_w_in, sgu_ln_g, sgu_ln_b, sgu_w_s, sgu_b_s, sgu_w_out, moe_norm, moe_w_group, moe_b_group, moe_w_sub, moe_b_sub, moe_w1, moe_w3, moe_w2, final_norm):
    return _forward(x, gla_norm, gla_w_in, gla_w_gate_up, gla_b_gate, gla_head_g, gla_w_out, sgu_norm,
                    sgu_w_in, sgu_ln_g, sgu_ln_b, sgu_w_s, sgu_b_s, sgu_w_out, moe_norm, moe_w_group,
                    moe_b_group, moe_w_sub, moe_b_sub, moe_w1, moe_w3, moe_w2, final_norm,
                    ts_gla=1024, ts_sgu=1024, ts_fin=1024)
```

```python
import jax
import jax.numpy as jnp
from jax import lax
from jax.experimental import pallas as pl
from jax.experimental.pallas import tpu as pltpu

D_MODEL = 1024
HALF = D_MODEL // 2
EPS = 1e-6
LANES = 128
SUBLANES = 8

CHUNK = 64
GLA_HEADS = 4
GLA_DK = 128
GLA_DV = 256
GLA_HK = GLA_HEADS * GLA_DK
GLA_HV = GLA_HEADS * GLA_DV
GLA_GATE_RANK = 16
GLA_TAU = 16.0

SGU_BLOCK = 128
SGU_GROUPS = 4
SGU_HALF = 2048
SGU_GC = SGU_HALF // SGU_GROUPS

N_GROUPS = 4
EXPERTS_PER_GROUP = 8
N_EXPERTS = N_GROUPS * EXPERTS_PER_GROUP
TOP_K = 2
EXPERT_FF = 512
ROUTE_ROWS = 64
META_ROWS = 8

SORT_TILE = 256
RUN_CHUNK = SUBLANES
LOCAL_CAP = 768
MAX_CHUNKS = LOCAL_CAP // RUN_CHUNK
EXPERT_ROWS = 512
ZERO_CHUNK = EXPERT_ROWS
MAX_PAIRS = MAX_CHUNKS // 2
TABLE_LEN = 256
TAB_PAIR_LOC, TAB_PAIR_GLB = 0, MAX_PAIRS
TAB_ODD_LOC, TAB_ODD_GLB = 2 * MAX_PAIRS, 2 * MAX_PAIRS + N_EXPERTS
TAB_PAIRS = 2 * MAX_PAIRS + 2 * N_EXPERTS
TAB_ODDS, TAB_UNITS = TAB_PAIRS + 1, TAB_PAIRS + 2
assert LOCAL_CAP >= TOP_K * SORT_TILE + N_EXPERTS * (RUN_CHUNK - 1)
assert TAB_UNITS < TABLE_LEN

VMEM_LIMIT = 56 * 1024 * 1024

f32 = jnp.float32
bf16 = jnp.bfloat16
i32 = jnp.int32
u32 = jnp.uint32


def _dot(a, b):
    return jnp.dot(a, b, preferred_element_type=f32)


def _dot_tn(a, b):
    return lax.dot_general(a, b, (((0,), (0,)), ((), ())), preferred_element_type=f32)


def _dot_nt(a, b):
    return lax.dot_general(a, b, (((1,), (1,)), ((), ())), preferred_element_type=f32)


def _rms(x):
    return lax.rsqrt(jnp.mean(x * x, axis=-1, keepdims=True) + EPS)


def _pack_rows(v):
    lo = lax.bitcast_convert_type(v[:, :HALF], u32)
    hi = lax.bitcast_convert_type(v[:, HALF:], u32)
    return lax.shift_right_logical(lo, jnp.uint32(16)) | (hi & jnp.uint32(0xFFFF0000))


def _unpack_rows(w):
    lo = lax.bitcast_convert_type(lax.shift_left(w, jnp.uint32(16)), f32)
    hi = lax.bitcast_convert_type(w & jnp.uint32(0xFFFF0000), f32)
    return jnp.concatenate([lo, hi], axis=-1).astype(bf16)


def _const_spec(shape):
    return pl.BlockSpec(shape, lambda *_: (0,) * len(shape))


def _route(x1, norm_g, wr_t, br, meta_ref, cnt_ref):
    n = x1.shape[0]
    hn = (x1 * _rms(x1) * norm_g).astype(bf16)
    lt = _dot_nt(wr_t, hn) + br
    rows = lax.broadcasted_iota(i32, (SUBLANES, n), 0)
    neg = jnp.float32(-jnp.inf)
    lg = jnp.where(rows < N_GROUPS, lt[0:SUBLANES], neg)
    gmax = jnp.max(lg, axis=0, keepdims=True)
    gidx = jnp.min(jnp.where(lg == gmax, rows, SUBLANES), axis=0, keepdims=True)
    g_w = 1.0 / jnp.sum(jnp.exp(lg - gmax), axis=0, keepdims=True)
    chosen = jnp.zeros((SUBLANES, n), f32)
    for g in range(N_GROUPS):
        chosen = jnp.where(gidx == g, lt[SUBLANES * (g + 1):SUBLANES * (g + 2)], chosen)
    m1 = jnp.max(chosen, axis=0, keepdims=True)
    i1 = jnp.min(jnp.where(chosen == m1, rows, SUBLANES), axis=0, keepdims=True)
    rest = jnp.where(rows == i1, neg, chosen)
    m2 = jnp.max(rest, axis=0, keepdims=True)
    i2 = jnp.min(jnp.where(rest == m2, rows, SUBLANES), axis=0, keepdims=True)
    t = jnp.exp(m2 - m1)
    s1 = 1.0 / (1.0 + t)
    s2 = t / (1.0 + t)
    e1 = gidx * EXPERTS_PER_GROUP + i1
    e2 = gidx * EXPERTS_PER_GROUP + i2
    zero = jnp.zeros((1, n), f32)
    meta_ref[...] = jnp.concatenate(
        [e1.astype(f32), e2.astype(f32), g_w * s1, g_w * s2, zero, zero, zero, zero], axis=0)
    ids = lax.broadcasted_iota(i32, (N_EXPERTS, n), 0)
    hits = (ids == e1).astype(f32) + (ids == e2).astype(f32)
    for u in range(n // SORT_TILE):
        cnt_ref[u * N_EXPERTS:(u + 1) * N_EXPERTS, :] = jnp.sum(
            hits[:, u * SORT_TILE:(u + 1) * SORT_TILE], axis=1, keepdims=True)


def _router_params(w_group, b_group, w_sub, b_sub):
    tail = ROUTE_ROWS - SUBLANES - N_EXPERTS
    wrt = jnp.concatenate([
        w_group.T, jnp.zeros((SUBLANES - N_GROUPS, D_MODEL), f32),
        jnp.transpose(w_sub, (0, 2, 1)).reshape(N_EXPERTS, D_MODEL),
        jnp.zeros((tail, D_MODEL), f32)], axis=0)
    br = jnp.concatenate([b_group, jnp.zeros((SUBLANES - N_GROUPS,), f32), b_sub.reshape(N_EXPERTS),
                          jnp.zeros((tail,), f32)]).reshape(ROUTE_ROWS, 1)
    return wrt.astype(bf16), br


def _route_out_specs(ts, index):
    tiles = ts // SORT_TILE
    return [pl.BlockSpec((ts, D_MODEL), lambda *g: (index(*g), 0)),
            pl.BlockSpec((META_ROWS, ts), lambda *g: (0, index(*g))),
            pl.BlockSpec((tiles * N_EXPERTS, 1), lambda *g: (index(*g), 0))]


def _route_out_shapes(t):
    return [jax.ShapeDtypeStruct((t, D_MODEL), f32),
            jax.ShapeDtypeStruct((META_ROWS, t), f32),
            jax.ShapeDtypeStruct((t // SORT_TILE * N_EXPERTS, 1), f32)]


def _gla_kernel(x_ref, ng_ref, wq_ref, wk_ref, wv_ref, wr_ref, wg_ref, wgu_ref, bg_ref,
                hg_ref, wo_ref, mng_ref, wrt_ref, br_ref,
                x1_ref, meta_ref, cnt_ref, st_ref, o_ref):
    ts = x_ref.shape[1]

    @pl.when(pl.program_id(1) == 0)
    def _():
        st_ref[...] = jnp.zeros_like(st_ref)

    x = x_ref[0]
    h = (x * _rms(x) * ng_ref[...]).astype(bf16)
    q = _dot(h, wq_ref[...]) * (GLA_DK ** -0.5)
    k = _dot(h, wk_ref[...])
    v = _dot(h, wv_ref[...]).astype(bf16)
    glr = _dot(h, wg_ref[...]).astype(bf16)
    r = _dot(h, wr_ref[...])
    gp = _dot(glr, wgu_ref[...]) + bg_ref[...]
    log_a = (jnp.minimum(gp, 0.0) - jnp.log(1.0 + jnp.exp(-jnp.abs(gp)))) * (1.0 / GLA_TAU)

    row = lax.broadcasted_iota(i32, (CHUNK, GLA_HK), 0)
    for c in range(ts // CHUNK):
        rs = slice(c * CHUNK, (c + 1) * CHUNK)
        b = log_a[rs]
        sh = 1
        while sh < CHUNK:
            b = b + jnp.where(row >= sh, pltpu.roll(b, sh, axis=0), 0.0)
            sh *= 2
        b_end = b[CHUNK - 1:CHUNK]
        kdec = (k[rs] * jnp.exp(b_end - b)).astype(bf16)
        decay = jnp.exp(b_end)
        qc = q[rs].astype(bf16)
        vc = v[rs]
        for hd in range(GLA_HEADS):
            ks = slice(hd * GLA_DK, (hd + 1) * GLA_DK)
            vs = slice(hd * GLA_DV, (hd + 1) * GLA_DV)
            st = st_ref[hd] * decay[:, ks] + _dot_tn(vc[:, vs], kdec[:, ks])
            st_ref[hd] = st
            o_ref[rs, vs] = _dot_nt(qc[:, ks], st.astype(bf16))

    gated = []
    for hd in range(GLA_HEADS):
        vs = slice(hd * GLA_DV, (hd + 1) * GLA_DV)
        oh = o_ref[:, vs]
        rh = r[:, vs]
        gated.append(oh * _rms(oh) * hg_ref[:, vs] * (rh / (1.0 + jnp.exp(-rh))))
    y = _dot(jnp.concatenate(gated, axis=-1).astype(bf16), wo_ref[...])
    x1 = x + y
    x1_ref[...] = x1
    _route(x1, mng_ref[...], wrt_ref[...], br_ref[...], meta_ref, cnt_ref)


def _gla_layer(x, norm_g, w_in, w_gate_up, b_gate, head_g, w_out, moe_norm_g, wrt, br, ts):
    bsz, seq, _ = x.shape
    t = bsz * seq
    wq = w_in[:, 0:GLA_HK].astype(bf16)
    wk = w_in[:, GLA_HK:2 * GLA_HK].astype(bf16)
    wv = w_in[:, 2 * GLA_HK:2 * GLA_HK + GLA_HV].astype(bf16)
    wr = w_in[:, 2 * GLA_HK + GLA_HV:2 * GLA_HK + 2 * GLA_HV].astype(bf16)
    wg = jnp.pad(w_in[:, 2 * GLA_HK + 2 * GLA_HV:], ((0, 0), (0, LANES - GLA_GATE_RANK))).astype(bf16)
    wgu = jnp.pad(w_gate_up, ((0, LANES - GLA_GATE_RANK), (0, 0))).astype(bf16)
    n_s = seq // ts
    args = (x, norm_g.reshape(1, D_MODEL), wq, wk, wv, wr, wg, wgu, b_gate.reshape(1, GLA_HK),
            head_g.reshape(1, GLA_HV), w_out.astype(bf16), moe_norm_g.reshape(1, D_MODEL), wrt, br)
    in_specs = [pl.BlockSpec((1, ts, D_MODEL), lambda b, s: (b, s, 0))]
    in_specs += [_const_spec(a.shape) for a in args[1:]]
    return pl.pallas_call(
        _gla_kernel,
        grid=(bsz, n_s),
        in_specs=in_specs,
        out_specs=_route_out_specs(ts, lambda b, s: b * n_s + s),
        out_shape=_route_out_shapes(t),
        scratch_shapes=[pltpu.VMEM((GLA_HEADS, GLA_DV, GLA_DK), f32),
                        pltpu.VMEM((ts, GLA_HV), f32)],
        compiler_params=pltpu.CompilerParams(
            dimension_semantics=("arbitrary", "arbitrary"), vmem_limit_bytes=VMEM_LIMIT),
        name="gla_mixer",
    )(*args)


def _n_expert_blocks(t):
    tiles = t // SORT_TILE
    worst = t * TOP_K + tiles * N_EXPERTS * (RUN_CHUNK - 1) + N_EXPERTS * (EXPERT_ROWS - 1)
    return -(-worst // EXPERT_ROWS)


def _zero_table_len(t):
    tail = _n_expert_blocks(t) * EXPERT_ROWS - t * TOP_K
    n = N_EXPERTS + -(-tail // ZERO_CHUNK) + 1
    return -(-(n + 1) // LANES) * LANES


def _cumsum(x, axis):
    x = jnp.moveaxis(x, axis, -1)
    n = x.shape[-1]
    upto = jnp.arange(n, dtype=i32)[:, None] <= jnp.arange(n, dtype=i32)[None, :]
    return jnp.moveaxis(jnp.sum(x[..., :, None] * upto.astype(i32), axis=-2), -1, axis)


def _flat_chunks(n_per, max_n):
    cum = _cumsum(n_per, -1)
    c = jnp.arange(max_n, dtype=i32)
    seg = jnp.minimum(jnp.sum((c[:, None] >= cum[..., None, :]).astype(i32), axis=-1), n_per.shape[-1] - 1)
    onehot = (seg[..., None] == jnp.arange(n_per.shape[-1], dtype=i32)).astype(i32)
    return onehot, c - _pick(onehot, cum - n_per), cum[..., -1]


def _pick(onehot, per_segment):
    return jnp.sum(onehot * per_segment[..., None, :], axis=-1)


def _plan(cnt, t):
    tiles = t // SORT_TILE
    cnt = cnt.reshape(tiles, N_EXPERTS).astype(i32)
    nch = (cnt + RUN_CHUNK - 1) // RUN_CHUNK
    run = nch * RUN_CHUNK
    counts = jnp.sum(run, axis=0)
    padded = ((counts + EXPERT_ROWS - 1) // EXPERT_ROWS) * EXPERT_ROWS
    pad_end = _cumsum(padded, 0)
    pad_start = pad_end - padded
    base = pad_start[None, :] + _cumsum(run, 0) - run
    n_pair = nch // 2
    off = _cumsum(run, 1) - run
    onehot, j, n_pairs = _flat_chunks(n_pair, MAX_PAIRS)
    pair_loc = _pick(onehot, off) + j * (2 * RUN_CHUNK)
    pair_glb = _pick(onehot, base) + j * (2 * RUN_CHUNK)
    onehot, _, n_odds = _flat_chunks(nch % 2, N_EXPERTS)
    odd_loc = _pick(onehot, off + n_pair * (2 * RUN_CHUNK))
    odd_glb = _pick(onehot, base + n_pair * (2 * RUN_CHUNK))
    units = jnp.sum(nch, axis=1)
    fill = jnp.zeros((tiles, TABLE_LEN - TAB_UNITS - 1), i32)
    table = jnp.concatenate([pair_loc, pair_glb, odd_loc, odd_glb, n_pairs[:, None], n_odds[:, None],
                             units[:, None], fill], axis=1).reshape(tiles, 1, TABLE_LEN)
    off_col = off.astype(f32).reshape(tiles, N_EXPERTS, 1)

    nb = _n_expert_blocks(t)
    n_blk = padded // EXPERT_ROWS
    n_big = n_blk // BIG_BLOCKS
    n_items = n_big + n_blk % BIG_BLOCKS
    ionehot, k, n_total = _flat_chunks(n_items, _n_work_items(t))
    k_big = _pick(ionehot, n_big)
    item_big = (k < k_big).astype(i32)
    item_start = _pick(ionehot, pad_start // EXPERT_ROWS) + jnp.where(
        k < k_big, BIG_BLOCKS * k, BIG_BLOCKS * k_big + k - k_big)
    icum = _cumsum(n_items, 0)
    item_first = jnp.concatenate([icum - n_items, n_total[None], pad_end[-1:] // EXPERT_ROWS])
    items = (item_first, item_start, item_big)

    region_end = jnp.concatenate([pad_start[1:], jnp.full((1,), nb * EXPERT_ROWS, i32)])
    zlen = region_end - (pad_start + counts)
    nz = (zlen + ZERO_CHUNK - 1) // ZERO_CHUNK
    zl = _zero_table_len(t)
    zonehot, zj, nztot = _flat_chunks(nz, zl - 1)
    zstart = _pick(zonehot, region_end) - (zj + 1) * ZERO_CHUNK
    ztable = jnp.concatenate([zstart, nztot[None]]).reshape(1, 1, zl)
    return table, off_col, items, ztable


ISSUE_UNROLL = 4


def _start_runs(tab, u, make):
    for rows, n, loc0, glb0 in ((2 * RUN_CHUNK, tab[u, 0, TAB_PAIRS], TAB_PAIR_LOC, TAB_PAIR_GLB),
                                (RUN_CHUNK, tab[u, 0, TAB_ODDS], TAB_ODD_LOC, TAB_ODD_GLB)):
        def start(c, priority):
            make(rows, tab[u, 0, loc0 + c], tab[u, 0, glb0 + c]).start(priority=priority)

        def four(k, carry):
            for q in range(ISSUE_UNROLL):
                start(ISSUE_UNROLL * k + q, q % 2)
            return carry
        lax.fori_loop(0, n // ISSUE_UNROLL, four, 0)
        lax.fori_loop(n - n % ISSUE_UNROLL, n, lambda c, carry: (start(c, 0), carry)[1], 0)


def _wait_runs(units, make):
    left = units
    for per_wait in (16, 2, 1):
        lax.fori_loop(0, left // per_wait,
                      lambda c, carry: (make(per_wait * RUN_CHUNK, 0, 0).wait(), carry)[1], 0)
        left = left % per_wait


def _dispatch_kernel(tab_ref, ztab_ref, x_ref, meta_ref, offc_ref, ng_ref,
                     out_hbm, cmeta_ref, stage, zbuf, sems, zsem, prev_n):
    t = pl.program_id(0)
    slot = t % 2
    zl = ztab_ref.shape[2]

    def zero_copy(z):
        dst = pl.multiple_of(ztab_ref[0, 0, z], RUN_CHUNK)
        return pltpu.make_async_copy(zbuf, out_hbm.at[pl.ds(dst, ZERO_CHUNK), :], zsem)

    @pl.when(t == 0)
    def _():
        zbuf[...] = jnp.zeros_like(zbuf)
        nz = ztab_ref[0, 0, zl - 1]
        lax.fori_loop(0, nz, lambda z, c: (zero_copy(z).start(), c)[1], 0)
        lax.fori_loop(0, nz, lambda z, c: (zero_copy(z).wait(), c)[1], 0)

    tiles = x_ref.shape[0] // SORT_TILE
    ids = lax.broadcasted_iota(i32, (N_EXPERTS, SORT_TILE), 0)
    before = (lax.broadcasted_iota(i32, (SORT_TILE, SORT_TILE), 0)
              < lax.broadcasted_iota(i32, (SORT_TILE, SORT_TILE), 1)).astype(bf16)
    p = lax.broadcasted_iota(i32, (LOCAL_CAP, SORT_TILE), 0).astype(f32)
    zero = jnp.zeros((1, SORT_TILE), f32)
    for u in range(tiles):
        cs = slice(u * SORT_TILE, (u + 1) * SORT_TILE)
        x = x_ref[cs, :]
        hn = (x * _rms(x) * ng_ref[...]).astype(bf16)
        oh0 = ids == meta_ref[0:1, cs].astype(i32)
        oh1 = ids == meta_ref[1:2, cs].astype(i32)
        c0 = _dot(oh0.astype(bf16), before)
        c1 = _dot(oh1.astype(bf16), before)
        n0 = jnp.sum(oh0.astype(f32), axis=1, keepdims=True)
        offc = offc_ref[u]
        pos0 = jnp.sum(jnp.where(oh0, offc + c0, 0.0), axis=0, keepdims=True)
        pos1 = jnp.sum(jnp.where(oh1, offc + n0 + c1, 0.0), axis=0, keepdims=True)
        cmeta_ref[:, cs] = jnp.concatenate(
            [pos0, pos1, meta_ref[2:3, cs], meta_ref[3:4, cs], zero, zero, zero, zero], axis=0)
        perm = jnp.logical_or(p == pos0, p == pos1).astype(bf16)
        stage[slot, pl.ds(u * LOCAL_CAP, LOCAL_CAP), :] = _pack_rows(_dot(perm, hn))

    def copy_from(s, u):
        def make(rows, loc, glb):
            src = pl.multiple_of(u * LOCAL_CAP + loc, RUN_CHUNK)
            return pltpu.make_async_copy(stage.at[s, pl.ds(src, rows), :],
                                         out_hbm.at[pl.ds(pl.multiple_of(glb, RUN_CHUNK), rows), :], sems.at[s])
        return make

    @pl.when(t > 0)
    def _():
        _wait_runs(prev_n[0], copy_from(1 - slot, 0))

    units = 0
    for u in range(tiles):
        _start_runs(tab_ref, u, copy_from(slot, u))
        units = units + tab_ref[u, 0, TAB_UNITS]
    prev_n[0] = units

    @pl.when(t == pl.num_programs(0) - 1)
    def _():
        _wait_runs(units, copy_from(slot, 0))


def _dispatch(x, meta, table, off_col, ztable, norm_g, ts):
    t = x.shape[0]
    tiles = ts // SORT_TILE
    zl = ztable.shape[2]
    nb = _n_expert_blocks(t)
    return pl.pallas_call(
        _dispatch_kernel,
        grid=(t // ts,),
        in_specs=[
            pl.BlockSpec((tiles, 1, TABLE_LEN), lambda i: (i, 0, 0), memory_space=pltpu.SMEM),
            pl.BlockSpec((1, 1, zl), lambda i: (0, 0, 0), memory_space=pltpu.SMEM),
            pl.BlockSpec((ts, D_MODEL), lambda i: (i, 0)),
            pl.BlockSpec((META_ROWS, ts), lambda i: (0, i)),
            pl.BlockSpec((tiles, N_EXPERTS, 1), lambda i: (i, 0, 0)),
            _const_spec((1, D_MODEL)),
        ],
        out_specs=[pl.BlockSpec(memory_space=pl.ANY),
                   pl.BlockSpec((META_ROWS, ts), lambda i: (0, i))],
        out_shape=[jax.ShapeDtypeStruct((nb * EXPERT_ROWS, HALF), u32),
                   jax.ShapeDtypeStruct((META_ROWS, t), f32)],
        scratch_shapes=[pltpu.VMEM((2, tiles * LOCAL_CAP, HALF), u32),
                        pltpu.VMEM((ZERO_CHUNK, HALF), u32),
                        pltpu.SemaphoreType.DMA((2,)),
                        pltpu.SemaphoreType.DMA(()),
                        pltpu.SMEM((1,), i32)],
        compiler_params=pltpu.CompilerParams(
            dimension_semantics=("arbitrary",), vmem_limit_bytes=VMEM_LIMIT),
        name="moe_dispatch",
    )(table, ztable, x, meta, off_col, norm_g.reshape(1, D_MODEL))


X_BUFFERS = 3
Y_BUFFERS = 2
BIG_BLOCKS = 2


def _n_work_items(t):
    return _n_expert_blocks(t) // BIG_BLOCKS + (BIG_BLOCKS - 1) * N_EXPERTS + 1


def _expert_kernel(ifirst_ref, istart_ref, ibig_ref, w1_ref, w3_ref, w2_ref, xs_hbm, ys_hbm,
                   xbuf, ybuf, xsem, ysem, w1b, w3b, w2b):
    e = pl.program_id(0)
    lo = ifirst_ref[e]
    hi = ifirst_ref[e + 1]
    total = ifirst_ref[N_EXPERTS]
    nvalid = ifirst_ref[N_EXPERTS + 1]
    n_blocks = ys_hbm.shape[0] // EXPERT_ROWS

    def rows_of(big):
        return (BIG_BLOCKS if big else 1) * EXPERT_ROWS

    def hbm_rows(ref, i, big):
        return ref.at[pl.ds(pl.multiple_of(istart_ref[i] * EXPERT_ROWS, EXPERT_ROWS), rows_of(big)), :]

    def x_copy(i, big):
        slot = i % X_BUFFERS
        return pltpu.make_async_copy(hbm_rows(xs_hbm, i, big),
                                     xbuf.at[slot, pl.ds(0, rows_of(big)), :], xsem.at[slot])

    def y_copy(i, big):
        slot = i % Y_BUFFERS
        return pltpu.make_async_copy(ybuf.at[slot, pl.ds(0, rows_of(big)), :],
                                     hbm_rows(ys_hbm, i, big), ysem.at[slot])

    def by_size(i, fn):
        @pl.when(ibig_ref[i] != 0)
        def _():
            fn(True)

        @pl.when(ibig_ref[i] == 0)
        def _():
            fn(False)

    @pl.when(e == 0)
    def _():
        for k in range(X_BUFFERS - 1):
            @pl.when(k < total)
            def _():
                by_size(k, lambda big: x_copy(k, big).start())

    @pl.when(hi > lo)
    def _():
        w1b[...] = w1_ref[0, 0].astype(bf16)
        w3b[...] = w3_ref[0, 0].astype(bf16)
        w2b[...] = w2_ref[0, 0].astype(bf16)

    def item(i, carry):
        ahead = i + X_BUFFERS - 1

        @pl.when(ahead < total)
        def _():
            by_size(ahead, lambda big: x_copy(ahead, big).start())

        def run(big):
            rows = rows_of(big)
            x_copy(i, big).wait()
            hn = _unpack_rows(xbuf[i % X_BUFFERS, pl.ds(0, rows), :])
            h1 = _dot(hn, w1b[...])
            h3 = _dot(hn, w3b[...])
            hid = ((h1 / (1.0 + jnp.exp(-h1))) * h3).astype(bf16)
            y = _dot(hid, w2b[...])

            @pl.when(i >= Y_BUFFERS)
            def _():
                by_size(i - Y_BUFFERS, lambda b: y_copy(i - Y_BUFFERS, b).wait())

            ybuf[i % Y_BUFFERS, pl.ds(0, rows), :] = _pack_rows(y.astype(bf16).astype(f32))
            y_copy(i, big).start()

        by_size(i, run)
        return carry

    lax.fori_loop(lo, hi, item, 0)

    @pl.when(e == pl.num_programs(0) - 1)
    def _():
        for k in range(1, Y_BUFFERS + 1):
            @pl.when(total >= k)
            def _():
                by_size(total - k, lambda big: y_copy(total - k, big).wait())
        ybuf[0, pl.ds(0, EXPERT_ROWS), :] = jnp.zeros((EXPERT_ROWS, HALF), u32)

        def tail(g):
            dst = pl.multiple_of(g * EXPERT_ROWS, EXPERT_ROWS)
            return pltpu.make_async_copy(ybuf.at[0, pl.ds(0, EXPERT_ROWS), :],
                                         ys_hbm.at[pl.ds(dst, EXPERT_ROWS), :], ysem.at[0])
        lax.fori_loop(nvalid, n_blocks, lambda g, c: (tail(g).start(), c)[1], 0)
        lax.fori_loop(nvalid, n_blocks, lambda g, c: (tail(g).wait(), c)[1], 0)


def _experts(xs, item_first, item_start, item_big, w1, w3, w2, layer):
    w_index = lambda e, *_: (layer, e, 0, 0)
    big_rows = BIG_BLOCKS * EXPERT_ROWS
    grid_spec = pltpu.PrefetchScalarGridSpec(
        num_scalar_prefetch=3,
        grid=(N_EXPERTS,),
        in_specs=[
            pl.BlockSpec((1, 1, D_MODEL, EXPERT_FF), w_index),
            pl.BlockSpec((1, 1, D_MODEL, EXPERT_FF), w_index),
            pl.BlockSpec((1, 1, EXPERT_FF, D_MODEL), w_index),
            pl.BlockSpec(memory_space=pl.ANY),
        ],
        out_specs=pl.BlockSpec(memory_space=pl.ANY),
        scratch_shapes=[pltpu.VMEM((X_BUFFERS, big_rows, HALF), u32),
                        pltpu.VMEM((Y_BUFFERS, big_rows, HALF), u32),
                        pltpu.SemaphoreType.DMA((X_BUFFERS,)),
                        pltpu.SemaphoreType.DMA((Y_BUFFERS,)),
                        pltpu.VMEM((D_MODEL, EXPERT_FF), bf16),
                        pltpu.VMEM((D_MODEL, EXPERT_FF), bf16),
                        pltpu.VMEM((EXPERT_FF, D_MODEL), bf16)],
    )
    return pl.pallas_call(
        _expert_kernel,
        grid_spec=grid_spec,
        out_shape=jax.ShapeDtypeStruct(xs.shape, u32),
        compiler_params=pltpu.CompilerParams(
            dimension_semantics=("arbitrary",), vmem_limit_bytes=VMEM_LIMIT),
        name="moe_experts",
    )(item_first, item_start, item_big, w1, w3, w2, xs)


def _combine(x_ref, tab_ref, tab_next_ref, cmeta_ref, ys_hbm, ybuf, sems):
    i = pl.program_id(0)
    n_steps = pl.num_programs(0)
    tiles = x_ref.shape[0] // SORT_TILE
    slot = i % 2

    def copy_into(s, u):
        def make(rows, loc, glb):
            dst = pl.multiple_of(u * LOCAL_CAP + loc, RUN_CHUNK)
            return pltpu.make_async_copy(ys_hbm.at[pl.ds(pl.multiple_of(glb, RUN_CHUNK), rows), :],
                                         ybuf.at[s, pl.ds(dst, rows), :], sems.at[s])
        return make

    def fetch(tab, s):
        for u in range(tiles):
            _start_runs(tab, u, copy_into(s, u))

            def clear(c, carry):
                dst = pl.multiple_of(u * LOCAL_CAP + c * RUN_CHUNK, RUN_CHUNK)
                ybuf[s, pl.ds(dst, RUN_CHUNK), :] = jnp.zeros((RUN_CHUNK, HALF), u32)
                return carry
            lax.fori_loop(tab[u, 0, TAB_UNITS], MAX_CHUNKS, clear, 0)

    @pl.when(i == 0)
    def _():
        fetch(tab_ref, 0)

    @pl.when(i + 1 < n_steps)
    def _():
        fetch(tab_next_ref, 1 - slot)

    for u in range(tiles):
        _wait_runs(tab_ref[u, 0, TAB_UNITS], copy_into(slot, u))

    p = lax.broadcasted_iota(i32, (LOCAL_CAP, SORT_TILE), 0).astype(f32)
    parts = []
    for u in range(tiles):
        cs = slice(u * SORT_TILE, (u + 1) * SORT_TILE)
        pos0, pos1 = cmeta_ref[0:1, cs], cmeta_ref[1:2, cs]
        g = (jnp.where(p == pos0, cmeta_ref[2:3, cs], 0.0)
             + jnp.where(p == pos1, cmeta_ref[3:4, cs], 0.0))
        g_hi = g.astype(bf16)
        g_lo = (g - g_hi.astype(f32)).astype(bf16)
        y = _unpack_rows(ybuf[slot, pl.ds(u * LOCAL_CAP, LOCAL_CAP), :])
        both = _dot_tn(jnp.concatenate([g_hi, g_lo], axis=1), y)
        parts.append(x_ref[cs, :] + both[:SORT_TILE] + both[SORT_TILE:])
    return parts[0] if tiles == 1 else jnp.concatenate(parts, axis=0)


def _combine_specs(ts, n_steps):
    tiles = ts // SORT_TILE
    return [
        pl.BlockSpec((ts, D_MODEL), lambda i: (i, 0)),
        pl.BlockSpec((tiles, 1, TABLE_LEN), lambda i: (i, 0, 0), memory_space=pltpu.SMEM),
        pl.BlockSpec((tiles, 1, TABLE_LEN), lambda i: (jnp.minimum(i + 1, n_steps - 1), 0, 0),
                     memory_space=pltpu.SMEM),
        pl.BlockSpec((META_ROWS, ts), lambda i: (0, i)),
        pl.BlockSpec(memory_space=pl.ANY),
    ]


def _combine_scratch(ts):
    tiles = ts // SORT_TILE
    return [pltpu.VMEM((2, tiles * LOCAL_CAP, HALF), u32), pltpu.SemaphoreType.DMA((2,))]


def _gelu(z):
    return 0.5 * z * (1.0 + lax.erf(z * (2.0 ** -0.5)))


def _sgu_kernel(x_ref, ng_ref, wu_ref, wv_ref, lng_ref, lnb_ref,
                ws_ref, bs_ref, wo_ref, mng_ref, wrt_ref, br_ref,
                x3_ref, meta_ref, cnt_ref, vn_ref):
    ts = x_ref.shape[0]
    x2 = x_ref[...]
    h = (x2 * _rms(x2) * ng_ref[...]).astype(bf16)
    v = _gelu(_dot(h, wv_ref[...]))
    mu = jnp.mean(v, axis=-1, keepdims=True)
    vc = v - mu
    rstd = lax.rsqrt(jnp.mean(vc * vc, axis=-1, keepdims=True) + EPS)
    vn_ref[...] = (vc * rstd * lng_ref[...] + lnb_ref[...]).astype(bf16)

    pos = lax.broadcasted_iota(i32, (SGU_BLOCK, SGU_BLOCK), 0) // CHUNK
    src = lax.broadcasted_iota(i32, (SGU_BLOCK, SGU_BLOCK), 1) // CHUNK
    acc = x2
    for g in range(SGU_GROUPS):
        cs = slice(g * SGU_GC, (g + 1) * SGU_GC)
        ws = jnp.where(pos >= src, ws_ref[g], jnp.zeros((), bf16))
        u = _gelu(_dot(h, wu_ref[:, cs]))
        mixed = [_dot(ws, vn_ref[nb * SGU_BLOCK:(nb + 1) * SGU_BLOCK, cs]) + bs_ref[:, g:g + 1]
                 for nb in range(ts // SGU_BLOCK)]
        out = (u * jnp.concatenate(mixed, axis=0)).astype(bf16)
        acc = acc + _dot(out, wo_ref[cs, :])
    x3_ref[...] = acc
    _route(acc, mng_ref[...], wrt_ref[...], br_ref[...], meta_ref, cnt_ref)


def _sgu_layer(x2, norm_g, w_in, ln_g, ln_b, w_s, b_s, w_out, moe_norm_g, wrt, br, ts):
    t = x2.shape[0]
    n_steps = t // ts
    args = (norm_g.reshape(1, D_MODEL), w_in[:, :SGU_HALF].astype(bf16), w_in[:, SGU_HALF:].astype(bf16),
            ln_g.reshape(1, SGU_HALF), ln_b.reshape(1, SGU_HALF), w_s.astype(bf16), b_s.T,
            w_out.astype(bf16), moe_norm_g.reshape(1, D_MODEL), wrt, br)
    return pl.pallas_call(
        _sgu_kernel,
        grid=(n_steps,),
        in_specs=[pl.BlockSpec((ts, D_MODEL), lambda i: (i, 0))] + [_const_spec(a.shape) for a in args],
        out_specs=_route_out_specs(ts, lambda i: i),
        out_shape=_route_out_shapes(t),
        scratch_shapes=[pltpu.VMEM((ts, SGU_HALF), bf16)],
        compiler_params=pltpu.CompilerParams(
            dimension_semantics=("arbitrary",), vmem_limit_bytes=VMEM_LIMIT),
        name="sgu_mixer",
    )(x2, *args)


def _combine_kernel(x_ref, tab_ref, tab_next_ref, cmeta_ref, ys_hbm, out_ref, ybuf, sems):
    out_ref[...] = _combine(x_ref, tab_ref, tab_next_ref, cmeta_ref, ys_hbm, ybuf, sems)


def _final_kernel(x_ref, tab_ref, tab_next_ref, cmeta_ref, ys_hbm, ng_ref, out_ref, ybuf, sems):
    x = _combine(x_ref, tab_ref, tab_next_ref, cmeta_ref, ys_hbm, ybuf, sems)
    out_ref[...] = x * _rms(x) * ng_ref[...]


def _combine_layer(x, table, cmeta, ys, norm_g, ts):
    t = x.shape[0]
    n_steps = t // ts
    final = norm_g is not None
    extra = (norm_g.reshape(1, D_MODEL),) if final else ()
    return pl.pallas_call(
        _final_kernel if final else _combine_kernel,
        grid=(n_steps,),
        in_specs=_combine_specs(ts, n_steps) + [_const_spec(a.shape) for a in extra],
        out_specs=pl.BlockSpec((ts, D_MODEL), lambda i: (i, 0)),
        out_shape=jax.ShapeDtypeStruct((t, D_MODEL), f32),
        scratch_shapes=_combine_scratch(ts),
        compiler_params=pltpu.CompilerParams(
            dimension_semantics=("arbitrary",), vmem_limit_bytes=VMEM_LIMIT),
        name="final_norm" if final else "moe_combine",
    )(x, table, table, cmeta, ys, *extra)


def _moe(x, meta, cnt, norm_g, w1, w3, w2, layer, ts):
    table, off_col, items, ztable = _plan(cnt, x.shape[0])
    sorted_x, cmeta = _dispatch(x, meta, table, off_col, ztable, norm_g, ts)
    return _experts(sorted_x, *items, w1, w3, w2, layer), table, cmeta


def _forward(x, gla_norm, gla_w_in, gla_w_gate_up, gla_b_gate, gla_head_g, gla_w_out, sgu_norm, sgu_w_in,
             sgu_ln_g, sgu_ln_b, sgu_w_s, sgu_b_s, sgu_w_out, moe_norm, moe_w_group, moe_b_group,
             moe_w_sub, moe_b_sub, moe_w1, moe_w3, moe_w2, final_norm, *, ts_gla, ts_sgu, ts_fin):
    wrt0, br0 = _router_params(moe_w_group[0], moe_b_group[0], moe_w_sub[0], moe_b_sub[0])
    wrt1, br1 = _router_params(moe_w_group[1], moe_b_group[1], moe_w_sub[1], moe_b_sub[1])
    x1, meta0, cnt0 = _gla_layer(x, gla_norm[0], gla_w_in[0], gla_w_gate_up[0], gla_b_gate[0], gla_head_g[0],
                                 gla_w_out[0], moe_norm[0], wrt0, br0, ts_gla)
    ys0, table0, cmeta0 = _moe(x1, meta0, cnt0, moe_norm[0], moe_w1, moe_w3, moe_w2, 0, ts_fin)
    x2 = _combine_layer(x1, table0, cmeta0, ys0, None, ts_fin)
    x3, meta1, cnt1 = _sgu_layer(x2, sgu_norm[0], sgu_w_in[0], sgu_ln_g[0], sgu_ln_b[0],
                                 sgu_w_s[0], sgu_b_s[0], sgu_w_out[0], moe_norm[1], wrt1, br1, ts_sgu)
    ys1, table1, cmeta1 = _moe(x3, meta1, cnt1, moe_norm[1], moe_w1, moe_w3, moe_w2, 1, ts_fin)
    out = _combine_layer(x3, table1, cmeta1, ys1, final_norm, ts_fin)
    return out.reshape(x.shape)


def kernel(x, gla_norm, gla_w_in, gla_w_gate_up, gla_b_gate, gla_head_g, gla_w_out, sgu_norm, sgu_w_in, sgu_ln_g, sgu_ln_b, sgu_w_s, sgu_b_s, sgu_w_out, moe_norm, moe_w_group, moe_b_group, moe_w_sub, moe_b_sub, moe_w1, moe_w3, moe_w2, final_norm):
    return _forward(x, gla_norm, gla_w_in, gla_w_gate_up, gla_b_gate, gla_head_g, gla_w_out, sgu_norm,
                    sgu_w_in, sgu_ln_g, sgu_ln_b, sgu_w_s, sgu_b_s, sgu_w_out, moe_norm, moe_w_group,
                    moe_b_group, moe_w_sub, moe_b_sub, moe_w1, moe_w3, moe_w2, final_norm,
                    ts_gla=1024, ts_sgu=1024, ts_fin=1024)
```

```python
import jax
import jax.numpy as jnp
from jax import lax
from jax.experimental import pallas as pl
from jax.experimental.pallas import tpu as pltpu

D_MODEL = 1024
HALF = D_MODEL // 2
EPS = 1e-6
LANES = 128
SUBLANES = 8

CHUNK = 64
GLA_HEADS = 4
GLA_DK = 128
GLA_DV = 256
GLA_HK = GLA_HEADS * GLA_DK
GLA_HV = GLA_HEADS * GLA_DV
GLA_GATE_RANK = 16
GLA_TAU = 16.0

SGU_BLOCK = 128
SGU_GROUPS = 4
SGU_HALF = 2048
SGU_GC = SGU_HALF // SGU_GROUPS

N_GROUPS = 4
EXPERTS_PER_GROUP = 8
N_EXPERTS = N_GROUPS * EXPERTS_PER_GROUP
TOP_K = 2
EXPERT_FF = 512
ROUTE_ROWS = 64
META_ROWS = 8

SORT_TILE = 256
RUN_CHUNK = SUBLANES
LOCAL_CAP = 768
MAX_CHUNKS = LOCAL_CAP // RUN_CHUNK
EXPERT_ROWS = 256
ZERO_CHUNK = EXPERT_ROWS
MAX_PAIRS = MAX_CHUNKS // 2
TABLE_LEN = 256
TAB_PAIR_LOC, TAB_PAIR_GLB = 0, MAX_PAIRS
TAB_ODD_LOC, TAB_ODD_GLB = 2 * MAX_PAIRS, 2 * MAX_PAIRS + N_EXPERTS
TAB_PAIRS = 2 * MAX_PAIRS + 2 * N_EXPERTS
TAB_ODDS, TAB_UNITS = TAB_PAIRS + 1, TAB_PAIRS + 2
assert LOCAL_CAP >= TOP_K * SORT_TILE + N_EXPERTS * (RUN_CHUNK - 1)
assert TAB_UNITS < TABLE_LEN

VMEM_LIMIT = 56 * 1024 * 1024

f32 = jnp.float32
bf16 = jnp.bfloat16
i32 = jnp.int32
u32 = jnp.uint32


def _dot(a, b):
    return jnp.dot(a, b, preferred_element_type=f32)


def _dot_tn(a, b):
    return lax.dot_general(a, b, (((0,), (0,)), ((), ())), preferred_element_type=f32)


def _dot_nt(a, b):
    return lax.dot_general(a, b, (((1,), (1,)), ((), ())), preferred_element_type=f32)


def _rms(x):
    return lax.rsqrt(jnp.mean(x * x, axis=-1, keepdims=True) + EPS)


def _pack_rows(v):
    lo = lax.bitcast_convert_type(v[:, :HALF], u32)
    hi = lax.bitcast_convert_type(v[:, HALF:], u32)
    return lax.shift_right_logical(lo, jnp.uint32(16)) | (hi & jnp.uint32(0xFFFF0000))


def _unpack_rows(w):
    lo = lax.bitcast_convert_type(lax.shift_left(w, jnp.uint32(16)), f32)
    hi = lax.bitcast_convert_type(w & jnp.uint32(0xFFFF0000), f32)
    return jnp.concatenate([lo, hi], axis=-1).astype(bf16)


def _const_spec(shape):
    return pl.BlockSpec(shape, lambda *_: (0,) * len(shape))


def _route(x1, norm_g, wr_t, br, meta_ref, cnt_ref):
    n = x1.shape[0]
    hn = (x1 * _rms(x1) * norm_g).astype(bf16)
    lt = _dot_nt(wr_t, hn) + br
    rows = lax.broadcasted_iota(i32, (SUBLANES, n), 0)
    neg = jnp.float32(-jnp.inf)
    lg = jnp.where(rows < N_GROUPS, lt[0:SUBLANES], neg)
    gmax = jnp.max(lg, axis=0, keepdims=True)
    gidx = jnp.min(jnp.where(lg == gmax, rows, SUBLANES), axis=0, keepdims=True)
    g_w = 1.0 / jnp.sum(jnp.exp(lg - gmax), axis=0, keepdims=True)
    chosen = jnp.zeros((SUBLANES, n), f32)
    for g in range(N_GROUPS):
        chosen = jnp.where(gidx == g, lt[SUBLANES * (g + 1):SUBLANES * (g + 2)], chosen)
    m1 = jnp.max(chosen, axis=0, keepdims=True)
    i1 = jnp.min(jnp.where(chosen == m1, rows, SUBLANES), axis=0, keepdims=True)
    rest = jnp.where(rows == i1, neg, chosen)
    m2 = jnp.max(rest, axis=0, keepdims=True)
    i2 = jnp.min(jnp.where(rest == m2, rows, SUBLANES), axis=0, keepdims=True)
    t = jnp.exp(m2 - m1)
    s1 = 1.0 / (1.0 + t)
    s2 = t / (1.0 + t)
    e1 = gidx * EXPERTS_PER_GROUP + i1
    e2 = gidx * EXPERTS_PER_GROUP + i2
    zero = jnp.zeros((1, n), f32)
    meta_ref[...] = jnp.concatenate(
        [e1.astype(f32), e2.astype(f32), g_w * s1, g_w * s2, zero, zero, zero, zero], axis=0)
    ids = lax.broadcasted_iota(i32, (N_EXPERTS, n), 0)
    hits = (ids == e1).astype(f32) + (ids == e2).astype(f32)
    for u in range(n // SORT_TILE):
        cnt_ref[u * N_EXPERTS:(u + 1) * N_EXPERTS, :] = jnp.sum(
            hits[:, u * SORT_TILE:(u + 1) * SORT_TILE], axis=1, keepdims=True)


def _router_params(w_group, b_group, w_sub, b_sub):
    tail = ROUTE_ROWS - SUBLANES - N_EXPERTS
    wrt = jnp.concatenate([
        w_group.T, jnp.zeros((SUBLANES - N_GROUPS, D_MODEL), f32),
        jnp.transpose(w_sub, (0, 2, 1)).reshape(N_EXPERTS, D_MODEL),
        jnp.zeros((tail, D_MODEL), f32)], axis=0)
    br = jnp.concatenate([b_group, jnp.zeros((SUBLANES - N_GROUPS,), f32), b_sub.reshape(N_EXPERTS),
                          jnp.zeros((tail,), f32)]).reshape(ROUTE_ROWS, 1)
    return wrt.astype(bf16), br


def _route_out_specs(ts, index):
    tiles = ts // SORT_TILE
    return [pl.BlockSpec((ts, D_MODEL), lambda *g: (index(*g), 0)),
            pl.BlockSpec((META_ROWS, ts), lambda *g: (0, index(*g))),
            pl.BlockSpec((tiles * N_EXPERTS, 1), lambda *g: (index(*g), 0))]


def _route_out_shapes(t):
    return [jax.ShapeDtypeStruct((t, D_MODEL), f32),
            jax.ShapeDtypeStruct((META_ROWS, t), f32),
            jax.ShapeDtypeStruct((t // SORT_TILE * N_EXPERTS, 1), f32)]


def _gla_kernel(x_ref, ng_ref, wq_ref, wk_ref, wv_ref, wr_ref, wg_ref, wgu_ref, bg_ref,
                hg_ref, wo_ref, mng_ref, wrt_ref, br_ref,
                x1_ref, meta_ref, cnt_ref, st_ref, o_ref):
    ts = x_ref.shape[1]

    @pl.when(pl.program_id(1) == 0)
    def _():
        st_ref[...] = jnp.zeros_like(st_ref)

    x = x_ref[0]
    h = (x * _rms(x) * ng_ref[...]).astype(bf16)
    q = _dot(h, wq_ref[...]) * (GLA_DK ** -0.5)
    k = _dot(h, wk_ref[...])
    v = _dot(h, wv_ref[...]).astype(bf16)
    glr = _dot(h, wg_ref[...]).astype(bf16)
    r = _dot(h, wr_ref[...])
    gp = _dot(glr, wgu_ref[...]) + bg_ref[...]
    log_a = (jnp.minimum(gp, 0.0) - jnp.log(1.0 + jnp.exp(-jnp.abs(gp)))) * (1.0 / GLA_TAU)

    row = lax.broadcasted_iota(i32, (CHUNK, GLA_HK), 0)
    for c in range(ts // CHUNK):
        rs = slice(c * CHUNK, (c + 1) * CHUNK)
        b = log_a[rs]
        sh = 1
        while sh < CHUNK:
            b = b + jnp.where(row >= sh, pltpu.roll(b, sh, axis=0), 0.0)
            sh *= 2
        b_end = b[CHUNK - 1:CHUNK]
        kdec = (k[rs] * jnp.exp(b_end - b)).astype(bf16)
        decay = jnp.exp(b_end)
        qc = q[rs].astype(bf16)
        vc = v[rs]
        for hd in range(GLA_HEADS):
            ks = slice(hd * GLA_DK, (hd + 1) * GLA_DK)
            vs = slice(hd * GLA_DV, (hd + 1) * GLA_DV)
            st = st_ref[hd] * decay[:, ks] + _dot_tn(vc[:, vs], kdec[:, ks])
            st_ref[hd] = st
            o_ref[rs, vs] = _dot_nt(qc[:, ks], st.astype(bf16))

    gated = []
    for hd in range(GLA_HEADS):
        vs = slice(hd * GLA_DV, (hd + 1) * GLA_DV)
        oh = o_ref[:, vs]
        rh = r[:, vs]
        gated.append(oh * _rms(oh) * hg_ref[:, vs] * (rh / (1.0 + jnp.exp(-rh))))
    y = _dot(jnp.concatenate(gated, axis=-1).astype(bf16), wo_ref[...])
    x1 = x + y
    x1_ref[...] = x1
    _route(x1, mng_ref[...], wrt_ref[...], br_ref[...], meta_ref, cnt_ref)


def _gla_layer(x, norm_g, w_in, w_gate_up, b_gate, head_g, w_out, moe_norm_g, wrt, br, ts):
    bsz, seq, _ = x.shape
    t = bsz * seq
    wq = w_in[:, 0:GLA_HK].astype(bf16)
    wk = w_in[:, GLA_HK:2 * GLA_HK].astype(bf16)
    wv = w_in[:, 2 * GLA_HK:2 * GLA_HK + GLA_HV].astype(bf16)
    wr = w_in[:, 2 * GLA_HK + GLA_HV:2 * GLA_HK + 2 * GLA_HV].astype(bf16)
    wg = jnp.pad(w_in[:, 2 * GLA_HK + 2 * GLA_HV:], ((0, 0), (0, LANES - GLA_GATE_RANK))).astype(bf16)
    wgu = jnp.pad(w_gate_up, ((0, LANES - GLA_GATE_RANK), (0, 0))).astype(bf16)
    n_s = seq // ts
    args = (x, norm_g.reshape(1, D_MODEL), wq, wk, wv, wr, wg, wgu, b_gate.reshape(1, GLA_HK),
            head_g.reshape(1, GLA_HV), w_out.astype(bf16), moe_norm_g.reshape(1, D_MODEL), wrt, br)
    in_specs = [pl.BlockSpec((1, ts, D_MODEL), lambda b, s: (b, s, 0))]
    in_specs += [_const_spec(a.shape) for a in args[1:]]
    return pl.pallas_call(
        _gla_kernel,
        grid=(bsz, n_s),
        in_specs=in_specs,
        out_specs=_route_out_specs(ts, lambda b, s: b * n_s + s),
        out_shape=_route_out_shapes(t),
        scratch_shapes=[pltpu.VMEM((GLA_HEADS, GLA_DV, GLA_DK), f32),
                        pltpu.VMEM((ts, GLA_HV), f32)],
        compiler_params=pltpu.CompilerParams(
            dimension_semantics=("arbitrary", "arbitrary"), vmem_limit_bytes=VMEM_LIMIT),
        name="gla_mixer",
    )(*args)


def _n_expert_blocks(t):
    tiles = t // SORT_TILE
    worst = t * TOP_K + tiles * N_EXPERTS * (RUN_CHUNK - 1) + N_EXPERTS * (EXPERT_ROWS - 1)
    return -(-worst // EXPERT_ROWS)


def _zero_table_len(t):
    tail = _n_expert_blocks(t) * EXPERT_ROWS - t * TOP_K
    n = N_EXPERTS + -(-tail // ZERO_CHUNK) + 1
    return -(-(n + 1) // LANES) * LANES


def _cumsum(x, axis):
    x = jnp.moveaxis(x, axis, -1)
    n = x.shape[-1]
    upto = jnp.arange(n, dtype=i32)[:, None] <= jnp.arange(n, dtype=i32)[None, :]
    return jnp.moveaxis(jnp.sum(x[..., :, None] * upto.astype(i32), axis=-2), -1, axis)


def _flat_chunks(n_per, max_n):
    cum = _cumsum(n_per, -1)
    c = jnp.arange(max_n, dtype=i32)
    seg = jnp.minimum(jnp.sum((c[:, None] >= cum[..., None, :]).astype(i32), axis=-1), n_per.shape[-1] - 1)
    onehot = (seg[..., None] == jnp.arange(n_per.shape[-1], dtype=i32)).astype(i32)
    return onehot, c - _pick(onehot, cum - n_per), cum[..., -1]


def _pick(onehot, per_segment):
    return jnp.sum(onehot * per_segment[..., None, :], axis=-1)


def _plan(cnt, t):
    tiles = t // SORT_TILE
    cnt = cnt.reshape(tiles, N_EXPERTS).astype(i32)
    nch = (cnt + RUN_CHUNK - 1) // RUN_CHUNK
    run = nch * RUN_CHUNK
    counts = jnp.sum(run, axis=0)
    padded = ((counts + EXPERT_ROWS - 1) // EXPERT_ROWS) * EXPERT_ROWS
    pad_end = _cumsum(padded, 0)
    pad_start = pad_end - padded
    base = pad_start[None, :] + _cumsum(run, 0) - run
    n_pair = nch // 2
    off = _cumsum(run, 1) - run
    onehot, j, n_pairs = _flat_chunks(n_pair, MAX_PAIRS)
    pair_loc = _pick(onehot, off) + j * (2 * RUN_CHUNK)
    pair_glb = _pick(onehot, base) + j * (2 * RUN_CHUNK)
    onehot, _, n_odds = _flat_chunks(nch % 2, N_EXPERTS)
    odd_loc = _pick(onehot, off + n_pair * (2 * RUN_CHUNK))
    odd_glb = _pick(onehot, base + n_pair * (2 * RUN_CHUNK))
    units = jnp.sum(nch, axis=1)
    fill = jnp.zeros((tiles, TABLE_LEN - TAB_UNITS - 1), i32)
    table = jnp.concatenate([pair_loc, pair_glb, odd_loc, odd_glb, n_pairs[:, None], n_odds[:, None],
                             units[:, None], fill], axis=1).reshape(tiles, 1, TABLE_LEN)
    off_col = off.astype(f32).reshape(tiles, N_EXPERTS, 1)

    nb = _n_expert_blocks(t)
    n_blk = padded // EXPERT_ROWS
    n_big = n_blk // BIG_BLOCKS
    n_items = n_big + n_blk % BIG_BLOCKS
    ionehot, k, n_total = _flat_chunks(n_items, _n_work_items(t))
    k_big = _pick(ionehot, n_big)
    item_big = (k < k_big).astype(i32)
    item_start = _pick(ionehot, pad_start // EXPERT_ROWS) + jnp.where(
        k < k_big, BIG_BLOCKS * k, BIG_BLOCKS * k_big + k - k_big)
    icum = _cumsum(n_items, 0)
    item_first = jnp.concatenate([icum - n_items, n_total[None]])
    items = (item_first, item_start, item_big)

    region_end = jnp.concatenate([pad_start[1:], jnp.full((1,), nb * EXPERT_ROWS, i32)])
    zlen = region_end - (pad_start + counts)
    nz = (zlen + ZERO_CHUNK - 1) // ZERO_CHUNK
    zl = _zero_table_len(t)
    zonehot, zj, nztot = _flat_chunks(nz, zl - 1)
    zstart = _pick(zonehot, region_end) - (zj + 1) * ZERO_CHUNK
    ztable = jnp.concatenate([zstart, nztot[None]]).reshape(1, 1, zl)
    return table, off_col, items, ztable


ISSUE_UNROLL = 4


def _start_runs(tab, u, make):
    for rows, n, loc0, glb0 in ((2 * RUN_CHUNK, tab[u, 0, TAB_PAIRS], TAB_PAIR_LOC, TAB_PAIR_GLB),
                                (RUN_CHUNK, tab[u, 0, TAB_ODDS], TAB_ODD_LOC, TAB_ODD_GLB)):
        def start(c, priority):
            make(rows, tab[u, 0, loc0 + c], tab[u, 0, glb0 + c]).start(priority=priority)

        def four(k, carry):
            for q in range(ISSUE_UNROLL):
                start(ISSUE_UNROLL * k + q, q % 2)
            return carry
        lax.fori_loop(0, n // ISSUE_UNROLL, four, 0)
        lax.fori_loop(n - n % ISSUE_UNROLL, n, lambda c, carry: (start(c, 0), carry)[1], 0)


def _wait_runs(units, make):
    left = units
    for per_wait in (16, 2, 1):
        lax.fori_loop(0, left // per_wait,
                      lambda c, carry: (make(per_wait * RUN_CHUNK, 0, 0).wait(), carry)[1], 0)
        left = left % per_wait


def _dispatch_kernel(tab_ref, ztab_ref, x_ref, meta_ref, offc_ref, ng_ref,
                     out_hbm, cmeta_ref, stage, zbuf, sems, zsem, prev_n):
    t = pl.program_id(0)
    slot = t % 2
    zl = ztab_ref.shape[2]

    def zero_copy(z):
        dst = pl.multiple_of(ztab_ref[0, 0, z], RUN_CHUNK)
        return pltpu.make_async_copy(zbuf, out_hbm.at[pl.ds(dst, ZERO_CHUNK), :], zsem)

    @pl.when(t == 0)
    def _():
        zbuf[...] = jnp.zeros_like(zbuf)
        nz = ztab_ref[0, 0, zl - 1]
        lax.fori_loop(0, nz, lambda z, c: (zero_copy(z).start(), c)[1], 0)
        lax.fori_loop(0, nz, lambda z, c: (zero_copy(z).wait(), c)[1], 0)

    tiles = x_ref.shape[0] // SORT_TILE
    ids = lax.broadcasted_iota(i32, (N_EXPERTS, SORT_TILE), 0)
    before = (lax.broadcasted_iota(i32, (SORT_TILE, SORT_TILE), 0)
              < lax.broadcasted_iota(i32, (SORT_TILE, SORT_TILE), 1)).astype(bf16)
    p = lax.broadcasted_iota(i32, (LOCAL_CAP, SORT_TILE), 0).astype(f32)
    zero = jnp.zeros((1, SORT_TILE), f32)
    for u in range(tiles):
        cs = slice(u * SORT_TILE, (u + 1) * SORT_TILE)
        x = x_ref[cs, :]
        hn = (x * _rms(x) * ng_ref[...]).astype(bf16)
        oh0 = ids == meta_ref[0:1, cs].astype(i32)
        oh1 = ids == meta_ref[1:2, cs].astype(i32)
        ahead = _dot(jnp.concatenate([oh0.astype(bf16), oh1.astype(bf16)], axis=0), before)
        c0, c1 = ahead[:N_EXPERTS], ahead[N_EXPERTS:]
        n0 = jnp.sum(oh0.astype(f32), axis=1, keepdims=True)
        offc = offc_ref[u]
        pos0 = jnp.sum(jnp.where(oh0, offc + c0, 0.0), axis=0, keepdims=True)
        pos1 = jnp.sum(jnp.where(oh1, offc + n0 + c1, 0.0), axis=0, keepdims=True)
        cmeta_ref[:, cs] = jnp.concatenate(
            [pos0, pos1, meta_ref[2:3, cs], meta_ref[3:4, cs], zero, zero, zero, zero], axis=0)
        perm = jnp.logical_or(p == pos0, p == pos1).astype(bf16)
        stage[slot, pl.ds(u * LOCAL_CAP, LOCAL_CAP), :] = _pack_rows(_dot(perm, hn))

    def copy_from(s, u):
        def make(rows, loc, glb):
            src = pl.multiple_of(u * LOCAL_CAP + loc, RUN_CHUNK)
            return pltpu.make_async_copy(stage.at[s, pl.ds(src, rows), :],
                                         out_hbm.at[pl.ds(pl.multiple_of(glb, RUN_CHUNK), rows), :], sems.at[s])
        return make

    @pl.when(t > 0)
    def _():
        _wait_runs(prev_n[0], copy_from(1 - slot, 0))

    units = 0
    for u in range(tiles):
        _start_runs(tab_ref, u, copy_from(slot, u))
        units = units + tab_ref[u, 0, TAB_UNITS]
    prev_n[0] = units

    @pl.when(t == pl.num_programs(0) - 1)
    def _():
        _wait_runs(units, copy_from(slot, 0))


def _dispatch(x, meta, table, off_col, ztable, norm_g, ts):
    t = x.shape[0]
    tiles = ts // SORT_TILE
    zl = ztable.shape[2]
    nb = _n_expert_blocks(t)
    return pl.pallas_call(
        _dispatch_kernel,
        grid=(t // ts,),
        in_specs=[
            pl.BlockSpec((tiles, 1, TABLE_LEN), lambda i: (i, 0, 0), memory_space=pltpu.SMEM),
            pl.BlockSpec((1, 1, zl), lambda i: (0, 0, 0), memory_space=pltpu.SMEM),
            pl.BlockSpec((ts, D_MODEL), lambda i: (i, 0)),
            pl.BlockSpec((META_ROWS, ts), lambda i: (0, i)),
            pl.BlockSpec((tiles, N_EXPERTS, 1), lambda i: (i, 0, 0)),
            _const_spec((1, D_MODEL)),
        ],
        out_specs=[pl.BlockSpec(memory_space=pl.ANY),
                   pl.BlockSpec((META_ROWS, ts), lambda i: (0, i))],
        out_shape=[jax.ShapeDtypeStruct((nb * EXPERT_ROWS, HALF), u32),
                   jax.ShapeDtypeStruct((META_ROWS, t), f32)],
        scratch_shapes=[pltpu.VMEM((2, tiles * LOCAL_CAP, HALF), u32),
                        pltpu.VMEM((ZERO_CHUNK, HALF), u32),
                        pltpu.SemaphoreType.DMA((2,)),
                        pltpu.SemaphoreType.DMA(()),
                        pltpu.SMEM((1,), i32)],
        compiler_params=pltpu.CompilerParams(
            dimension_semantics=("arbitrary",), vmem_limit_bytes=VMEM_LIMIT),
        name="moe_dispatch",
    )(table, ztable, x, meta, off_col, norm_g.reshape(1, D_MODEL))


X_BUFFERS = 3
Y_BUFFERS = 2
BIG_BLOCKS = 4


def _n_work_items(t):
    return _n_expert_blocks(t) // BIG_BLOCKS + (BIG_BLOCKS - 1) * N_EXPERTS + 1


def _expert_kernel(ifirst_ref, istart_ref, ibig_ref, w1_ref, w3_ref, w2_ref, xs_hbm, ys_hbm,
                   xbuf, ybuf, xsem, ysem, w1b, w3b, w2b):
    e = pl.program_id(0)
    lo = ifirst_ref[e]
    hi = ifirst_ref[e + 1]
    total = ifirst_ref[N_EXPERTS]

    def rows_of(big):
        return (BIG_BLOCKS if big else 1) * EXPERT_ROWS

    def hbm_rows(ref, i, big):
        return ref.at[pl.ds(pl.multiple_of(istart_ref[i] * EXPERT_ROWS, EXPERT_ROWS), rows_of(big)), :]

    def x_copy(i, big):
        slot = i % X_BUFFERS
        return pltpu.make_async_copy(hbm_rows(xs_hbm, i, big),
                                     xbuf.at[slot, pl.ds(0, rows_of(big)), :], xsem.at[slot])

    def y_copy(i, big):
        slot = i % Y_BUFFERS
        return pltpu.make_async_copy(ybuf.at[slot, pl.ds(0, rows_of(big)), :],
                                     hbm_rows(ys_hbm, i, big), ysem.at[slot])

    def by_size(i, fn):
        @pl.when(ibig_ref[i] != 0)
        def _():
            fn(True)

        @pl.when(ibig_ref[i] == 0)
        def _():
            fn(False)

    @pl.when(e == 0)
    def _():
        for k in range(X_BUFFERS - 1):
            @pl.when(k < total)
            def _():
                by_size(k, lambda big: x_copy(k, big).start())

    @pl.when(hi > lo)
    def _():
        w1b[...] = w1_ref[0, 0].astype(bf16)
        w3b[...] = w3_ref[0, 0].astype(bf16)
        w2b[...] = w2_ref[0, 0].astype(bf16)

    def item(i, carry):
        ahead = i + X_BUFFERS - 1

        @pl.when(ahead < total)
        def _():
            by_size(ahead, lambda big: x_copy(ahead, big).start())

        def run(big):
            rows = rows_of(big)
            x_copy(i, big).wait()
            hn = _unpack_rows(xbuf[i % X_BUFFERS, pl.ds(0, rows), :])
            h1 = _dot(hn, w1b[...])
            h3 = _dot(hn, w3b[...])
            hid = ((h1 / (1.0 + jnp.exp(-h1))) * h3).astype(bf16)
            y = _dot(hid, w2b[...])

            @pl.when(i >= Y_BUFFERS)
            def _():
                by_size(i - Y_BUFFERS, lambda b: y_copy(i - Y_BUFFERS, b).wait())

            ybuf[i % Y_BUFFERS, pl.ds(0, rows), :] = _pack_rows(y.astype(bf16).astype(f32))
            y_copy(i, big).start()

        by_size(i, run)
        return carry

    lax.fori_loop(lo, hi, item, 0)

    @pl.when(e == pl.num_programs(0) - 1)
    def _():
        for k in range(1, Y_BUFFERS + 1):
            @pl.when(total >= k)
            def _():
                by_size(total - k, lambda big: y_copy(total - k, big).wait())


def _experts(xs, item_first, item_start, item_big, w1, w3, w2, layer):
    w_index = lambda e, *_: (layer, e, 0, 0)
    big_rows = BIG_BLOCKS * EXPERT_ROWS
    grid_spec = pltpu.PrefetchScalarGridSpec(
        num_scalar_prefetch=3,
        grid=(N_EXPERTS,),
        in_specs=[
            pl.BlockSpec((1, 1, D_MODEL, EXPERT_FF), w_index),
            pl.BlockSpec((1, 1, D_MODEL, EXPERT_FF), w_index),
            pl.BlockSpec((1, 1, EXPERT_FF, D_MODEL), w_index),
            pl.BlockSpec(memory_space=pl.ANY),
        ],
        out_specs=pl.BlockSpec(memory_space=pl.ANY),
        scratch_shapes=[pltpu.VMEM((X_BUFFERS, big_rows, HALF), u32),
                        pltpu.VMEM((Y_BUFFERS, big_rows, HALF), u32),
                        pltpu.SemaphoreType.DMA((X_BUFFERS,)),
                        pltpu.SemaphoreType.DMA((Y_BUFFERS,)),
                        pltpu.VMEM((D_MODEL, EXPERT_FF), bf16),
                        pltpu.VMEM((D_MODEL, EXPERT_FF), bf16),
                        pltpu.VMEM((EXPERT_FF, D_MODEL), bf16)],
    )
    return pl.pallas_call(
        _expert_kernel,
        grid_spec=grid_spec,
        out_shape=jax.ShapeDtypeStruct(xs.shape, u32),
        input_output_aliases={6: 0},
        compiler_params=pltpu.CompilerParams(
            dimension_semantics=("arbitrary",), vmem_limit_bytes=VMEM_LIMIT),
        name="moe_experts",
    )(item_first, item_start, item_big, w1, w3, w2, xs)


def _combine(x_ref, tab_ref, tab_next_ref, cmeta_ref, ys_hbm, ybuf, sems):
    i = pl.program_id(0)
    n_steps = pl.num_programs(0)
    tiles = x_ref.shape[0] // SORT_TILE
    slot = i % 2

    def copy_into(s, u):
        def make(rows, loc, glb):
            dst = pl.multiple_of(u * LOCAL_CAP + loc, RUN_CHUNK)
            return pltpu.make_async_copy(ys_hbm.at[pl.ds(pl.multiple_of(glb, RUN_CHUNK), rows), :],
                                         ybuf.at[s, pl.ds(dst, rows), :], sems.at[s])
        return make

    def fetch(tab, s):
        for u in range(tiles):
            _start_runs(tab, u, copy_into(s, u))

            def clear(c, carry):
                dst = pl.multiple_of(u * LOCAL_CAP + c * RUN_CHUNK, RUN_CHUNK)
                ybuf[s, pl.ds(dst, RUN_CHUNK), :] = jnp.zeros((RUN_CHUNK, HALF), u32)
                return carry
            lax.fori_loop(tab[u, 0, TAB_UNITS], MAX_CHUNKS, clear, 0)

    @pl.when(i == 0)
    def _():
        fetch(tab_ref, 0)

    @pl.when(i + 1 < n_steps)
    def _():
        fetch(tab_next_ref, 1 - slot)

    for u in range(tiles):
        _wait_runs(tab_ref[u, 0, TAB_UNITS], copy_into(slot, u))

    p = lax.broadcasted_iota(i32, (LOCAL_CAP, SORT_TILE), 0).astype(f32)
    parts = []
    for u in range(tiles):
        cs = slice(u * SORT_TILE, (u + 1) * SORT_TILE)
        pos0, pos1 = cmeta_ref[0:1, cs], cmeta_ref[1:2, cs]
        g = (jnp.where(p == pos0, cmeta_ref[2:3, cs], 0.0)
             + jnp.where(p == pos1, cmeta_ref[3:4, cs], 0.0))
        g_hi = g.astype(bf16)
        g_lo = (g - g_hi.astype(f32)).astype(bf16)
        y = _unpack_rows(ybuf[slot, pl.ds(u * LOCAL_CAP, LOCAL_CAP), :])
        both = _dot_tn(jnp.concatenate([g_hi, g_lo], axis=1), y)
        parts.append(x_ref[cs, :] + both[:SORT_TILE] + both[SORT_TILE:])
    return parts[0] if tiles == 1 else jnp.concatenate(parts, axis=0)


def _combine_specs(ts, n_steps):
    tiles = ts // SORT_TILE
    return [
        pl.BlockSpec((ts, D_MODEL), lambda i: (i, 0)),
        pl.BlockSpec((tiles, 1, TABLE_LEN), lambda i: (i, 0, 0), memory_space=pltpu.SMEM),
        pl.BlockSpec((tiles, 1, TABLE_LEN), lambda i: (jnp.minimum(i + 1, n_steps - 1), 0, 0),
                     memory_space=pltpu.SMEM),
        pl.BlockSpec((META_ROWS, ts), lambda i: (0, i)),
        pl.BlockSpec(memory_space=pl.ANY),
    ]


def _combine_scratch(ts):
    tiles = ts // SORT_TILE
    return [pltpu.VMEM((2, tiles * LOCAL_CAP, HALF), u32), pltpu.SemaphoreType.DMA((2,))]


def _gelu(z):
    return 0.5 * z * (1.0 + lax.erf(z * (2.0 ** -0.5)))


def _sgu_kernel(x_ref, ng_ref, wu_ref, wv_ref, lng_ref, lnb_ref,
                ws_ref, bs_ref, wo_ref, mng_ref, wrt_ref, br_ref,
                x3_ref, meta_ref, cnt_ref, vn_ref):
    ts = x_ref.shape[0]
    x2 = x_ref[...]
    h = (x2 * _rms(x2) * ng_ref[...]).astype(bf16)
    v = _gelu(_dot(h, wv_ref[...]))
    mu = jnp.mean(v, axis=-1, keepdims=True)
    vc = v - mu
    rstd = lax.rsqrt(jnp.mean(vc * vc, axis=-1, keepdims=True) + EPS)
    vn_ref[...] = (vc * rstd * lng_ref[...] + lnb_ref[...]).astype(bf16)

    pos = lax.broadcasted_iota(i32, (SGU_BLOCK, SGU_BLOCK), 0) // CHUNK
    src = lax.broadcasted_iota(i32, (SGU_BLOCK, SGU_BLOCK), 1) // CHUNK
    acc = x2
    for g in range(SGU_GROUPS):
        cs = slice(g * SGU_GC, (g + 1) * SGU_GC)
        ws = jnp.where(pos >= src, ws_ref[g], jnp.zeros((), bf16))
        u = _gelu(_dot(h, wu_ref[:, cs]))
        mixed = [_dot(ws, vn_ref[nb * SGU_BLOCK:(nb + 1) * SGU_BLOCK, cs]) + bs_ref[:, g:g + 1]
                 for nb in range(ts // SGU_BLOCK)]
        out = (u * jnp.concatenate(mixed, axis=0)).astype(bf16)
        acc = acc + _dot(out, wo_ref[cs, :])
    x3_ref[...] = acc
    _route(acc, mng_ref[...], wrt_ref[...], br_ref[...], meta_ref, cnt_ref)


def _sgu_layer(x2, norm_g, w_in, ln_g, ln_b, w_s, b_s, w_out, moe_norm_g, wrt, br, ts):
    t = x2.shape[0]
    n_steps = t // ts
    args = (norm_g.reshape(1, D_MODEL), w_in[:, :SGU_HALF].astype(bf16), w_in[:, SGU_HALF:].astype(bf16),
            ln_g.reshape(1, SGU_HALF), ln_b.reshape(1, SGU_HALF), w_s.astype(bf16), b_s.T,
            w_out.astype(bf16), moe_norm_g.reshape(1, D_MODEL), wrt, br)
    return pl.pallas_call(
        _sgu_kernel,
        grid=(n_steps,),
        in_specs=[pl.BlockSpec((ts, D_MODEL), lambda i: (i, 0))] + [_const_spec(a.shape) for a in args],
        out_specs=_route_out_specs(ts, lambda i: i),
        out_shape=_route_out_shapes(t),
        scratch_shapes=[pltpu.VMEM((ts, SGU_HALF), bf16)],
        compiler_params=pltpu.CompilerParams(
            dimension_semantics=("arbitrary",), vmem_limit_bytes=VMEM_LIMIT),
        name="sgu_mixer",
    )(x2, *args)


def _combine_kernel(x_ref, tab_ref, tab_next_ref, cmeta_ref, ys_hbm, out_ref, ybuf, sems):
    out_ref[...] = _combine(x_ref, tab_ref, tab_next_ref, cmeta_ref, ys_hbm, ybuf, sems)


def _final_kernel(x_ref, tab_ref, tab_next_ref, cmeta_ref, ys_hbm, ng_ref, out_ref, ybuf, sems):
    x = _combine(x_ref, tab_ref, tab_next_ref, cmeta_ref, ys_hbm, ybuf, sems)
    out_ref[...] = x * _rms(x) * ng_ref[...]


def _combine_layer(x, table, cmeta, ys, norm_g, ts):
    t = x.shape[0]
    n_steps = t // ts
    final = norm_g is not None
    extra = (norm_g.reshape(1, D_MODEL),) if final else ()
    return pl.pallas_call(
        _final_kernel if final else _combine_kernel,
        grid=(n_steps,),
        in_specs=_combine_specs(ts, n_steps) + [_const_spec(a.shape) for a in extra],
        out_specs=pl.BlockSpec((ts, D_MODEL), lambda i: (i, 0)),
        out_shape=jax.ShapeDtypeStruct((t, D_MODEL), f32),
        scratch_shapes=_combine_scratch(ts),
        compiler_params=pltpu.CompilerParams(
            dimension_semantics=("arbitrary",), vmem_limit_bytes=VMEM_LIMIT),
        name="final_norm" if final else "moe_combine",
    )(x, table, table, cmeta, ys, *extra)


def _moe(x, meta, cnt, norm_g, w1, w3, w2, layer, ts):
    table, off_col, items, ztable = _plan(cnt, x.shape[0])
    sorted_x, cmeta = _dispatch(x, meta, table, off_col, ztable, norm_g, ts)
    return _experts(sorted_x, *items, w1, w3, w2, layer), table, cmeta


def _forward(x, gla_norm, gla_w_in, gla_w_gate_up, gla_b_gate, gla_head_g, gla_w_out, sgu_norm, sgu_w_in,
             sgu_ln_g, sgu_ln_b, sgu_w_s, sgu_b_s, sgu_w_out, moe_norm, moe_w_group, moe_b_group,
             moe_w_sub, moe_b_sub, moe_w1, moe_w3, moe_w2, final_norm, *, ts_gla, ts_sgu, ts_fin):
    wrt0, br0 = _router_params(moe_w_group[0], moe_b_group[0], moe_w_sub[0], moe_b_sub[0])
    wrt1, br1 = _router_params(moe_w_group[1], moe_b_group[1], moe_w_sub[1], moe_b_sub[1])
    x1, meta0, cnt0 = _gla_layer(x, gla_norm[0], gla_w_in[0], gla_w_gate_up[0], gla_b_gate[0], gla_head_g[0],
                                 gla_w_out[0], moe_norm[0], wrt0, br0, ts_gla)
    ys0, table0, cmeta0 = _moe(x1, meta0, cnt0, moe_norm[0], moe_w1, moe_w3, moe_w2, 0, ts_fin)
    x2 = _combine_layer(x1, table0, cmeta0, ys0, None, ts_fin)
    x3, meta1, cnt1 = _sgu_layer(x2, sgu_norm[0], sgu_w_in[0], sgu_ln_g[0], sgu_ln_b[0],
                                 sgu_w_s[0], sgu_b_s[0], sgu_w_out[0], moe_norm[1], wrt1, br1, ts_sgu)
    ys1, table1, cmeta1 = _moe(x3, meta1, cnt1, moe_norm[1], moe_w1, moe_w3, moe_w2, 1, ts_fin)
    out = _combine_layer(x3, table1, cmeta1, ys1, final_norm, ts_fin)
    return out.reshape(x.shape)


def kernel(x, gla_norm, gla_w_in, gla_w_gate_up, gla_b_gate, gla_head_g, gla_w_out, sgu_norm, sgu_w_in, sgu_ln_g, sgu_ln_b, sgu_w_s, sgu_b_s, sgu_w_out, moe_norm, moe_w_group, moe_b_group, moe_w_sub, moe_b_sub, moe_w1, moe_w3, moe_w2, final_norm):
    return _forward(x, gla_norm, gla_w_in, gla_w_gate_up, gla_b_gate, gla_head_g, gla_w_out, sgu_norm,
                    sgu_w_in, sgu_ln_g, sgu_ln_b, sgu_w_s, sgu_b_s, sgu_w_out, moe_norm, moe_w_group,
                    moe_b_group, moe_w_sub, moe_b_sub, moe_w1, moe_w3, moe_w2, final_norm,
                    ts_gla=1024, ts_sgu=1024, ts_fin=1024)
```

```python
import jax
import jax.numpy as jnp
from jax import lax
from jax.experimental import pallas as pl
from jax.experimental.pallas import tpu as pltpu

D_MODEL = 1024
HALF = D_MODEL // 2
EPS = 1e-6
LANES = 128
SUBLANES = 8

CHUNK = 64
GLA_HEADS = 4
GLA_DK = 128
GLA_DV = 256
GLA_HK = GLA_HEADS * GLA_DK
GLA_HV = GLA_HEADS * GLA_DV
GLA_GATE_RANK = 16
GLA_TAU = 16.0

SGU_BLOCK = 128
SGU_GROUPS = 4
SGU_HALF = 2048
SGU_GC = SGU_HALF // SGU_GROUPS

N_GROUPS = 4
EXPERTS_PER_GROUP = 8
N_EXPERTS = N_GROUPS * EXPERTS_PER_GROUP
TOP_K = 2
EXPERT_FF = 512
ROUTE_ROWS = 64
META_ROWS = 8

SORT_TILE = 256
RUN_CHUNK = SUBLANES
LOCAL_CAP = 768
MAX_CHUNKS = LOCAL_CAP // RUN_CHUNK
EXPERT_ROWS = 256
ZERO_CHUNK = EXPERT_ROWS
MAX_PAIRS = MAX_CHUNKS // 2
TABLE_LEN = 256
TAB_PAIR_LOC, TAB_PAIR_GLB = 0, MAX_PAIRS
TAB_ODD_LOC, TAB_ODD_GLB = 2 * MAX_PAIRS, 2 * MAX_PAIRS + N_EXPERTS
TAB_PAIRS = 2 * MAX_PAIRS + 2 * N_EXPERTS
TAB_ODDS, TAB_UNITS = TAB_PAIRS + 1, TAB_PAIRS + 2
assert LOCAL_CAP >= TOP_K * SORT_TILE + N_EXPERTS * (RUN_CHUNK - 1)
assert TAB_UNITS < TABLE_LEN

VMEM_LIMIT = 56 * 1024 * 1024

f32 = jnp.float32
bf16 = jnp.bfloat16
i32 = jnp.int32
u32 = jnp.uint32


def _dot(a, b):
    return jnp.dot(a, b, preferred_element_type=f32)


def _dot_tn(a, b):
    return lax.dot_general(a, b, (((0,), (0,)), ((), ())), preferred_element_type=f32)


def _dot_nt(a, b):
    return lax.dot_general(a, b, (((1,), (1,)), ((), ())), preferred_element_type=f32)


def _rms(x):
    return lax.rsqrt(jnp.mean(x * x, axis=-1, keepdims=True) + EPS)


def _pack_rows(v):
    lo = lax.bitcast_convert_type(v[:, :HALF], u32)
    hi = lax.bitcast_convert_type(v[:, HALF:], u32)
    return lax.shift_right_logical(lo, jnp.uint32(16)) | hi


def _unpack_rows(w):
    lo = lax.bitcast_convert_type(lax.shift_left(w, jnp.uint32(16)), f32)
    hi = lax.bitcast_convert_type(w & jnp.uint32(0xFFFF0000), f32)
    return jnp.concatenate([lo, hi], axis=-1).astype(bf16)


def _const_spec(shape):
    return pl.BlockSpec(shape, lambda *_: (0,) * len(shape))


def _route(x1, norm_g, wr_t, br, meta_ref, cnt_ref, hn_ref):
    n = x1.shape[0]
    hn = (x1 * _rms(x1) * norm_g).astype(bf16)
    hn_ref[...] = hn
    lt = _dot_nt(wr_t, hn) + br
    rows = lax.broadcasted_iota(i32, (SUBLANES, n), 0)
    neg = jnp.float32(-jnp.inf)
    lg = jnp.where(rows < N_GROUPS, lt[0:SUBLANES], neg)
    gmax = jnp.max(lg, axis=0, keepdims=True)
    gidx = jnp.min(jnp.where(lg == gmax, rows, SUBLANES), axis=0, keepdims=True)
    g_w = 1.0 / jnp.sum(jnp.exp(lg - gmax), axis=0, keepdims=True)
    chosen = jnp.zeros((SUBLANES, n), f32)
    for g in range(N_GROUPS):
        chosen = jnp.where(gidx == g, lt[SUBLANES * (g + 1):SUBLANES * (g + 2)], chosen)
    m1 = jnp.max(chosen, axis=0, keepdims=True)
    i1 = jnp.min(jnp.where(chosen == m1, rows, SUBLANES), axis=0, keepdims=True)
    rest = jnp.where(rows == i1, neg, chosen)
    m2 = jnp.max(rest, axis=0, keepdims=True)
    i2 = jnp.min(jnp.where(rest == m2, rows, SUBLANES), axis=0, keepdims=True)
    t = jnp.exp(m2 - m1)
    s1 = 1.0 / (1.0 + t)
    s2 = t / (1.0 + t)
    e1 = gidx * EXPERTS_PER_GROUP + i1
    e2 = gidx * EXPERTS_PER_GROUP + i2
    zero = jnp.zeros((1, n), f32)
    meta_ref[...] = jnp.concatenate(
        [e1.astype(f32), e2.astype(f32), g_w * s1, g_w * s2, zero, zero, zero, zero], axis=0)
    ids = lax.broadcasted_iota(i32, (N_EXPERTS, n), 0)
    hits = (ids == e1).astype(f32) + (ids == e2).astype(f32)
    for u in range(n // SORT_TILE):
        cnt_ref[u * N_EXPERTS:(u + 1) * N_EXPERTS, :] = jnp.sum(
            hits[:, u * SORT_TILE:(u + 1) * SORT_TILE], axis=1, keepdims=True)


def _router_params(w_group, b_group, w_sub, b_sub):
    tail = ROUTE_ROWS - SUBLANES - N_EXPERTS
    wrt = jnp.concatenate([
        w_group.T, jnp.zeros((SUBLANES - N_GROUPS, D_MODEL), f32),
        jnp.transpose(w_sub, (0, 2, 1)).reshape(N_EXPERTS, D_MODEL),
        jnp.zeros((tail, D_MODEL), f32)], axis=0)
    br = jnp.concatenate([b_group, jnp.zeros((SUBLANES - N_GROUPS,), f32), b_sub.reshape(N_EXPERTS),
                          jnp.zeros((tail,), f32)]).reshape(ROUTE_ROWS, 1)
    return wrt.astype(bf16), br


def _route_out_specs(ts, index):
    tiles = ts // SORT_TILE
    return [pl.BlockSpec((ts, D_MODEL), lambda *g: (index(*g), 0)),
            pl.BlockSpec((META_ROWS, ts), lambda *g: (0, index(*g))),
            pl.BlockSpec((tiles * N_EXPERTS, 1), lambda *g: (index(*g), 0)),
            pl.BlockSpec((ts, D_MODEL), lambda *g: (index(*g), 0))]


def _route_out_shapes(t):
    return [jax.ShapeDtypeStruct((t, D_MODEL), f32),
            jax.ShapeDtypeStruct((META_ROWS, t), f32),
            jax.ShapeDtypeStruct((t // SORT_TILE * N_EXPERTS, 1), f32),
            jax.ShapeDtypeStruct((t, D_MODEL), bf16)]


def _gla_kernel(x_ref, ng_ref, wq_ref, wk_ref, wv_ref, wr_ref, wg_ref, wgu_ref, bg_ref,
                hg_ref, wo_ref, mng_ref, wrt_ref, br_ref,
                x1_ref, meta_ref, cnt_ref, hn_ref, st_ref, o_ref):
    ts = x_ref.shape[1]

    @pl.when(pl.program_id(1) == 0)
    def _():
        st_ref[...] = jnp.zeros_like(st_ref)

    x = x_ref[0]
    h = (x * _rms(x) * ng_ref[...]).astype(bf16)
    q = _dot(h, wq_ref[...]) * (GLA_DK ** -0.5)
    k = _dot(h, wk_ref[...])
    v = _dot(h, wv_ref[...]).astype(bf16)
    glr = _dot(h, wg_ref[...]).astype(bf16)
    r = _dot(h, wr_ref[...])
    gp = _dot(glr, wgu_ref[...]) + bg_ref[...]
    log_a = (jnp.minimum(gp, 0.0) - jnp.log(1.0 + jnp.exp(-jnp.abs(gp)))) * (1.0 / GLA_TAU)

    row = lax.broadcasted_iota(i32, (CHUNK, GLA_HK), 0)
    for c in range(ts // CHUNK):
        rs = slice(c * CHUNK, (c + 1) * CHUNK)
        b = log_a[rs]
        sh = 1
        while sh < CHUNK:
            b = b + jnp.where(row >= sh, pltpu.roll(b, sh, axis=0), 0.0)
            sh *= 2
        b_end = b[CHUNK - 1:CHUNK]
        kdec = (k[rs] * jnp.exp(b_end - b)).astype(bf16)
        decay = jnp.exp(b_end)
        qc = q[rs].astype(bf16)
        vc = v[rs]
        for hd in range(GLA_HEADS):
            ks = slice(hd * GLA_DK, (hd + 1) * GLA_DK)
            vs = slice(hd * GLA_DV, (hd + 1) * GLA_DV)
            st = st_ref[hd] * decay[:, ks] + _dot_tn(vc[:, vs], kdec[:, ks])
            st_ref[hd] = st
            o_ref[rs, vs] = _dot_nt(qc[:, ks], st.astype(bf16))

    gated = []
    for hd in range(GLA_HEADS):
        vs = slice(hd * GLA_DV, (hd + 1) * GLA_DV)
        oh = o_ref[:, vs]
        rh = r[:, vs]
        gated.append(oh * _rms(oh) * hg_ref[:, vs] * (rh / (1.0 + jnp.exp(-rh))))
    y = _dot(jnp.concatenate(gated, axis=-1).astype(bf16), wo_ref[...])
    x1 = x + y
    x1_ref[...] = x1
    _route(x1, mng_ref[...], wrt_ref[...], br_ref[...], meta_ref, cnt_ref, hn_ref)


def _gla_layer(x, norm_g, w_in, w_gate_up, b_gate, head_g, w_out, moe_norm_g, wrt, br, ts):
    bsz, seq, _ = x.shape
    t = bsz * seq
    wq = w_in[:, 0:GLA_HK].astype(bf16)
    wk = w_in[:, GLA_HK:2 * GLA_HK].astype(bf16)
    wv = w_in[:, 2 * GLA_HK:2 * GLA_HK + GLA_HV].astype(bf16)
    wr = w_in[:, 2 * GLA_HK + GLA_HV:2 * GLA_HK + 2 * GLA_HV].astype(bf16)
    wg = jnp.pad(w_in[:, 2 * GLA_HK + 2 * GLA_HV:], ((0, 0), (0, LANES - GLA_GATE_RANK))).astype(bf16)
    wgu = jnp.pad(w_gate_up, ((0, LANES - GLA_GATE_RANK), (0, 0))).astype(bf16)
    n_s = seq // ts
    args = (x, norm_g.reshape(1, D_MODEL), wq, wk, wv, wr, wg, wgu, b_gate.reshape(1, GLA_HK),
            head_g.reshape(1, GLA_HV), w_out.astype(bf16), moe_norm_g.reshape(1, D_MODEL), wrt, br)
    in_specs = [pl.BlockSpec((1, ts, D_MODEL), lambda b, s: (b, s, 0))]
    in_specs += [_const_spec(a.shape) for a in args[1:]]
    return pl.pallas_call(
        _gla_kernel,
        grid=(bsz, n_s),
        in_specs=in_specs,
        out_specs=_route_out_specs(ts, lambda b, s: b * n_s + s),
        out_shape=_route_out_shapes(t),
        scratch_shapes=[pltpu.VMEM((GLA_HEADS, GLA_DV, GLA_DK), f32),
                        pltpu.VMEM((ts, GLA_HV), f32)],
        compiler_params=pltpu.CompilerParams(
            dimension_semantics=("arbitrary", "arbitrary"), vmem_limit_bytes=VMEM_LIMIT),
        name="gla_mixer",
    )(*args)


def _n_expert_blocks(t):
    tiles = t // SORT_TILE
    worst = t * TOP_K + tiles * N_EXPERTS * (RUN_CHUNK - 1) + N_EXPERTS * (EXPERT_ROWS - 1)
    return -(-worst // EXPERT_ROWS)


def _zero_table_len(t):
    tail = _n_expert_blocks(t) * EXPERT_ROWS - t * TOP_K
    n = N_EXPERTS + -(-tail // ZERO_CHUNK) + 1
    return -(-(n + 1) // LANES) * LANES


def _cumsum(x, axis):
    x = jnp.moveaxis(x, axis, -1)
    n = x.shape[-1]
    upto = jnp.arange(n, dtype=i32)[:, None] <= jnp.arange(n, dtype=i32)[None, :]
    return jnp.moveaxis(jnp.sum(x[..., :, None] * upto.astype(i32), axis=-2), -1, axis)


def _flat_chunks(n_per, max_n):
    cum = _cumsum(n_per, -1)
    c = jnp.arange(max_n, dtype=i32)
    seg = jnp.minimum(jnp.sum((c[:, None] >= cum[..., None, :]).astype(i32), axis=-1), n_per.shape[-1] - 1)
    onehot = (seg[..., None] == jnp.arange(n_per.shape[-1], dtype=i32)).astype(i32)
    return onehot, c - _pick(onehot, cum - n_per), cum[..., -1]


def _pick(onehot, per_segment):
    return jnp.sum(onehot * per_segment[..., None, :], axis=-1)


def _plan(cnt, t):
    tiles = t // SORT_TILE
    cnt = cnt.reshape(tiles, N_EXPERTS).astype(i32)
    nch = (cnt + RUN_CHUNK - 1) // RUN_CHUNK
    run = nch * RUN_CHUNK
    counts = jnp.sum(run, axis=0)
    padded = ((counts + EXPERT_ROWS - 1) // EXPERT_ROWS) * EXPERT_ROWS
    pad_end = _cumsum(padded, 0)
    pad_start = pad_end - padded
    base = pad_start[None, :] + _cumsum(run, 0) - run
    n_pair = nch // 2
    off = _cumsum(run, 1) - run
    onehot, j, n_pairs = _flat_chunks(n_pair, MAX_PAIRS)
    pair_loc = _pick(onehot, off) + j * (2 * RUN_CHUNK)
    pair_glb = _pick(onehot, base) + j * (2 * RUN_CHUNK)
    onehot, _, n_odds = _flat_chunks(nch % 2, N_EXPERTS)
    odd_loc = _pick(onehot, off + n_pair * (2 * RUN_CHUNK))
    odd_glb = _pick(onehot, base + n_pair * (2 * RUN_CHUNK))
    units = jnp.sum(nch, axis=1)
    fill = jnp.zeros((tiles, TABLE_LEN - TAB_UNITS - 1), i32)
    table = jnp.concatenate([pair_loc, pair_glb, odd_loc, odd_glb, n_pairs[:, None], n_odds[:, None],
                             units[:, None], fill], axis=1).reshape(tiles, 1, TABLE_LEN)
    off_col = off.astype(f32).reshape(tiles, N_EXPERTS, 1)

    nb = _n_expert_blocks(t)
    n_blk = padded // EXPERT_ROWS
    n_big = n_blk // BIG_BLOCKS
    n_items = n_big + n_blk % BIG_BLOCKS
    ionehot, k, n_total = _flat_chunks(n_items, _n_work_items(t))
    k_big = _pick(ionehot, n_big)
    item_big = (k < k_big).astype(i32)
    item_start = _pick(ionehot, pad_start // EXPERT_ROWS) + jnp.where(
        k < k_big, BIG_BLOCKS * k, BIG_BLOCKS * k_big + k - k_big)
    icum = _cumsum(n_items, 0)
    item_first = jnp.concatenate([icum - n_items, n_total[None]])
    items = (item_first, item_start, item_big)

    region_end = jnp.concatenate([pad_start[1:], jnp.full((1,), nb * EXPERT_ROWS, i32)])
    zlen = region_end - (pad_start + counts)
    nz = (zlen + ZERO_CHUNK - 1) // ZERO_CHUNK
    zl = _zero_table_len(t)
    zonehot, zj, nztot = _flat_chunks(nz, zl - 1)
    zstart = _pick(zonehot, region_end) - (zj + 1) * ZERO_CHUNK
    ztable = jnp.concatenate([zstart, nztot[None]]).reshape(1, 1, zl)
    return table, off_col, items, ztable


ISSUE_UNROLL = 4


def _start_runs(tab, u, make):
    for rows, n, loc0, glb0 in ((2 * RUN_CHUNK, tab[u, 0, TAB_PAIRS], TAB_PAIR_LOC, TAB_PAIR_GLB),
                                (RUN_CHUNK, tab[u, 0, TAB_ODDS], TAB_ODD_LOC, TAB_ODD_GLB)):
        def start(c, priority):
            make(rows, tab[u, 0, loc0 + c], tab[u, 0, glb0 + c]).start(priority=priority)

        def four(k, carry):
            for q in range(ISSUE_UNROLL):
                start(ISSUE_UNROLL * k + q, q % 2)
            return carry
        lax.fori_loop(0, n // ISSUE_UNROLL, four, 0)
        lax.fori_loop(n - n % ISSUE_UNROLL, n, lambda c, carry: (start(c, 0), carry)[1], 0)


def _wait_runs(units, make):
    left = units
    for per_wait in (16, 2, 1):
        lax.fori_loop(0, left // per_wait,
                      lambda c, carry: (make(per_wait * RUN_CHUNK, 0, 0).wait(), carry)[1], 0)
        left = left % per_wait


def _dispatch_kernel(tab_ref, ztab_ref, hn_ref, meta_ref, offc_ref,
                     out_hbm, cmeta_ref, stage, zbuf, sems, zsem, prev_n):
    t = pl.program_id(0)
    slot = t % 2
    zl = ztab_ref.shape[2]

    def zero_copy(z):
        dst = pl.multiple_of(ztab_ref[0, 0, z], RUN_CHUNK)
        return pltpu.make_async_copy(zbuf, out_hbm.at[pl.ds(dst, ZERO_CHUNK), :], zsem)

    @pl.when(t == 0)
    def _():
        zbuf[...] = jnp.zeros_like(zbuf)
        nz = ztab_ref[0, 0, zl - 1]
        lax.fori_loop(0, nz, lambda z, c: (zero_copy(z).start(), c)[1], 0)
        lax.fori_loop(0, nz, lambda z, c: (zero_copy(z).wait(), c)[1], 0)

    tiles = hn_ref.shape[0] // SORT_TILE
    ids = lax.broadcasted_iota(i32, (N_EXPERTS, SORT_TILE), 0)
    before = (lax.broadcasted_iota(i32, (SORT_TILE, SORT_TILE), 0)
              < lax.broadcasted_iota(i32, (SORT_TILE, SORT_TILE), 1)).astype(bf16)
    p = lax.broadcasted_iota(i32, (LOCAL_CAP, SORT_TILE), 0).astype(f32)
    zero = jnp.zeros((1, SORT_TILE), f32)
    for u in range(tiles):
        cs = slice(u * SORT_TILE, (u + 1) * SORT_TILE)
        oh0 = ids == meta_ref[0:1, cs].astype(i32)
        oh1 = ids == meta_ref[1:2, cs].astype(i32)
        ahead = _dot(jnp.concatenate([oh0.astype(bf16), oh1.astype(bf16)], axis=0), before)
        c0, c1 = ahead[:N_EXPERTS], ahead[N_EXPERTS:]
        n0 = jnp.sum(oh0.astype(f32), axis=1, keepdims=True)
        offc = offc_ref[u]
        pos0 = jnp.sum(jnp.where(oh0, offc + c0, 0.0), axis=0, keepdims=True)
        pos1 = jnp.sum(jnp.where(oh1, offc + n0 + c1, 0.0), axis=0, keepdims=True)
        cmeta_ref[:, cs] = jnp.concatenate(
            [pos0, pos1, meta_ref[2:3, cs], meta_ref[3:4, cs], zero, zero, zero, zero], axis=0)
        perm = jnp.logical_or(p == pos0, p == pos1).astype(bf16)
        stage[slot, pl.ds(u * LOCAL_CAP, LOCAL_CAP), :] = _pack_rows(_dot(perm, hn_ref[cs, :]))

    def copy_from(s, u):
        def make(rows, loc, glb):
            src = pl.multiple_of(u * LOCAL_CAP + loc, RUN_CHUNK)
            return pltpu.make_async_copy(stage.at[s, pl.ds(src, rows), :],
                                         out_hbm.at[pl.ds(pl.multiple_of(glb, RUN_CHUNK), rows), :], sems.at[s])
        return make

    @pl.when(t > 0)
    def _():
        _wait_runs(prev_n[0], copy_from(1 - slot, 0))

    units = 0
    for u in range(tiles):
        _start_runs(tab_ref, u, copy_from(slot, u))
        units = units + tab_ref[u, 0, TAB_UNITS]
    prev_n[0] = units

    @pl.when(t == pl.num_programs(0) - 1)
    def _():
        _wait_runs(units, copy_from(slot, 0))


def _dispatch(hn, meta, table, off_col, ztable, ts):
    t = hn.shape[0]
    tiles = ts // SORT_TILE
    zl = ztable.shape[2]
    nb = _n_expert_blocks(t)
    return pl.pallas_call(
        _dispatch_kernel,
        grid=(t // ts,),
        in_specs=[
            pl.BlockSpec((tiles, 1, TABLE_LEN), lambda i: (i, 0, 0), memory_space=pltpu.SMEM),
            pl.BlockSpec((1, 1, zl), lambda i: (0, 0, 0), memory_space=pltpu.SMEM),
            pl.BlockSpec((ts, D_MODEL), lambda i: (i, 0)),
            pl.BlockSpec((META_ROWS, ts), lambda i: (0, i)),
            pl.BlockSpec((tiles, N_EXPERTS, 1), lambda i: (i, 0, 0)),
        ],
        out_specs=[pl.BlockSpec(memory_space=pl.ANY),
                   pl.BlockSpec((META_ROWS, ts), lambda i: (0, i))],
        out_shape=[jax.ShapeDtypeStruct((nb * EXPERT_ROWS, HALF), u32),
                   jax.ShapeDtypeStruct((META_ROWS, t), f32)],
        scratch_shapes=[pltpu.VMEM((2, tiles * LOCAL_CAP, HALF), u32),
                        pltpu.VMEM((ZERO_CHUNK, HALF), u32),
                        pltpu.SemaphoreType.DMA((2,)),
                        pltpu.SemaphoreType.DMA(()),
                        pltpu.SMEM((1,), i32)],
        compiler_params=pltpu.CompilerParams(
            dimension_semantics=("arbitrary",), vmem_limit_bytes=VMEM_LIMIT),
        name="moe_dispatch",
    )(table, ztable, hn, meta, off_col)


X_BUFFERS = 3
Y_BUFFERS = 2
BIG_BLOCKS = 4


def _n_work_items(t):
    return _n_expert_blocks(t) // BIG_BLOCKS + (BIG_BLOCKS - 1) * N_EXPERTS + 1


def _expert_kernel(ifirst_ref, istart_ref, ibig_ref, w1_ref, w3_ref, w2_ref, xs_hbm, ys_hbm,
                   xbuf, ybuf, xsem, ysem, w1b, w3b, w2b):
    e = pl.program_id(0)
    lo = ifirst_ref[e]
    hi = ifirst_ref[e + 1]
    total = ifirst_ref[N_EXPERTS]

    def rows_of(big):
        return (BIG_BLOCKS if big else 1) * EXPERT_ROWS

    def hbm_rows(ref, i, big):
        return ref.at[pl.ds(pl.multiple_of(istart_ref[i] * EXPERT_ROWS, EXPERT_ROWS), rows_of(big)), :]

    def x_copy(i, big):
        slot = i % X_BUFFERS
        return pltpu.make_async_copy(hbm_rows(xs_hbm, i, big),
                                     xbuf.at[slot, pl.ds(0, rows_of(big)), :], xsem.at[slot])

    def y_copy(i, big):
        slot = i % Y_BUFFERS
        return pltpu.make_async_copy(ybuf.at[slot, pl.ds(0, rows_of(big)), :],
                                     hbm_rows(ys_hbm, i, big), ysem.at[slot])

    def by_size(i, fn):
        @pl.when(ibig_ref[i] != 0)
        def _():
            fn(True)

        @pl.when(ibig_ref[i] == 0)
        def _():
            fn(False)

    @pl.when(e == 0)
    def _():
        for k in range(X_BUFFERS - 1):
            @pl.when(k < total)
            def _():
                by_size(k, lambda big: x_copy(k, big).start())

    @pl.when(hi > lo)
    def _():
        w1b[...] = w1_ref[0, 0].astype(bf16)
        w3b[...] = w3_ref[0, 0].astype(bf16)
        w2b[...] = w2_ref[0, 0].astype(bf16)

    def item(i, carry):
        ahead = i + X_BUFFERS - 1

        @pl.when(ahead < total)
        def _():
            by_size(ahead, lambda big: x_copy(ahead, big).start())

        def run(big):
            rows = rows_of(big)
            x_copy(i, big).wait()
            hn = _unpack_rows(xbuf[i % X_BUFFERS, pl.ds(0, rows), :])
            h1 = _dot(hn, w1b[...])
            h3 = _dot(hn, w3b[...])
            hid = ((h1 / (1.0 + jnp.exp(-h1))) * h3).astype(bf16)
            y = _dot(hid, w2b[...])

            @pl.when(i >= Y_BUFFERS)
            def _():
                by_size(i - Y_BUFFERS, lambda b: y_copy(i - Y_BUFFERS, b).wait())

            ybuf[i % Y_BUFFERS, pl.ds(0, rows), :] = _pack_rows(y.astype(bf16).astype(f32))
            y_copy(i, big).start()

        by_size(i, run)
        return carry

    lax.fori_loop(lo, hi, item, 0)

    @pl.when(e == pl.num_programs(0) - 1)
    def _():
        for k in range(1, Y_BUFFERS + 1):
            @pl.when(total >= k)
            def _():
                by_size(total - k, lambda big: y_copy(total - k, big).wait())


def _experts(xs, item_first, item_start, item_big, w1, w3, w2, layer):
    w_index = lambda e, *_: (layer, e, 0, 0)
    big_rows = BIG_BLOCKS * EXPERT_ROWS
    grid_spec = pltpu.PrefetchScalarGridSpec(
        num_scalar_prefetch=3,
        grid=(N_EXPERTS,),
        in_specs=[
            pl.BlockSpec((1, 1, D_MODEL, EXPERT_FF), w_index),
            pl.BlockSpec((1, 1, D_MODEL, EXPERT_FF), w_index),
            pl.BlockSpec((1, 1, EXPERT_FF, D_MODEL), w_index),
            pl.BlockSpec(memory_space=pl.ANY),
        ],
        out_specs=pl.BlockSpec(memory_space=pl.ANY),
        scratch_shapes=[pltpu.VMEM((X_BUFFERS, big_rows, HALF), u32),
                        pltpu.VMEM((Y_BUFFERS, big_rows, HALF), u32),
                        pltpu.SemaphoreType.DMA((X_BUFFERS,)),
                        pltpu.SemaphoreType.DMA((Y_BUFFERS,)),
                        pltpu.VMEM((D_MODEL, EXPERT_FF), bf16),
                        pltpu.VMEM((D_MODEL, EXPERT_FF), bf16),
                        pltpu.VMEM((EXPERT_FF, D_MODEL), bf16)],
    )
    return pl.pallas_call(
        _expert_kernel,
        grid_spec=grid_spec,
        out_shape=jax.ShapeDtypeStruct(xs.shape, u32),
        input_output_aliases={6: 0},
        compiler_params=pltpu.CompilerParams(
            dimension_semantics=("arbitrary",), vmem_limit_bytes=VMEM_LIMIT),
        name="moe_experts",
    )(item_first, item_start, item_big, w1, w3, w2, xs)


def _combine(x_ref, tab_ref, tab_next_ref, cmeta_ref, ys_hbm, ybuf, sems):
    i = pl.program_id(0)
    n_steps = pl.num_programs(0)
    tiles = x_ref.shape[0] // SORT_TILE
    slot = i % 2

    def copy_into(s, u):
        def make(rows, loc, glb):
            dst = pl.multiple_of(u * LOCAL_CAP + loc, RUN_CHUNK)
            return pltpu.make_async_copy(ys_hbm.at[pl.ds(pl.multiple_of(glb, RUN_CHUNK), rows), :],
                                         ybuf.at[s, pl.ds(dst, rows), :], sems.at[s])
        return make

    def fetch(tab, s):
        for u in range(tiles):
            _start_runs(tab, u, copy_into(s, u))

            def clear(c, carry):
                dst = pl.multiple_of(u * LOCAL_CAP + c * RUN_CHUNK, RUN_CHUNK)
                ybuf[s, pl.ds(dst, RUN_CHUNK), :] = jnp.zeros((RUN_CHUNK, HALF), u32)
                return carry
            lax.fori_loop(tab[u, 0, TAB_UNITS], MAX_CHUNKS, clear, 0)

    @pl.when(i == 0)
    def _():
        fetch(tab_ref, 0)

    @pl.when(i + 1 < n_steps)
    def _():
        fetch(tab_next_ref, 1 - slot)

    for u in range(tiles):
        _wait_runs(tab_ref[u, 0, TAB_UNITS], copy_into(slot, u))

    p = lax.broadcasted_iota(i32, (LOCAL_CAP, SORT_TILE), 0).astype(f32)
    parts = []
    for u in range(tiles):
        cs = slice(u * SORT_TILE, (u + 1) * SORT_TILE)
        pos0, pos1 = cmeta_ref[0:1, cs], cmeta_ref[1:2, cs]
        g = (jnp.where(p == pos0, cmeta_ref[2:3, cs], 0.0)
             + jnp.where(p == pos1, cmeta_ref[3:4, cs], 0.0))
        g_hi = g.astype(bf16)
        g_lo = (g - g_hi.astype(f32)).astype(bf16)
        y = _unpack_rows(ybuf[slot, pl.ds(u * LOCAL_CAP, LOCAL_CAP), :])
        both = _dot_tn(jnp.concatenate([g_hi, g_lo], axis=1), y)
        parts.append(x_ref[cs, :] + both[:SORT_TILE] + both[SORT_TILE:])
    return parts[0] if tiles == 1 else jnp.concatenate(parts, axis=0)


def _combine_specs(ts, n_steps):
    tiles = ts // SORT_TILE
    return [
        pl.BlockSpec((ts, D_MODEL), lambda i: (i, 0)),
        pl.BlockSpec((tiles, 1, TABLE_LEN), lambda i: (i, 0, 0), memory_space=pltpu.SMEM),
        pl.BlockSpec((tiles, 1, TABLE_LEN), lambda i: (jnp.minimum(i + 1, n_steps - 1), 0, 0),
                     memory_space=pltpu.SMEM),
        pl.BlockSpec((META_ROWS, ts), lambda i: (0, i)),
        pl.BlockSpec(memory_space=pl.ANY),
    ]


def _combine_scratch(ts):
    tiles = ts // SORT_TILE
    return [pltpu.VMEM((2, tiles * LOCAL_CAP, HALF), u32), pltpu.SemaphoreType.DMA((2,))]


def _gelu(z):
    return 0.5 * z * (1.0 + lax.erf(z * (2.0 ** -0.5)))


def _sgu_kernel(x_ref, ng_ref, wu_ref, wv_ref, lng_ref, lnb_ref,
                ws_ref, bs_ref, wo_ref, mng_ref, wrt_ref, br_ref,
                x3_ref, meta_ref, cnt_ref, hn_ref, vn_ref):
    ts = x_ref.shape[0]
    x2 = x_ref[...]
    h = (x2 * _rms(x2) * ng_ref[...]).astype(bf16)
    v = _gelu(_dot(h, wv_ref[...]))
    mu = jnp.mean(v, axis=-1, keepdims=True)
    vc = v - mu
    rstd = lax.rsqrt(jnp.mean(vc * vc, axis=-1, keepdims=True) + EPS)
    vn_ref[...] = (vc * rstd * lng_ref[...] + lnb_ref[...]).astype(bf16)

    pos = lax.broadcasted_iota(i32, (SGU_BLOCK, SGU_BLOCK), 0) // CHUNK
    src = lax.broadcasted_iota(i32, (SGU_BLOCK, SGU_BLOCK), 1) // CHUNK
    acc = x2
    for g in range(SGU_GROUPS):
        cs = slice(g * SGU_GC, (g + 1) * SGU_GC)
        ws = jnp.where(pos >= src, ws_ref[g], jnp.zeros((), bf16))
        u = _gelu(_dot(h, wu_ref[:, cs]))
        mixed = [_dot(ws, vn_ref[nb * SGU_BLOCK:(nb + 1) * SGU_BLOCK, cs]) + bs_ref[:, g:g + 1]
                 for nb in range(ts // SGU_BLOCK)]
        out = (u * jnp.concatenate(mixed, axis=0)).astype(bf16)
        acc = acc + _dot(out, wo_ref[cs, :])
    x3_ref[...] = acc
    _route(acc, mng_ref[...], wrt_ref[...], br_ref[...], meta_ref, cnt_ref, hn_ref)


def _sgu_layer(x2, norm_g, w_in, ln_g, ln_b, w_s, b_s, w_out, moe_norm_g, wrt, br, ts):
    t = x2.shape[0]
    n_steps = t // ts
    args = (norm_g.reshape(1, D_MODEL), w_in[:, :SGU_HALF].astype(bf16), w_in[:, SGU_HALF:].astype(bf16),
            ln_g.reshape(1, SGU_HALF), ln_b.reshape(1, SGU_HALF), w_s.astype(bf16), b_s.T,
            w_out.astype(bf16), moe_norm_g.reshape(1, D_MODEL), wrt, br)
    return pl.pallas_call(
        _sgu_kernel,
        grid=(n_steps,),
        in_specs=[pl.BlockSpec((ts, D_MODEL), lambda i: (i, 0))] + [_const_spec(a.shape) for a in args],
        out_specs=_route_out_specs(ts, lambda i: i),
        out_shape=_route_out_shapes(t),
        scratch_shapes=[pltpu.VMEM((ts, SGU_HALF), bf16)],
        compiler_params=pltpu.CompilerParams(
            dimension_semantics=("arbitrary",), vmem_limit_bytes=VMEM_LIMIT),
        name="sgu_mixer",
    )(x2, *args)


def _combine_kernel(x_ref, tab_ref, tab_next_ref, cmeta_ref, ys_hbm, out_ref, ybuf, sems):
    out_ref[...] = _combine(x_ref, tab_ref, tab_next_ref, cmeta_ref, ys_hbm, ybuf, sems)


def _final_kernel(x_ref, tab_ref, tab_next_ref, cmeta_ref, ys_hbm, ng_ref, out_ref, ybuf, sems):
    x = _combine(x_ref, tab_ref, tab_next_ref, cmeta_ref, ys_hbm, ybuf, sems)
    out_ref[...] = x * _rms(x) * ng_ref[...]


def _combine_layer(x, table, cmeta, ys, norm_g, ts):
    t = x.shape[0]
    n_steps = t // ts
    final = norm_g is not None
    extra = (norm_g.reshape(1, D_MODEL),) if final else ()
    return pl.pallas_call(
        _final_kernel if final else _combine_kernel,
        grid=(n_steps,),
        in_specs=_combine_specs(ts, n_steps) + [_const_spec(a.shape) for a in extra],
        out_specs=pl.BlockSpec((ts, D_MODEL), lambda i: (i, 0)),
        out_shape=jax.ShapeDtypeStruct((t, D_MODEL), f32),
        scratch_shapes=_combine_scratch(ts),
        compiler_params=pltpu.CompilerParams(
            dimension_semantics=("arbitrary",), vmem_limit_bytes=VMEM_LIMIT),
        name="final_norm" if final else "moe_combine",
    )(x, table, table, cmeta, ys, *extra)


def _moe(hn, meta, cnt, w1, w3, w2, layer, ts):
    table, off_col, items, ztable = _plan(cnt, hn.shape[0])
    sorted_x, cmeta = _dispatch(hn, meta, table, off_col, ztable, min(2 * ts, hn.shape[0]))
    return _experts(sorted_x, *items, w1, w3, w2, layer), table, cmeta


def _forward(x, gla_norm, gla_w_in, gla_w_gate_up, gla_b_gate, gla_head_g, gla_w_out, sgu_norm, sgu_w_in,
             sgu_ln_g, sgu_ln_b, sgu_w_s, sgu_b_s, sgu_w_out, moe_norm, moe_w_group, moe_b_group,
             moe_w_sub, moe_b_sub, moe_w1, moe_w3, moe_w2, final_norm, *, ts_gla, ts_sgu, ts_fin):
    wrt0, br0 = _router_params(moe_w_group[0], moe_b_group[0], moe_w_sub[0], moe_b_sub[0])
    wrt1, br1 = _router_params(moe_w_group[1], moe_b_group[1], moe_w_sub[1], moe_b_sub[1])
    x1, meta0, cnt0, hn0 = _gla_layer(x, gla_norm[0], gla_w_in[0], gla_w_gate_up[0], gla_b_gate[0], gla_head_g[0],
                                 gla_w_out[0], moe_norm[0], wrt0, br0, ts_gla)
    ys0, table0, cmeta0 = _moe(hn0, meta0, cnt0, moe_w1, moe_w3, moe_w2, 0, ts_fin)
    x2 = _combine_layer(x1, table0, cmeta0, ys0, None, ts_fin)
    x3, meta1, cnt1, hn1 = _sgu_layer(x2, sgu_norm[0], sgu_w_in[0], sgu_ln_g[0], sgu_ln_b[0],
                                 sgu_w_s[0], sgu_b_s[0], sgu_w_out[0], moe_norm[1], wrt1, br1, ts_sgu)
    ys1, table1, cmeta1 = _moe(hn1, meta1, cnt1, moe_w1, moe_w3, moe_w2, 1, ts_fin)
    out = _combine_layer(x3, table1, cmeta1, ys1, final_norm, ts_fin)
    return out.reshape(x.shape)


def kernel(x, gla_norm, gla_w_in, gla_w_gate_up, gla_b_gate, gla_head_g, gla_w_out, sgu_norm, sgu_w_in, sgu_ln_g, sgu_ln_b, sgu_w_s, sgu_b_s, sgu_w_out, moe_norm, moe_w_group, moe_b_group, moe_w_sub, moe_b_sub, moe_w1, moe_w3, moe_w2, final_norm):
    return _forward(x, gla_norm, gla_w_in, gla_w_gate_up, gla_b_gate, gla_head_g, gla_w_out, sgu_norm,
                    sgu_w_in, sgu_ln_g, sgu_ln_b, sgu_w_s, sgu_b_s, sgu_w_out, moe_norm, moe_w_group,
                    moe_b_group, moe_w_sub, moe_b_sub, moe_w1, moe_w3, moe_w2, final_norm,
                    ts_gla=1024, ts_sgu=1024, ts_fin=1024)
```
